```python
import math
import jax, jax.numpy as jnp
from jax import lax
import numpy as np

D_MODEL = 2048
BATCH = 1
SEQ = 8192
DEPTH = 2

N_MIXERS = 2
Q_BLOCK = 128
RMS_EPS = 1e-6
NEG_INF = -1e30

DIFF_HEADS = 8
DIFF_HEAD_DIM = 128
DIFF_V_DIM = 2 * DIFF_HEAD_DIM
DIFF_QK_WIDTH = DIFF_HEADS * 2 * DIFF_HEAD_DIM
DIFF_V_WIDTH = DIFF_HEADS * DIFF_V_DIM

MLA_HEADS = 16
MLA_Q_RANK = 512
MLA_KV_RANK = 512
MLA_NOPE = 128
MLA_ROPE = 64
MLA_V = 128
ROPE_THETA = 10000.0

REL_BUCKETS = 32
REL_MAX_DIST = 128

N_EXPERTS = 32
TOP_K = 4
D_FF = 2048
SWIGLU_LIMIT = 7.0
SWIGLU_ALPHA = 1.702
MOE_BLOCK = 128

N_DIFF_LAYERS = (DEPTH + 1) // 2
N_MLA_LAYERS = DEPTH // 2

kernel_name = "hybrid_diffattn_mla_moe_adaln"


def rmsnorm(x, g):
    xf = x.astype(jnp.float32)
    y = xf * lax.rsqrt(jnp.mean(xf * xf, axis=-1, keepdims=True) + RMS_EPS)
    return (y * g.astype(jnp.float32)).astype(x.dtype)


def t5_bucket(rel):
    n = jnp.maximum(rel, 0)
    max_exact = REL_BUCKETS // 2
    nf = jnp.maximum(n, 1).astype(jnp.float32)
    large = max_exact + (jnp.log(nf / max_exact) / math.log(REL_MAX_DIST / max_exact)
                         * (REL_BUCKETS - max_exact)).astype(jnp.int32)
    large = jnp.minimum(large, REL_BUCKETS - 1)
    return jnp.where(n < max_exact, n, large)


def rope(x, pos):
    half = x.shape[-1] // 2
    inv = ROPE_THETA ** (-jnp.arange(half, dtype=jnp.float32) / half)
    ang = pos.astype(jnp.float32)[:, :, None] * inv
    cos = jnp.cos(ang)[:, :, None, :]
    sin = jnp.sin(ang)[:, :, None, :]
    xf = x.astype(jnp.float32)
    x1, x2 = xf[..., :half], xf[..., half:]
    return jnp.concatenate([x1 * cos - x2 * sin, x1 * sin + x2 * cos], axis=-1).astype(x.dtype)


def block_rel(pos, q0, kend):
    pq = pos[:, q0:q0 + Q_BLOCK]
    pk = pos[:, :kend]
    rel = pq[:, :, None] - pk[:, None, :]
    return rel, rel >= 0


def causal_blocks(fn, seq):
    outs = [fn(b * Q_BLOCK, (b + 1) * Q_BLOCK) for b in range(seq // Q_BLOCK)]
    return jnp.concatenate(outs, axis=1)


def diff_attention(u, pos, w_qkv, lq1, lk1, lq2, lk2, sub_g, w_o, rel_bias, layer_idx):
    B, S, _ = u.shape
    qkv = u @ w_qkv
    q = qkv[..., :DIFF_QK_WIDTH].reshape(B, S, DIFF_HEADS, 2, DIFF_HEAD_DIM)
    k = qkv[..., DIFF_QK_WIDTH:2 * DIFF_QK_WIDTH].reshape(B, S, DIFF_HEADS, 2, DIFF_HEAD_DIM)
    v = qkv[..., 2 * DIFF_QK_WIDTH:].reshape(B, S, DIFF_HEADS, DIFF_V_DIM)
    lam_init = 0.8 - 0.6 * math.exp(-0.3 * (layer_idx - 1))
    lam = (jnp.exp(jnp.sum(lq1.astype(jnp.float32) * lk1.astype(jnp.float32)))
           - jnp.exp(jnp.sum(lq2.astype(jnp.float32) * lk2.astype(jnp.float32))) + lam_init)
    scale = DIFF_HEAD_DIM ** -0.5

    def block(q0, kend):
        qb = q[:, q0:q0 + Q_BLOCK]
        logits = jnp.einsum('bqhmd,bkhmd->bhmqk', qb, k[:, :kend]).astype(jnp.float32) * scale
        rel, mask = block_rel(pos, q0, kend)
        bias = rel_bias[t5_bucket(rel)].astype(jnp.float32)
        logits = logits + jnp.transpose(bias, (0, 3, 1, 2))[:, :, None]
        logits = jnp.where(mask[:, None, None], logits, NEG_INF)
        p = jax.nn.softmax(logits, axis=-1)
        attn = p[:, :, 0] - lam * p[:, :, 1]
        return jnp.einsum('bhqk,bkhe->bqhe', attn.astype(v.dtype), v[:, :kend])

    o = causal_blocks(block, S)
    o = rmsnorm(o, sub_g) * (1.0 - lam_init)
    return o.reshape(B, S, DIFF_V_WIDTH) @ w_o


def mla_attention(u, pos, w_in, q_norm_g, kv_norm_g, w_uq, w_ukv, w_o):
    B, S, _ = u.shape
    z = u @ w_in
    cq = rmsnorm(z[..., :MLA_Q_RANK], q_norm_g)
    ckv = rmsnorm(z[..., MLA_Q_RANK:MLA_Q_RANK + MLA_KV_RANK], kv_norm_g)
    kr = z[..., MLA_Q_RANK + MLA_KV_RANK:]
    q = (cq @ w_uq).reshape(B, S, MLA_HEADS, MLA_NOPE + MLA_ROPE)
    q_nope = q[..., :MLA_NOPE]
    q_rope = rope(q[..., MLA_NOPE:], pos)
    kv = (ckv @ w_ukv).reshape(B, S, MLA_HEADS, MLA_NOPE + MLA_V)
    k_nope = kv[..., :MLA_NOPE]
    v = kv[..., MLA_NOPE:]
    k_rope = rope(kr[:, :, None, :], pos)[:, :, 0]
    scale = (MLA_NOPE + MLA_ROPE) ** -0.5

    def block(q0, kend):
        qn = q_nope[:, q0:q0 + Q_BLOCK]
        qr = q_rope[:, q0:q0 + Q_BLOCK]
        logits = (jnp.einsum('bqhd,bkhd->bhqk', qn, k_nope[:, :kend])
                  + jnp.einsum('bqhr,bkr->bhqk', qr, k_rope[:, :kend])).astype(jnp.float32) * scale
        _, mask = block_rel(pos, q0, kend)
        logits = jnp.where(mask[:, None], logits, NEG_INF)
        p = jax.nn.softmax(logits, axis=-1)
        return jnp.einsum('bhqk,bkhd->bqhd', p.astype(v.dtype), v[:, :kend])

    o = causal_blocks(block, S)
    return o.reshape(B, S, MLA_HEADS * MLA_V) @ w_o


def moe(u, w_router, b_router, w_gu, b_gu, w_down, b_down):
    B, S, D = u.shape
    T = B * S
    xt = u.reshape(T, D)
    logits = (xt @ w_router + b_router).astype(jnp.float32)
    top_val, top_idx = lax.top_k(logits, TOP_K)
    gates = jax.nn.softmax(top_val, axis=-1)
    A = T * TOP_K
    e_flat = top_idx.reshape(A)
    tok_flat = jnp.arange(A, dtype=jnp.int32) // TOP_K
    g_flat = gates.reshape(A)
    order = jnp.argsort(e_flat)
    e_sorted = e_flat[order]
    counts = jnp.bincount(e_flat, length=N_EXPERTS)
    padded = ((counts + MOE_BLOCK - 1) // MOE_BLOCK) * MOE_BLOCK
    pad_end = jnp.cumsum(padded)
    pad_start = pad_end - padded
    start = jnp.cumsum(counts) - counts
    rank = jnp.arange(A, dtype=jnp.int32) - start[e_sorted]
    dest = pad_start[e_sorted] + rank
    n_blocks = -(-A // MOE_BLOCK) + N_EXPERTS
    P = n_blocks * MOE_BLOCK
    src_tok = jnp.full((P,), T, jnp.int32).at[dest].set(tok_flat[order])
    row_gate = jnp.zeros((P,), jnp.float32).at[dest].set(g_flat[order])
    xpad = jnp.concatenate([xt, jnp.zeros((1, D), xt.dtype)], axis=0)
    xbuf = xpad[src_tok].reshape(n_blocks, MOE_BLOCK, D)
    block_start = jnp.arange(n_blocks, dtype=jnp.int32) * MOE_BLOCK
    block_exp = jnp.minimum(jnp.searchsorted(pad_end, block_start, side='right'), N_EXPERTS - 1)

    def expert_block(args):
        xb, e = args
        gu = xb @ w_gu[e] + b_gu[e]
        g = jnp.minimum(gu[:, :D_FF], SWIGLU_LIMIT)
        up = jnp.clip(gu[:, D_FF:], -SWIGLU_LIMIT, SWIGLU_LIMIT)
        act = (up + 1.0) * (g * jax.nn.sigmoid(g * SWIGLU_ALPHA))
        return act @ w_down[e] + b_down[e]

    ybuf = lax.map(expert_block, (xbuf, block_exp)).reshape(P, D)
    y = jax.ops.segment_sum(ybuf * row_gate[:, None].astype(ybuf.dtype), src_tok,
                            num_segments=T + 1)[:T]
    return y.reshape(B, S, D)


def setup_inputs(seed: int = 0) -> dict:
    key = jax.random.key(seed)
    ks = jax.random.split(key, 32)
    f32 = jnp.float32
    D = D_MODEL

    def nrm(k, shape, s):
        return jax.random.normal(k, shape, f32) * s

    return {
        "x": nrm(ks[0], (BATCH, SEQ, D), 1.0),
        "c": nrm(ks[1], (BATCH, D), 1.0),
        "positions": jnp.broadcast_to(jnp.arange(SEQ, dtype=jnp.int32), (BATCH, SEQ)),
        "ada_w": nrm(ks[2], (DEPTH, D, 6 * D), 0.5 * D ** -0.5),
        "ada_b": nrm(ks[3], (DEPTH, 6 * D), 0.02),
        "norm1_g": 1.0 + nrm(ks[4], (DEPTH, D), 0.02),
        "norm2_g": 1.0 + nrm(ks[5], (DEPTH, D), 0.02),
        "final_g": 1.0 + nrm(ks[6], (D,), 0.02),
        "rel_bias": nrm(ks[7], (REL_BUCKETS, DIFF_HEADS), 0.3),
        "diff_w_qkv": nrm(ks[8], (N_DIFF_LAYERS, D, 2 * DIFF_QK_WIDTH + DIFF_V_WIDTH), D ** -0.5),
        "diff_lq1": nrm(ks[9], (N_DIFF_LAYERS, DIFF_HEAD_DIM), 0.1),
        "diff_lk1": nrm(ks[10], (N_DIFF_LAYERS, DIFF_HEAD_DIM), 0.1),
        "diff_lq2": nrm(ks[11], (N_DIFF_LAYERS, DIFF_HEAD_DIM), 0.1),
        "diff_lk2": nrm(ks[12], (N_DIFF_LAYERS, DIFF_HEAD_DIM), 0.1),
        "diff_sub_g": 1.0 + nrm(ks[13], (N_DIFF_LAYERS, DIFF_V_DIM), 0.02),
        "diff_w_o": nrm(ks[14], (N_DIFF_LAYERS, DIFF_V_WIDTH, D), DIFF_V_WIDTH ** -0.5),
        "mla_w_in": nrm(ks[15], (N_MLA_LAYERS, D, MLA_Q_RANK + MLA_KV_RANK + MLA_ROPE), D ** -0.5),
        "mla_q_norm_g": 1.0 + nrm(ks[16], (N_MLA_LAYERS, MLA_Q_RANK), 0.02),
        "mla_kv_norm_g": 1.0 + nrm(ks[17], (N_MLA_LAYERS, MLA_KV_RANK), 0.02),
        "mla_w_uq": nrm(ks[18], (N_MLA_LAYERS, MLA_Q_RANK, MLA_HEADS * (MLA_NOPE + MLA_ROPE)), MLA_Q_RANK ** -0.5),
        "mla_w_ukv": nrm(ks[19], (N_MLA_LAYERS, MLA_KV_RANK, MLA_HEADS * (MLA_NOPE + MLA_V)), MLA_KV_RANK ** -0.5),
        "mla_w_o": nrm(ks[20], (N_MLA_LAYERS, MLA_HEADS * MLA_V, D), (MLA_HEADS * MLA_V) ** -0.5),
        "router_w": nrm(ks[21], (DEPTH, D, N_EXPERTS), D ** -0.5),
        "router_b": nrm(ks[22], (DEPTH, N_EXPERTS), 0.01),
        "exp_w_gu": nrm(ks[23], (DEPTH, N_EXPERTS, D, 2 * D_FF), D ** -0.5),
        "exp_b_gu": nrm(ks[24], (DEPTH, N_EXPERTS, 2 * D_FF), 0.01),
        "exp_w_down": nrm(ks[25], (DEPTH, N_EXPERTS, D_FF, D), D_FF ** -0.5),
        "exp_b_down": nrm(ks[26], (DEPTH, N_EXPERTS, D), 0.01),
    }


def reference(x, c, positions, ada_w, ada_b, norm1_g, norm2_g, final_g, rel_bias,
              diff_w_qkv, diff_lq1, diff_lk1, diff_lq2, diff_lk2, diff_sub_g, diff_w_o,
              mla_w_in, mla_q_norm_g, mla_kv_norm_g, mla_w_uq, mla_w_ukv, mla_w_o,
              router_w, router_b, exp_w_gu, exp_b_gu, exp_w_down, exp_b_down):
    h = x
    cs = jax.nn.silu(c)
    for i in range(DEPTH):
        mod = (cs @ ada_w[i] + ada_b[i])[:, None, :]
        sh1, sc1, g1, sh2, sc2, g2 = jnp.split(mod, 6, axis=-1)
        u = rmsnorm(h, norm1_g[i]) * (1.0 + sc1) + sh1
        j = i // N_MIXERS
        if i % N_MIXERS == 0:
            m = diff_attention(u, positions, diff_w_qkv[j], diff_lq1[j], diff_lk1[j],
                               diff_lq2[j], diff_lk2[j], diff_sub_g[j], diff_w_o[j],
                               rel_bias, i + 1)
        else:
            m = mla_attention(u, positions, mla_w_in[j], mla_q_norm_g[j], mla_kv_norm_g[j],
                              mla_w_uq[j], mla_w_ukv[j], mla_w_o[j])
        h = h + g1 * m
        u = rmsnorm(h, norm2_g[i]) * (1.0 + sc2) + sh2
        h = h + g2 * moe(u, router_w[i], router_b[i], exp_w_gu[i], exp_b_gu[i],
                         exp_w_down[i], exp_b_down[i])
    return rmsnorm(h, final_g)
```

```python
import dataclasses
import functools
import math

import jax
import jax.numpy as jnp
from jax import lax
from jax.experimental import pallas as pl
from jax.experimental.pallas import tpu as pltpu

F32 = jnp.float32
BF16 = jnp.bfloat16
I32 = jnp.int32

RMS_EPS = 1e-6
NEG_INF = -1e30
LOG2E = math.log2(math.e)
ROPE_THETA = 10000.0
REL_BUCKETS = 32
REL_MAX_DIST = 128
REL_FAR = 113
SWIGLU_LIMIT = 7.0
SWIGLU_ALPHA = 1.702
Q_BLOCK = 128
MOE_BLOCK = 128
LANE = 128
V7X_VMEM_BYTES = 64 * 1024 * 1024


@dataclasses.dataclass(frozen=True)
class Cfg:
    d_model: int = 2048
    seq: int = 8192
    depth: int = 2
    diff_heads: int = 8
    diff_head_dim: int = 128
    mla_heads: int = 16
    mla_q_rank: int = 512
    mla_kv_rank: int = 512
    mla_nope: int = 128
    mla_rope: int = 64
    mla_v: int = 128
    n_experts: int = 32
    top_k: int = 4
    d_ff: int = 2048
    ada_tn: int = 1024
    lin_tm: int = 1024
    lin_tn: int = 1024
    attn_t: int = 512
    router_tm: int = 256
    moe_rows: int = 1280
    moe_tf: int = 256
    comb_tm: int = 128


def _vmem_limit(nbytes):
    return int(min(nbytes * 1.25 + (6 << 20), V7X_VMEM_BYTES - (6 << 20)))


def _cparams(sem, nbytes):
    return pltpu.CompilerParams(dimension_semantics=sem, vmem_limit_bytes=_vmem_limit(nbytes))


def _ada_kernel(c_ref, w_ref, b_ref, o_ref):
    c = c_ref[...]
    cs = c * jax.nn.sigmoid(c)
    o_ref[...] = jnp.sum(w_ref[...] * cs, axis=0, keepdims=True) + b_ref[...]


def ada_modulation(c_col, ada_w, ada_b, cfg):
    depth, d, n = ada_w.shape
    tn = min(cfg.ada_tn, n)
    assert n % tn == 0
    return pl.pallas_call(
        _ada_kernel,
        grid=(depth, n // tn),
        in_specs=[
            pl.BlockSpec((d, 1), lambda l, j: (0, 0)),
            pl.BlockSpec((None, d, tn), lambda l, j: (l, 0, j)),
            pl.BlockSpec((None, 1, tn), lambda l, j: (l, 0, j)),
        ],
        out_specs=pl.BlockSpec((None, 1, tn), lambda l, j: (l, 0, j)),
        out_shape=jax.ShapeDtypeStruct((depth, 1, n), F32),
        compiler_params=_cparams(("arbitrary", "arbitrary"), 2 * d * tn * 4 + d * tn * 4),
        name="ada_modulation",
    )(c_col, ada_w, ada_b.reshape(depth, 1, n))


def _rmsnorm_f32(x, g):
    return x * lax.rsqrt(jnp.mean(x * x, axis=-1, keepdims=True) + RMS_EPS) * g


def _norm_linear_kernel(x_ref, g_ref, sc_ref, sh_ref, w_ref, cs_ref, o_ref, xn_ref, *, modulate):
    @pl.when(pl.program_id(1) == 0)
    def _():
        y = _rmsnorm_f32(x_ref[...].astype(F32), g_ref[...])
        if modulate:
            y = y * (1.0 + sc_ref[...]) + sh_ref[...]
        xn_ref[...] = y.astype(BF16)

    acc = jnp.dot(xn_ref[...], w_ref[...], preferred_element_type=F32)
    o_ref[...] = (acc * cs_ref[...]).astype(o_ref.dtype)


def norm_linear(x, x_col_block, k, g, sc, sh, w_bf, col_scale, out_dtype, cfg, *, modulate, name):
    m = x.shape[0]
    n = w_bf.shape[1]
    tm = min(cfg.lin_tm, m)
    tn = n if n <= cfg.lin_tn or n % cfg.lin_tn else cfg.lin_tn
    assert m % tm == 0 and n % tn == 0
    out_b = jnp.dtype(out_dtype).itemsize
    est = 2 * tm * k * 4 + tm * k * 2 + 2 * k * tn * 2 + 2 * tm * tn * out_b + tm * k * 4
    return pl.pallas_call(
        functools.partial(_norm_linear_kernel, modulate=modulate),
        grid=(m // tm, n // tn),
        in_specs=[
            pl.BlockSpec((tm, k), lambda i, j: (i, x_col_block)),
            pl.BlockSpec((1, k), lambda i, j: (0, 0)),
            pl.BlockSpec((1, k), lambda i, j: (0, 0)),
            pl.BlockSpec((1, k), lambda i, j: (0, 0)),
            pl.BlockSpec((k, tn), lambda i, j: (0, j)),
            pl.BlockSpec((1, tn), lambda i, j: (0, j)),
        ],
        out_specs=pl.BlockSpec((tm, tn), lambda i, j: (i, j)),
        out_shape=jax.ShapeDtypeStruct((m, n), out_dtype),
        scratch_shapes=[pltpu.VMEM((tm, k), BF16)],
        compiler_params=_cparams(("arbitrary", "arbitrary"), est),
        name=name,
    )(x, g, sc, sh, w_bf, col_scale)


def _linear_res_kernel(a_ref, w_ref, h_ref, g_ref, o_ref):
    acc = jnp.dot(a_ref[...], w_ref[...], preferred_element_type=F32)
    o_ref[...] = h_ref[...] + g_ref[...] * acc


def linear_residual(a_bf, w_bf, h, gate, cfg, *, name):
    m, k = a_bf.shape
    n = w_bf.shape[1]
    tm = min(cfg.lin_tm, m)
    tn = min(cfg.lin_tn, n)
    assert m % tm == 0 and n % tn == 0
    est = 2 * tm * k * 2 + 2 * k * tn * 2 + 4 * tm * tn * 4
    return pl.pallas_call(
        _linear_res_kernel,
        grid=(m // tm, n // tn),
        in_specs=[
            pl.BlockSpec((tm, k), lambda i, j: (i, 0)),
            pl.BlockSpec((k, tn), lambda i, j: (0, j)),
            pl.BlockSpec((tm, tn), lambda i, j: (i, j)),
            pl.BlockSpec((1, tn), lambda i, j: (0, j)),
        ],
        out_specs=pl.BlockSpec((tm, tn), lambda i, j: (i, j)),
        out_shape=jax.ShapeDtypeStruct((m, n), F32),
        compiler_params=_cparams(("arbitrary", "arbitrary"), est),
        name=name,
    )(a_bf, w_bf, h, gate)


KIND_SKIP, KIND_PLAIN, KIND_DIAG, KIND_OFFDIAG, KIND_GENERAL = 0, 1, 2, 3, 4


def _block_kinds(pos, t, far_dist, toeplitz):
    s = pos.shape[0]
    nb = s // t
    pb = pos.reshape(nb, t)
    pmin, pmax = pb.min(axis=1), pb.max(axis=1)
    consecutive = jnp.all(pb == pb[:, :1] + jnp.arange(t, dtype=pos.dtype)[None, :], axis=1)
    qi = jnp.arange(nb)[:, None]
    ki = jnp.arange(nb)[None, :]
    gap = pmin[:, None] - pmax[None, :]
    plain = (ki < qi) & (gap >= (far_dist if far_dist else 0))
    kinds = jnp.where(plain, KIND_PLAIN, KIND_GENERAL)
    if toeplitz:
        both = consecutive[:, None] & consecutive[None, :]
        d = pb[:, 0][:, None] - pb[:, 0][None, :]
        kinds = jnp.where(~plain & both & (ki == qi - 1) & (d == t), KIND_OFFDIAG, kinds)
        kinds = jnp.where(both & (ki == qi) & (d == 0), KIND_DIAG, kinds)
    kinds = jnp.where(ki > qi, KIND_SKIP, kinds)
    return kinds.reshape(-1).astype(I32)


def _t5_bias_minus_last(rel, rb_ref, h):
    n = jnp.maximum(rel, 0)
    max_exact = REL_BUCKETS // 2
    nf = jnp.maximum(n, 1).astype(F32)
    large = max_exact + (jnp.log(nf / max_exact) / math.log(REL_MAX_DIST / max_exact)
                         * (REL_BUCKETS - max_exact)).astype(I32)
    large = jnp.minimum(large, REL_BUCKETS - 1)
    bucket = jnp.where(n < max_exact, n, large)
    last = rb_ref[REL_BUCKETS - 1, h]
    out = jnp.zeros(rel.shape, F32)
    for b in range(REL_BUCKETS - 1):
        out = jnp.where(bucket == b, rb_ref[b, h] - last, out)
    return out


def _bias_tile_kernel(rb_ref, o_ref, *, t):
    kind = pl.program_id(0)
    h = pl.program_id(1)
    rel = kind * t + lax.broadcasted_iota(I32, (t, t), 0) - lax.broadcasted_iota(I32, (t, t), 1)
    bias = _t5_bias_minus_last(rel, rb_ref, h) * LOG2E
    o_ref[...] = jnp.where(rel >= 0, bias, NEG_INF)


def diff_bias_tiles(rel_bias, heads, t):
    return pl.pallas_call(
        functools.partial(_bias_tile_kernel, t=t),
        grid=(2, heads),
        in_specs=[pl.BlockSpec(memory_space=pltpu.SMEM)],
        out_specs=pl.BlockSpec((None, None, t, t), lambda k, h: (k, h, 0, 0)),
        out_shape=jax.ShapeDtypeStruct((2, heads, t, t), F32),
        compiler_params=_cparams(("arbitrary", "arbitrary"), 8 * t * t * 4),
        name="diff_bias_tiles",
    )(rel_bias)


def _structural_ok(qi, ki, t):
    q_idx = qi * t + lax.broadcasted_iota(I32, (t, t), 0)
    k_idx = ki * t + lax.broadcasted_iota(I32, (t, t), 1)
    return k_idx < ((q_idx // Q_BLOCK) + 1) * Q_BLOCK


def _online_softmax_step(s, v, m_ref, l_ref, acc_ref):
    m_prev = m_ref[...]
    m_new = jnp.maximum(m_prev, jnp.max(s, axis=-1, keepdims=True))
    alpha = jnp.exp2(m_prev - m_new)
    p = jnp.exp2(s - m_new)
    l_ref[...] = alpha * l_ref[...] + jnp.sum(p, axis=-1, keepdims=True)
    acc_ref[...] = alpha * acc_ref[...] + jnp.dot(p.astype(BF16), v, preferred_element_type=F32)
    m_ref[...] = m_new


def _diff_attn_kernel(kind_ref, q_ref, k_ref, v_ref, bt_ref, pq_ref, pk_ref, rb_ref, lam_ref, sg_ref, o_ref,
                      m_ref, l_ref, acc_ref, *, t, nk, hd, out_scale):
    h, qi, ki = pl.program_id(0), pl.program_id(1), pl.program_id(2)
    kind = kind_ref[qi * nk + ki]

    @pl.when(ki == 0)
    def _():
        m_ref[...] = jnp.full(m_ref.shape, -jnp.inf, F32)
        l_ref[...] = jnp.zeros(l_ref.shape, F32)
        acc_ref[...] = jnp.zeros(acc_ref.shape, F32)

    def step(adjust):
        v = v_ref[...]
        for mp in range(2):
            q = q_ref[:, mp * hd:(mp + 1) * hd]
            k = k_ref[:, mp * hd:(mp + 1) * hd]
            s = lax.dot_general(q, k, (((1,), (1,)), ((), ())), preferred_element_type=F32)
            _online_softmax_step(adjust(s), v, m_ref.at[mp], l_ref.at[mp], acc_ref.at[mp])

    @pl.when(kind == KIND_PLAIN)
    def _():
        step(lambda s: s)

    @pl.when((kind == KIND_DIAG) | (kind == KIND_OFFDIAG))
    def _():
        step(lambda s: s + bt_ref[...])

    @pl.when(kind == KIND_GENERAL)
    def _():
        rel = pq_ref[...] - pk_ref[...]
        add = _t5_bias_minus_last(rel, rb_ref, h) * LOG2E
        struct = _structural_ok(qi, ki, t)

        def adjust(s):
            s = jnp.where(rel >= 0, s + add, NEG_INF)
            return jnp.where(struct, s, -jnp.inf)
        step(adjust)

    @pl.when(ki == qi)
    def _():
        o = acc_ref[0] / l_ref[0] - lam_ref[0] * (acc_ref[1] / l_ref[1])
        o_ref[...] = (_rmsnorm_f32(o, sg_ref[...]) * out_scale).astype(o_ref.dtype)


def diff_attention(qkv, bias_tiles, kinds, pos, rel_bias, lam, sub_g, out_scale, cfg):
    s = qkv.shape[0]
    t = min(cfg.attn_t, s)
    nq = s // t
    heads, hd = cfg.diff_heads, cfg.diff_head_dim
    w = 2 * hd
    kv_map = lambda h, qi, ki, kr: (jnp.minimum(ki, qi), 0)

    def bt_map(h, qi, ki, kr):
        return (jnp.where(kr[qi * nq + ki] == KIND_OFFDIAG, 1, 0), h, 0, 0)

    grid_spec = pltpu.PrefetchScalarGridSpec(
        num_scalar_prefetch=1,
        grid=(heads, nq, nq),
        in_specs=[
            pl.BlockSpec((t, w), lambda h, qi, ki, kr: (qi, h)),
            pl.BlockSpec((t, w), lambda h, qi, ki, kr: (jnp.minimum(ki, qi), heads + h)),
            pl.BlockSpec((t, w), lambda h, qi, ki, kr: (jnp.minimum(ki, qi), 2 * heads + h)),
            pl.BlockSpec((None, None, t, t), bt_map),
            pl.BlockSpec((t, 1), lambda h, qi, ki, kr: (qi, 0)),
            pl.BlockSpec((1, t), lambda h, qi, ki, kr: (0, jnp.minimum(ki, qi))),
            pl.BlockSpec(memory_space=pltpu.SMEM),
            pl.BlockSpec(memory_space=pltpu.SMEM),
            pl.BlockSpec((1, w), lambda h, qi, ki, kr: (0, 0)),
        ],
        out_specs=pl.BlockSpec((t, w), lambda h, qi, ki, kr: (qi, h)),
        scratch_shapes=[pltpu.VMEM((2, t, 1), F32), pltpu.VMEM((2, t, 1), F32), pltpu.VMEM((2, t, w), F32)],
    )
    del kv_map
    est = 2 * (3 * t * w * 2 + t * t * 4 + t * w * 2) + 2 * t * w * 4 + 8 * t * t * 4
    return pl.pallas_call(
        functools.partial(_diff_attn_kernel, t=t, nk=nq, hd=hd, out_scale=out_scale),
        grid_spec=grid_spec,
        out_shape=jax.ShapeDtypeStruct((s, heads * w), BF16),
        compiler_params=_cparams(("arbitrary", "arbitrary", "arbitrary"), est),
        name="diff_attention",
    )(kinds, qkv, qkv, qkv, bias_tiles, pos.reshape(s, 1), pos.reshape(1, s), rel_bias, lam, sub_g)


def _rope_slot(x, c, s1, s2):
    return x * c + pltpu.roll(x, LANE - 32, 1) * s1 + pltpu.roll(x, 32, 1) * s2


def _rope_tables(pos, rope_dim):
    half = rope_dim // 2
    inv = ROPE_THETA ** (-jnp.arange(half, dtype=F32) / half)
    ang = pos.astype(F32)[:, None] * inv
    cos, sin = jnp.cos(ang), jnp.sin(ang)
    z = jnp.zeros_like(cos)
    pad = jnp.zeros((pos.shape[0], LANE - 2 * half), F32)
    c = jnp.concatenate([cos, cos, pad], axis=1)
    s1 = jnp.concatenate([-sin, z, pad], axis=1)
    s2 = jnp.concatenate([z, sin, pad], axis=1)
    return c, s1, s2


def _rope_rows_kernel(x_ref, c_ref, s1_ref, s2_ref, o_ref):
    o_ref[...] = _rope_slot(x_ref[...].astype(F32), c_ref[...], s1_ref[...], s2_ref[...]).astype(o_ref.dtype)


def rope_rows(x, col_block, tables, cfg):
    m = x.shape[0]
    tm = min(cfg.lin_tm, m)
    row = pl.BlockSpec((tm, LANE), lambda i: (i, 0))
    return pl.pallas_call(
        _rope_rows_kernel,
        grid=(m // tm,),
        in_specs=[pl.BlockSpec((tm, LANE), lambda i: (i, col_block)), row, row, row],
        out_specs=row,
        out_shape=jax.ShapeDtypeStruct((m, LANE), BF16),
        compiler_params=_cparams(("arbitrary",), 10 * tm * LANE * 4),
        name="mla_rope_key",
    )(x, *tables)


def _mla_attn_kernel(kind_ref, q_ref, kn_ref, kr_ref, v_ref, c_ref, s1_ref, s2_ref, pq_ref, pk_ref, o_ref,
                     qs_ref, kc_ref, m_ref, l_ref, acc_ref, *, t, nk):
    qi, ki = pl.program_id(1), pl.program_id(2)
    kind = kind_ref[qi * nk + ki]

    @pl.when(ki == 0)
    def _():
        m_ref[...] = jnp.full(m_ref.shape, -jnp.inf, F32)
        l_ref[...] = jnp.zeros(l_ref.shape, F32)
        acc_ref[...] = jnp.zeros(acc_ref.shape, F32)
        qs_ref[:, :LANE] = q_ref[:, :LANE]
        qr = _rope_slot(q_ref[:, LANE:].astype(F32), c_ref[...], s1_ref[...], s2_ref[...])
        qs_ref[:, LANE:] = qr.astype(BF16)

    def step(adjust):
        kc_ref[:, :LANE] = kn_ref[...]
        kc_ref[:, LANE:] = kr_ref[...]
        s = lax.dot_general(qs_ref[...], kc_ref[...], (((1,), (1,)), ((), ())), preferred_element_type=F32)
        _online_softmax_step(adjust(s), v_ref[...], m_ref, l_ref, acc_ref)

    @pl.when(kind == KIND_PLAIN)
    def _():
        step(lambda s: s)

    @pl.when(kind == KIND_GENERAL)
    def _():
        rel = pq_ref[...] - pk_ref[...]
        struct = _structural_ok(qi, ki, t)

        def adjust(s):
            s = jnp.where(rel >= 0, s, NEG_INF)
            return jnp.where(struct, s, -jnp.inf)
        step(adjust)

    @pl.when(ki == qi)
    def _():
        o_ref[...] = (acc_ref[...] / l_ref[...]).astype(o_ref.dtype)


def mla_attention(q_cat, kv, kr_rot, tables, kinds, pos, cfg):
    s = q_cat.shape[0]
    t = min(cfg.attn_t, s)
    nq = s // t
    heads = cfg.mla_heads
    kblk = lambda col: (lambda h, qi, ki, kr: (jnp.minimum(ki, qi), col(h)))
    qrow = pl.BlockSpec((t, LANE), lambda h, qi, ki, kr: (qi, 0))
    grid_spec = pltpu.PrefetchScalarGridSpec(
        num_scalar_prefetch=1,
        grid=(heads, nq, nq),
        in_specs=[
            pl.BlockSpec((t, 2 * LANE), lambda h, qi, ki, kr: (qi, h)),
            pl.BlockSpec((t, LANE), kblk(lambda h: 2 * h)),
            pl.BlockSpec((t, LANE), kblk(lambda h: 0)),
            pl.BlockSpec((t, LANE), kblk(lambda h: 2 * h + 1)),
            qrow, qrow, qrow,
            pl.BlockSpec((t, 1), lambda h, qi, ki, kr: (qi, 0)),
            pl.BlockSpec((1, t), lambda h, qi, ki, kr: (0, jnp.minimum(ki, qi))),
        ],
        out_specs=pl.BlockSpec((t, LANE), lambda h, qi, ki, kr: (qi, h)),
        scratch_shapes=[pltpu.VMEM((t, 2 * LANE), BF16), pltpu.VMEM((t, 2 * LANE), BF16),
                        pltpu.VMEM((t, 1), F32), pltpu.VMEM((t, 1), F32), pltpu.VMEM((t, LANE), F32)],
    )
    est = 2 * (t * 256 * 2 + 3 * t * 128 * 2 + 3 * t * 128 * 4 + t * 128 * 2) + 8 * t * t * 4
    return pl.pallas_call(
        functools.partial(_mla_attn_kernel, t=t, nk=nq),
        grid_spec=grid_spec,
        out_shape=jax.ShapeDtypeStruct((s, heads * cfg.mla_v), BF16),
        compiler_params=_cparams(("arbitrary", "arbitrary", "arbitrary"), est),
        name="mla_attention",
    )(kinds, q_cat, kv, kr_rot, kv, *tables, pos.reshape(s, 1), pos.reshape(1, s))


def _router_kernel(h_ref, g_ref, sc_ref, sh_ref, wr_ref, br_ref, u_ref, idx_ref, gate_ref, *, top_k):
    u = _rmsnorm_f32(h_ref[...], g_ref[...]) * (1.0 + sc_ref[...]) + sh_ref[...]
    u_ref[...] = u
    logits = jnp.dot(u, wr_ref[...], preferred_element_type=F32, precision=lax.Precision.HIGHEST) + br_ref[...]
    n_e = logits.shape[-1]
    lane = lax.broadcasted_iota(I32, logits.shape, 1)
    vals, idxs = [], []
    cur = logits
    for _ in range(top_k):
        mx = jnp.max(cur, axis=-1, keepdims=True)
        ix = jnp.min(jnp.where(cur == mx, lane, n_e), axis=-1, keepdims=True)
        vals.append(mx)
        idxs.append(ix)
        cur = jnp.where(lane == ix, -jnp.inf, cur)
    v = jnp.concatenate(vals, axis=1)
    e = jnp.exp(v - vals[0])
    gate_ref[...] = e / jnp.sum(e, axis=-1, keepdims=True)
    idx_ref[...] = jnp.concatenate(idxs, axis=1)


def router(h, g, sc, sh, w_router, b_router, cfg):
    t, d = h.shape
    e = w_router.shape[1]
    tm = min(cfg.router_tm, t)
    vec = pl.BlockSpec((1, d), lambda i: (0, 0))
    return pl.pallas_call(
        functools.partial(_router_kernel, top_k=cfg.top_k),
        grid=(t // tm,),
        in_specs=[pl.BlockSpec((tm, d), lambda i: (i, 0)), vec, vec, vec,
                  pl.BlockSpec((d, e), lambda i: (0, 0)), pl.BlockSpec((1, e), lambda i: (0, 0))],
        out_specs=[pl.BlockSpec((tm, d), lambda i: (i, 0)),
                   pl.BlockSpec((tm, cfg.top_k), lambda i: (i, 0)),
                   pl.BlockSpec((tm, cfg.top_k), lambda i: (i, 0))],
        out_shape=[jax.ShapeDtypeStruct((t, d), F32),
                   jax.ShapeDtypeStruct((t, cfg.top_k), I32),
                   jax.ShapeDtypeStruct((t, cfg.top_k), F32)],
        compiler_params=_cparams(("arbitrary",), 6 * tm * d * 4 + 2 * d * LANE * 4),
        name="moe_router",
    )(h, g, sc, sh, w_router, b_router)


def _routing_tables(idx, cfg):
    t, k = idx.shape
    a = t * k
    e = cfg.n_experts
    r = cfg.moe_rows
    p_rows = a + e * MOE_BLOCK
    g_max = e + a // r + 1
    e_flat = idx.reshape(a)
    onehot = (e_flat[:, None] == jnp.arange(e, dtype=I32)[None, :]).astype(I32)
    csum = jnp.cumsum(onehot, axis=0)
    counts = csum[-1]
    rank = jnp.sum(onehot * csum, axis=1) - 1
    padded = ((counts + MOE_BLOCK - 1) // MOE_BLOCK) * MOE_BLOCK
    pad_start = jnp.cumsum(padded) - padded
    pos = (pad_start[e_flat] + rank).astype(I32)
    tok_sorted = jnp.zeros((p_rows,), I32).at[pos].set(jnp.arange(a, dtype=I32) // k)
    n_grp = (padded + r - 1) // r
    cum = jnp.cumsum(n_grp)
    n_groups = cum[-1]
    gid = jnp.arange(g_max, dtype=I32)
    last = jnp.maximum(n_groups - 1, 0)
    gid_c = jnp.minimum(gid, last)
    g_exp = jnp.minimum(jnp.searchsorted(cum, gid_c, side="right"), e - 1).astype(I32)
    local = gid_c - (cum - n_grp)[g_exp]
    g_row = (pad_start[g_exp] + local * r).astype(I32)
    g_n = jnp.clip(padded[g_exp] - local * r, 0, r).astype(I32)
    g_n = jnp.where(gid < n_groups, g_n, 0)
    return tok_sorted, pos, g_exp, g_row, g_n, n_groups.reshape(1).astype(I32), p_rows, g_max


def _moe_kernel(tok_ref, gexp_ref, grow_ref, gn_ref, ng_ref,
                u_hbm, wg_ref, wu_ref, bg_ref, bu_ref, wd_ref, bd_ref, ys_hbm,
                xg_ref, xb_ref, yst_ref, wgu_ref, wdb_ref, gsem, osem, *, rows, tf, nc):
    del gexp_ref
    g, c = pl.program_id(0), pl.program_id(1)
    ng = ng_ref[0]
    sub = 2 * MOE_BLOCK

    def start_gather(group, lo, hi):
        base = grow_ref[group]

        def body(r, carry):
            tok = tok_ref[base + r]
            pltpu.make_async_copy(u_hbm.at[pl.ds(tok, 1)], xg_ref.at[pl.ds(r, 1)], gsem).start()
            return carry
        lax.fori_loop(lo, hi, body, 0)

    def wait_rows(ref, n, sem):
        n = pl.multiple_of(n, MOE_BLOCK)
        pltpu.make_async_copy(ref.at[pl.ds(0, n)], ref.at[pl.ds(0, n)], sem).wait()

    @pl.when(g < ng)
    def _():
        n = gn_ref[g]
        row0 = grow_ref[g]

        @pl.when(c == 0)
        def _():
            @pl.when(g == 0)
            def _():
                start_gather(0, 0, n)
            wait_rows(xg_ref, n, gsem)

            def cast(i, carry):
                r0 = pl.multiple_of(i * MOE_BLOCK, MOE_BLOCK)
                xb_ref[pl.ds(r0, MOE_BLOCK), :] = xg_ref[pl.ds(r0, MOE_BLOCK), :].astype(BF16)
                yst_ref[pl.ds(r0, MOE_BLOCK), :] = jnp.broadcast_to(bd_ref[...], (MOE_BLOCK, bd_ref.shape[-1]))
                return carry

            @pl.when(g > 0)
            def _():
                wait_rows(yst_ref, gn_ref[jnp.maximum(g - 1, 0)], osem)
            lax.fori_loop(0, n // MOE_BLOCK, cast, 0)

        @pl.when((c > 0) & (g + 1 < ng))
        def _():
            nxt = jnp.minimum(g + 1, ng - 1)
            n_next = gn_ref[nxt]
            per = (n_next + (nc - 2)) // (nc - 1)
            start_gather(nxt, jnp.minimum((c - 1) * per, n_next), jnp.minimum(c * per, n_next))

        wgu_ref[:, :tf] = wg_ref[...].astype(BF16)
        wgu_ref[:, tf:] = wu_ref[...].astype(BF16)
        wdb_ref[...] = wd_ref[...].astype(BF16)

        def block(r0, m):
            x = xb_ref[pl.ds(r0, m), :]
            hu = jnp.dot(x, wgu_ref[...], preferred_element_type=F32)
            gate = jnp.minimum(hu[:, :tf] + bg_ref[...], SWIGLU_LIMIT)
            up = jnp.clip(hu[:, tf:] + bu_ref[...], -SWIGLU_LIMIT, SWIGLU_LIMIT)
            act = ((up + 1.0) * (gate * jax.nn.sigmoid(gate * SWIGLU_ALPHA))).astype(BF16)
            yst_ref[pl.ds(r0, m), :] += jnp.dot(act, wdb_ref[...], preferred_element_type=F32)

            @pl.when(c == nc - 1)
            def _():
                dst0 = pl.multiple_of(row0 + r0, MOE_BLOCK)
                pltpu.make_async_copy(yst_ref.at[pl.ds(r0, m)], ys_hbm.at[pl.ds(dst0, m)], osem).start()

        def loop_body(i, carry):
            block(pl.multiple_of(i * sub, sub), sub)
            return carry
        lax.fori_loop(0, n // sub, loop_body, 0)

        @pl.when(n % sub != 0)
        def _():
            block(pl.multiple_of((n // sub) * sub, sub), MOE_BLOCK)

        @pl.when((c == nc - 1) & (g == ng - 1))
        def _():
            wait_rows(yst_ref, n, osem)
            yst_ref[pl.ds(0, MOE_BLOCK), :] = jnp.zeros((MOE_BLOCK, yst_ref.shape[-1]), F32)
            first = (row0 + n) // MOE_BLOCK
            n_fill = ys_hbm.shape[0] // MOE_BLOCK - first

            def fill(i, carry):
                dst0 = pl.multiple_of((first + i) * MOE_BLOCK, MOE_BLOCK)
                pltpu.make_async_copy(yst_ref.at[pl.ds(0, MOE_BLOCK)], ys_hbm.at[pl.ds(dst0, MOE_BLOCK)], osem).start()
                return carry
            lax.fori_loop(0, n_fill, fill, 0)

            def drain(i, carry):
                pltpu.make_async_copy(yst_ref.at[pl.ds(0, MOE_BLOCK)], yst_ref.at[pl.ds(0, MOE_BLOCK)], osem).wait()
                return carry
            lax.fori_loop(0, n_fill, drain, 0)


def moe_experts(u, tok_sorted, g_exp, g_row, g_n, n_groups, w_gu, b_gu, w_down, b_down, p_rows, g_max, cfg):
    t, d = u.shape
    e, _, f2 = w_gu.shape
    f = f2 // 2
    tf = min(cfg.moe_tf, f)
    nc = f // tf
    assert nc >= 2
    rows = cfg.moe_rows

    def chunk(g, c, ng):
        return jnp.where(g < ng[0], c, nc - 1)

    grid_spec = pltpu.PrefetchScalarGridSpec(
        num_scalar_prefetch=5,
        grid=(g_max, nc),
        in_specs=[
            pl.BlockSpec(memory_space=pl.ANY),
            pl.BlockSpec((None, d, tf), lambda g, c, tk, ge, gr, gn, ng: (ge[g], 0, chunk(g, c, ng))),
            pl.BlockSpec((None, d, tf), lambda g, c, tk, ge, gr, gn, ng: (ge[g], 0, nc + chunk(g, c, ng))),
            pl.BlockSpec((None, 1, tf), lambda g, c, tk, ge, gr, gn, ng: (ge[g], 0, chunk(g, c, ng))),
            pl.BlockSpec((None, 1, tf), lambda g, c, tk, ge, gr, gn, ng: (ge[g], 0, nc + chunk(g, c, ng))),
            pl.BlockSpec((None, tf, d), lambda g, c, tk, ge, gr, gn, ng: (ge[g], chunk(g, c, ng), 0)),
            pl.BlockSpec((None, 1, d), lambda g, c, tk, ge, gr, gn, ng: (ge[g], 0, 0)),
        ],
        out_specs=pl.BlockSpec(memory_space=pl.ANY),
        scratch_shapes=[
            pltpu.VMEM((rows, d), F32), pltpu.VMEM((rows, d), BF16), pltpu.VMEM((rows, d), F32),
            pltpu.VMEM((d, 2 * tf), BF16), pltpu.VMEM((tf, d), BF16),
            pltpu.SemaphoreType.DMA(()), pltpu.SemaphoreType.DMA(()),
        ],
    )
    est = rows * d * 10 + 3 * d * tf * 2 + 2 * 3 * d * tf * 4
    return pl.pallas_call(
        functools.partial(_moe_kernel, rows=rows, tf=tf, nc=nc),
        grid_spec=grid_spec,
        out_shape=jax.ShapeDtypeStruct((p_rows, d), F32),
        compiler_params=_cparams(("arbitrary", "arbitrary"), est),
        name="moe_experts",
    )(tok_sorted, g_exp, g_row, g_n, n_groups,
      u, w_gu, w_gu, b_gu.reshape(e, 1, f2), b_gu.reshape(e, 1, f2), w_down, b_down.reshape(e, 1, d))


def _combine_kernel(pos_ref, ys_hbm, h_ref, gate_ref, g2_ref, fg_ref, o_ref, rows_ref, sem, *, tm, top_k, final_norm):
    i = pl.program_id(0)
    n_tiles = pl.num_programs(0)

    def start_tile(tile, slot):
        def body(j, carry):
            src = pos_ref[tile * (tm * top_k) + j]
            pltpu.make_async_copy(ys_hbm.at[pl.ds(src, 1)], rows_ref.at[slot, pl.ds(j, 1)], sem.at[slot]).start()
            return carry
        lax.fori_loop(0, tm * top_k, body, 0)

    @pl.when(i == 0)
    def _():
        start_tile(0, 0)

    @pl.when(i + 1 < n_tiles)
    def _():
        start_tile(i + 1, (i + 1) % 2)

    slot = i % 2
    pltpu.make_async_copy(rows_ref.at[slot], rows_ref.at[slot], sem.at[slot]).wait()
    rows = rows_ref[slot].reshape(tm, top_k, rows_ref.shape[-1])
    gates = gate_ref[...]
    acc = rows[:, 0, :] * gates[:, 0:1]
    for k in range(1, top_k):
        acc = acc + rows[:, k, :] * gates[:, k:k + 1]
    out = h_ref[...] + g2_ref[...] * acc
    if final_norm:
        out = _rmsnorm_f32(out, fg_ref[...])
    o_ref[...] = out


def moe_combine(ys, pos, h, gates, g2, final_g, cfg, *, final_norm):
    t, d = h.shape
    top_k = cfg.top_k
    tm = min(cfg.comb_tm, t)
    vec = pl.BlockSpec((1, d), lambda i, p: (0, 0))
    grid_spec = pltpu.PrefetchScalarGridSpec(
        num_scalar_prefetch=1,
        grid=(t // tm,),
        in_specs=[pl.BlockSpec(memory_space=pl.ANY),
                  pl.BlockSpec((tm, d), lambda i, p: (i, 0)),
                  pl.BlockSpec((tm, top_k), lambda i, p: (i, 0)), vec, vec],
        out_specs=pl.BlockSpec((tm, d), lambda i, p: (i, 0)),
        scratch_shapes=[pltpu.VMEM((2, tm * top_k, d), F32), pltpu.SemaphoreType.DMA((2,))],
    )
    return pl.pallas_call(
        functools.partial(_combine_kernel, tm=tm, top_k=top_k, final_norm=final_norm),
        grid_spec=grid_spec,
        out_shape=jax.ShapeDtypeStruct((t, d), F32),
        compiler_params=_cparams(("arbitrary",), 2 * tm * top_k * d * 4 + 6 * tm * d * 4),
        name="moe_combine",
    )(pos, ys, h, gates, g2, final_g)


def moe_layer(h, g, sc, sh, g2, w_router, b_router, w_gu, b_gu, w_down, b_down, final_g, cfg, *, final_norm):
    u, idx, gates = router(h, g, sc, sh, w_router, b_router.reshape(1, -1), cfg)
    tok_sorted, pos, g_exp, g_row, g_n, n_groups, p_rows, g_max = _routing_tables(idx, cfg)
    ys = moe_experts(u, tok_sorted, g_exp, g_row, g_n, n_groups, w_gu, b_gu, w_down, b_down, p_rows, g_max, cfg)
    return moe_combine(ys, pos, h, gates, g2, final_g, cfg, final_norm=final_norm)


def _diff_mixer(h, pos, g, sc, sh, gate, w_qkv, lq1, lk1, lq2, lk2, sub_g, w_o, rel_bias, layer_idx, cfg):
    heads, hd = cfg.diff_heads, cfg.diff_head_dim
    qk_w = heads * 2 * hd
    lam_init = 0.8 - 0.6 * math.exp(-0.3 * (layer_idx - 1))
    lam = (jnp.exp(jnp.sum(lq1 * lk1)) - jnp.exp(jnp.sum(lq2 * lk2)) + lam_init).reshape(1).astype(F32)
    n = w_qkv.shape[1]
    col_scale = jnp.concatenate([jnp.full((qk_w,), hd ** -0.5 * LOG2E, F32), jnp.ones((n - qk_w,), F32)]).reshape(1, n)
    qkv = norm_linear(h, 0, h.shape[1], g, sc, sh, w_qkv.astype(BF16), col_scale, BF16, cfg,
                      modulate=True, name="diff_qkv_proj")
    t = min(cfg.attn_t, h.shape[0])
    tiles = diff_bias_tiles(rel_bias, heads, t)
    kinds = _block_kinds(pos, t, REL_FAR, toeplitz=True)
    o = diff_attention(qkv, tiles, kinds, pos, rel_bias, lam, sub_g.reshape(1, -1), 1.0 - lam_init, cfg)
    return linear_residual(o, w_o.astype(BF16), h, gate, cfg, name="diff_out_proj")


def _mla_mixer(h, pos, g, sc, sh, gate, w_in, q_norm_g, kv_norm_g, w_uq, w_ukv, w_o, cfg):
    d = h.shape[1]
    heads, nope, rope, qr, kvr = cfg.mla_heads, cfg.mla_nope, cfg.mla_rope, cfg.mla_q_rank, cfg.mla_kv_rank
    assert nope == LANE and cfg.mla_v == LANE and rope <= LANE and qr % LANE == 0 and kvr == qr
    w_in_p = jnp.concatenate([w_in, jnp.zeros((d, LANE - rope), w_in.dtype)], axis=1).astype(BF16)
    ones = lambda n: jnp.ones((1, n), F32)
    z = norm_linear(h, 0, d, g, sc, sh, w_in_p, ones(w_in_p.shape[1]), F32, cfg, modulate=True, name="mla_down_proj")
    w_q = w_uq.reshape(qr, heads, nope + rope)
    w_q = jnp.concatenate([w_q, jnp.zeros((qr, heads, 2 * LANE - nope - rope), w_uq.dtype)], axis=2)
    w_q = w_q.reshape(qr, heads * 2 * LANE).astype(BF16)
    zeros_k = jnp.zeros((1, qr), F32)
    q_scale = jnp.full((1, heads * 2 * LANE), (nope + rope) ** -0.5 * LOG2E, F32)
    q_cat = norm_linear(z, 0, qr, q_norm_g.reshape(1, -1), zeros_k, zeros_k, w_q, q_scale, BF16, cfg,
                        modulate=False, name="mla_q_up_proj")
    kv = norm_linear(z, 1, kvr, kv_norm_g.reshape(1, -1), zeros_k, zeros_k, w_ukv.astype(BF16),
                     ones(w_ukv.shape[1]), BF16, cfg, modulate=False, name="mla_kv_up_proj")
    tables = _rope_tables(pos, rope)
    kr_rot = rope_rows(z, (qr + kvr) // LANE, tables, cfg)
    t = min(cfg.attn_t, h.shape[0])
    kinds = _block_kinds(pos, t, 0, toeplitz=False)
    o = mla_attention(q_cat, kv, kr_rot, tables, kinds, pos, cfg)
    return linear_residual(o, w_o.astype(BF16), h, gate, cfg, name="mla_out_proj")


def _forward(cfg, x, c, positions, ada_w, ada_b, norm1_g, norm2_g, final_g, rel_bias,
             diff_w_qkv, diff_lq1, diff_lk1, diff_lq2, diff_lk2, diff_sub_g, diff_w_o,
             mla_w_in, mla_q_norm_g, mla_kv_norm_g, mla_w_uq, mla_w_ukv, mla_w_o,
             router_w, router_b, exp_w_gu, exp_b_gu, exp_w_down, exp_b_down):
    b, s, d = x.shape
    assert b == 1, "kernels are written for a single sequence"
    h = x.reshape(s, d)
    pos = positions.reshape(s).astype(I32)
    mod = ada_modulation(c.reshape(d, 1), ada_w, ada_b, cfg)
    fg = final_g.reshape(1, d)
    for i in range(cfg.depth):
        sh1, sc1, g1, sh2, sc2, g2 = [mod[i, :, j * d:(j + 1) * d] for j in range(6)]
        n1 = norm1_g[i].reshape(1, d)
        j = i // 2
        if i % 2 == 0:
            h = _diff_mixer(h, pos, n1, sc1, sh1, g1, diff_w_qkv[j], diff_lq1[j], diff_lk1[j], diff_lq2[j],
                            diff_lk2[j], diff_sub_g[j], diff_w_o[j], rel_bias, i + 1, cfg)
        else:
            h = _mla_mixer(h, pos, n1, sc1, sh1, g1, mla_w_in[j], mla_q_norm_g[j], mla_kv_norm_g[j],
                           mla_w_uq[j], mla_w_ukv[j], mla_w_o[j], cfg)
        h = moe_layer(h, norm2_g[i].reshape(1, d), sc2, sh2, g2, router_w[i], router_b[i], exp_w_gu[i], exp_b_gu[i],
                      exp_w_down[i], exp_b_down[i], fg, cfg, final_norm=(i == cfg.depth - 1))
    return h.reshape(b, s, d)


def kernel(x, c, positions, ada_w, ada_b, norm1_g, norm2_g, final_g, rel_bias, diff_w_qkv, diff_lq1, diff_lk1, diff_lq2, diff_lk2, diff_sub_g, diff_w_o, mla_w_in, mla_q_norm_g, mla_kv_norm_g, mla_w_uq, mla_w_ukv, mla_w_o, router_w, router_b, exp_w_gu, exp_b_gu, exp_w_down, exp_b_down):
    return _forward(Cfg(), x, c, positions, ada_w, ada_b, norm1_g, norm2_g, final_g, rel_bias,
                    diff_w_qkv, diff_lq1, diff_lk1, diff_lq2, diff_lk2, diff_sub_g, diff_w_o,
                    mla_w_in, mla_q_norm_g, mla_kv_norm_g, mla_w_uq, mla_w_ukv, mla_w_o,
                    router_w, router_b, exp_w_gu, exp_b_gu, exp_w_down, exp_b_down)
```

```python
import dataclasses
import functools
import math

import jax
import jax.numpy as jnp
import numpy as np
from jax import lax
from jax.experimental import pallas as pl
from jax.experimental.pallas import tpu as pltpu

F32 = jnp.float32
BF16 = jnp.bfloat16
I32 = jnp.int32

RMS_EPS = 1e-6
NEG_INF = -1e30
LOG2E = math.log2(math.e)
ROPE_THETA = 10000.0
REL_BUCKETS = 32
REL_MAX_DIST = 128
REL_FAR = 113
SWIGLU_LIMIT = 7.0
SWIGLU_ALPHA = 1.702
Q_BLOCK = 128
MOE_BLOCK = 128
LANE = 128
V7X_VMEM_BYTES = 64 * 1024 * 1024


@dataclasses.dataclass(frozen=True)
class Cfg:
    d_model: int = 2048
    seq: int = 8192
    depth: int = 2
    diff_heads: int = 8
    diff_head_dim: int = 128
    mla_heads: int = 16
    mla_q_rank: int = 512
    mla_kv_rank: int = 512
    mla_nope: int = 128
    mla_rope: int = 64
    mla_v: int = 128
    n_experts: int = 32
    top_k: int = 4
    d_ff: int = 2048
    ada_tn: int = 1024
    lin_tm: int = 1024
    lin_tn: int = 1024
    attn_t: int = 512
    diff_hb: int = 2
    mla_hb: int = 4
    router_tm: int = 256
    moe_rows: int = 1280
    moe_tf: int = 256
    comb_tm: int = 128


def _vmem_limit(nbytes):
    return int(min(nbytes * 1.25 + (6 << 20), V7X_VMEM_BYTES - (6 << 20)))


def _cparams(sem, nbytes):
    return pltpu.CompilerParams(dimension_semantics=sem, vmem_limit_bytes=_vmem_limit(nbytes))


def _ada_kernel(c_ref, w_ref, b_ref, o_ref):
    c = c_ref[...]
    cs = c * jax.nn.sigmoid(c)
    o_ref[...] = jnp.sum(w_ref[...] * cs, axis=0, keepdims=True) + b_ref[...]


def ada_modulation(c_col, ada_w, ada_b, cfg):
    depth, d, n = ada_w.shape
    tn = min(cfg.ada_tn, n)
    assert n % tn == 0
    return pl.pallas_call(
        _ada_kernel,
        grid=(depth, n // tn),
        in_specs=[
            pl.BlockSpec((d, 1), lambda l, j: (0, 0)),
            pl.BlockSpec((None, d, tn), lambda l, j: (l, 0, j)),
            pl.BlockSpec((None, 1, tn), lambda l, j: (l, 0, j)),
        ],
        out_specs=pl.BlockSpec((None, 1, tn), lambda l, j: (l, 0, j)),
        out_shape=jax.ShapeDtypeStruct((depth, 1, n), F32),
        compiler_params=_cparams(("arbitrary", "arbitrary"), 2 * d * tn * 4 + d * tn * 4),
        name="ada_modulation",
    )(c_col, ada_w, ada_b.reshape(depth, 1, n))


def _rmsnorm_f32(x, g):
    return x * lax.rsqrt(jnp.mean(x * x, axis=-1, keepdims=True) + RMS_EPS) * g


def _norm_linear_kernel(x_ref, g_ref, sc_ref, sh_ref, w_ref, cs_ref, o_ref, xn_ref, *, modulate):
    @pl.when(pl.program_id(1) == 0)
    def _():
        y = _rmsnorm_f32(x_ref[...].astype(F32), g_ref[...])
        if modulate:
            y = y * (1.0 + sc_ref[...]) + sh_ref[...]
        xn_ref[...] = y.astype(BF16)

    acc = jnp.dot(xn_ref[...], w_ref[...], preferred_element_type=F32)
    o_ref[...] = (acc * cs_ref[...]).astype(o_ref.dtype)


def norm_linear(x, x_col_block, k, g, sc, sh, w_bf, col_scale, out_dtype, cfg, *, modulate, name):
    m = x.shape[0]
    n = w_bf.shape[1]
    tm = min(cfg.lin_tm, m)
    tn = n if n <= cfg.lin_tn or n % cfg.lin_tn else cfg.lin_tn
    assert m % tm == 0 and n % tn == 0
    out_b = jnp.dtype(out_dtype).itemsize
    est = 2 * tm * k * 4 + tm * k * 2 + 2 * k * tn * 2 + 2 * tm * tn * out_b + tm * k * 4
    return pl.pallas_call(
        functools.partial(_norm_linear_kernel, modulate=modulate),
        grid=(m // tm, n // tn),
        in_specs=[
            pl.BlockSpec((tm, k), lambda i, j: (i, x_col_block)),
            pl.BlockSpec((1, k), lambda i, j: (0, 0)),
            pl.BlockSpec((1, k), lambda i, j: (0, 0)),
            pl.BlockSpec((1, k), lambda i, j: (0, 0)),
            pl.BlockSpec((k, tn), lambda i, j: (0, j)),
            pl.BlockSpec((1, tn), lambda i, j: (0, j)),
        ],
        out_specs=pl.BlockSpec((tm, tn), lambda i, j: (i, j)),
        out_shape=jax.ShapeDtypeStruct((m, n), out_dtype),
        scratch_shapes=[pltpu.VMEM((tm, k), BF16)],
        compiler_params=_cparams(("arbitrary", "arbitrary"), est),
        name=name,
    )(x, g, sc, sh, w_bf, col_scale)


def _linear_res_kernel(a_ref, w_ref, h_ref, g_ref, o_ref):
    acc = jnp.dot(a_ref[...], w_ref[...], preferred_element_type=F32)
    o_ref[...] = h_ref[...] + g_ref[...] * acc


def linear_residual(a_bf, w_bf, h, gate, cfg, *, name):
    m, k = a_bf.shape
    n = w_bf.shape[1]
    tm = min(cfg.lin_tm, m)
    tn = min(cfg.lin_tn, n)
    assert m % tm == 0 and n % tn == 0
    est = 2 * tm * k * 2 + 2 * k * tn * 2 + 4 * tm * tn * 4
    return pl.pallas_call(
        _linear_res_kernel,
        grid=(m // tm, n // tn),
        in_specs=[
            pl.BlockSpec((tm, k), lambda i, j: (i, 0)),
            pl.BlockSpec((k, tn), lambda i, j: (0, j)),
            pl.BlockSpec((tm, tn), lambda i, j: (i, j)),
            pl.BlockSpec((1, tn), lambda i, j: (0, j)),
        ],
        out_specs=pl.BlockSpec((tm, tn), lambda i, j: (i, j)),
        out_shape=jax.ShapeDtypeStruct((m, n), F32),
        compiler_params=_cparams(("arbitrary", "arbitrary"), est),
        name=name,
    )(a_bf, w_bf, h, gate)


KIND_SKIP, KIND_PLAIN, KIND_DIAG, KIND_OFFDIAG, KIND_GENERAL = 0, 1, 2, 3, 4


def _block_kinds(pos, t, far_dist, toeplitz):
    s = pos.shape[0]
    nb = s // t
    pb = pos.reshape(nb, t)
    pmin, pmax = pb.min(axis=1), pb.max(axis=1)
    consecutive = jnp.all(pb == pb[:, :1] + jnp.arange(t, dtype=pos.dtype)[None, :], axis=1)
    qi = jnp.arange(nb)[:, None]
    ki = jnp.arange(nb)[None, :]
    gap = pmin[:, None] - pmax[None, :]
    plain = (ki < qi) & (gap >= (far_dist if far_dist else 0))
    kinds = jnp.where(plain, KIND_PLAIN, KIND_GENERAL)
    if toeplitz:
        both = consecutive[:, None] & consecutive[None, :]
        d = pb[:, 0][:, None] - pb[:, 0][None, :]
        kinds = jnp.where(~plain & both & (ki == qi - 1) & (d == t), KIND_OFFDIAG, kinds)
        kinds = jnp.where(both & (ki == qi) & (d == 0), KIND_DIAG, kinds)
    return jnp.where(ki > qi, KIND_SKIP, kinds).astype(I32)


def _attn_pairs(kinds2d, nq):
    qi = np.concatenate([np.full(q + 1, q, np.int32) for q in range(nq)])
    ki = np.concatenate([np.arange(q + 1, dtype=np.int32) for q in range(nq)])
    return jnp.asarray(qi), jnp.asarray(ki), kinds2d[qi, ki].astype(I32)


def _t5_bias_minus_last(rel, rb_ref, h):
    n = jnp.maximum(rel, 0)
    max_exact = REL_BUCKETS // 2
    nf = jnp.maximum(n, 1).astype(F32)
    large = max_exact + (jnp.log(nf / max_exact) / math.log(REL_MAX_DIST / max_exact)
                         * (REL_BUCKETS - max_exact)).astype(I32)
    large = jnp.minimum(large, REL_BUCKETS - 1)
    bucket = jnp.where(n < max_exact, n, large)
    last = rb_ref[REL_BUCKETS - 1, h]
    out = jnp.zeros(rel.shape, F32)
    for b in range(REL_BUCKETS - 1):
        out = jnp.where(bucket == b, rb_ref[b, h] - last, out)
    return out


def _bias_tile_kernel(rb_ref, o_ref, *, t):
    kind = pl.program_id(0)
    h = pl.program_id(1)
    rel = kind * t + lax.broadcasted_iota(I32, (t, t), 0) - lax.broadcasted_iota(I32, (t, t), 1)
    bias = _t5_bias_minus_last(rel, rb_ref, h) * LOG2E
    o_ref[...] = jnp.where(rel >= 0, bias, NEG_INF)


def diff_bias_tiles(rel_bias, heads, t):
    return pl.pallas_call(
        functools.partial(_bias_tile_kernel, t=t),
        grid=(2, heads),
        in_specs=[pl.BlockSpec(memory_space=pltpu.SMEM)],
        out_specs=pl.BlockSpec((None, None, t, t), lambda k, h: (k, h, 0, 0)),
        out_shape=jax.ShapeDtypeStruct((2, heads, t, t), F32),
        compiler_params=_cparams(("arbitrary", "arbitrary"), 8 * t * t * 4),
        name="diff_bias_tiles",
    )(rel_bias)


def _structural_ok(qi, ki, t):
    q_idx = qi * t + lax.broadcasted_iota(I32, (t, t), 0)
    k_idx = ki * t + lax.broadcasted_iota(I32, (t, t), 1)
    return k_idx < ((q_idx // Q_BLOCK) + 1) * Q_BLOCK


def _online_softmax_step(s, v, m_ref, l_ref, acc_ref):
    nl = s.shape[1] // LANE
    m_prev = m_ref[...]
    m_new = jnp.maximum(m_prev, jnp.max(s, axis=-1, keepdims=True))
    alpha = jnp.exp2(m_prev - m_new)
    p = jnp.exp2(s - jnp.tile(m_new, (1, nl)))
    psum = p[:, :LANE]
    for j in range(1, nl):
        psum = psum + p[:, j * LANE:(j + 1) * LANE]
    l_ref[...] = alpha * l_ref[...] + psum
    pv = jnp.dot(p.astype(BF16), v, preferred_element_type=F32)
    acc_ref[...] = jnp.tile(alpha, (1, v.shape[1] // LANE)) * acc_ref[...] + pv
    m_ref[...] = m_new


def _diff_attn_kernel(qi_ref, ki_ref, kind_ref, q_ref, k_ref, v_ref, bt_ref, pq_ref, pk_ref, rb_ref, lam_ref, sg_ref,
                      o_ref, m_ref, l_ref, acc_ref, *, t, hd, hb, out_scale):
    hg, pair = pl.program_id(0), pl.program_id(1)
    qi, ki, kind = qi_ref[pair], ki_ref[pair], kind_ref[pair]
    w = 2 * hd

    @pl.when(ki == 0)
    def _():
        m_ref[...] = jnp.full(m_ref.shape, -jnp.inf, F32)
        l_ref[...] = jnp.zeros(l_ref.shape, F32)
        acc_ref[...] = jnp.zeros(acc_ref.shape, F32)

    def step(adjust):
        for h in range(hb):
            v = v_ref[:, h * w:(h + 1) * w]
            for mp in range(2):
                c0 = h * w + mp * hd
                s = lax.dot_general(q_ref[:, c0:c0 + hd], k_ref[:, c0:c0 + hd], (((1,), (1,)), ((), ())),
                                    preferred_element_type=F32)
                j = 2 * h + mp
                _online_softmax_step(adjust(s, h), v, m_ref.at[j], l_ref.at[j], acc_ref.at[j])

    @pl.when(kind == KIND_PLAIN)
    def _():
        step(lambda s, h: s)

    @pl.when((kind == KIND_DIAG) | (kind == KIND_OFFDIAG))
    def _():
        step(lambda s, h: s + bt_ref[h])

    @pl.when(kind == KIND_GENERAL)
    def _():
        rel = pq_ref[...] - pk_ref[...]
        struct = _structural_ok(qi, ki, t)

        def adjust(s, h):
            add = _t5_bias_minus_last(rel, rb_ref, hg * hb + h) * LOG2E
            s = jnp.where(rel >= 0, s + add, NEG_INF)
            return jnp.where(struct, s, -jnp.inf)
        step(adjust)

    @pl.when(ki == qi)
    def _():
        for h in range(hb):
            l0 = jnp.sum(l_ref[2 * h], axis=-1, keepdims=True)
            l1 = jnp.sum(l_ref[2 * h + 1], axis=-1, keepdims=True)
            o = acc_ref[2 * h] / l0 - lam_ref[0] * (acc_ref[2 * h + 1] / l1)
            o_ref[:, h * w:(h + 1) * w] = (_rmsnorm_f32(o, sg_ref[...]) * out_scale).astype(o_ref.dtype)


def diff_attention(qkv, bias_tiles, kinds2d, pos, rel_bias, lam, sub_g, out_scale, cfg):
    s = qkv.shape[0]
    t = min(cfg.attn_t, s)
    nq = s // t
    heads, hd, hb = cfg.diff_heads, cfg.diff_head_dim, cfg.diff_hb
    assert heads % hb == 0
    ng = heads // hb
    w = 2 * hd
    qi_tab, ki_tab, kind_tab = _attn_pairs(kinds2d, nq)

    grid_spec = pltpu.PrefetchScalarGridSpec(
        num_scalar_prefetch=3,
        grid=(ng, qi_tab.shape[0]),
        in_specs=[
            pl.BlockSpec((t, hb * w), lambda g, p, qt, kt, kd: (qt[p], g)),
            pl.BlockSpec((t, hb * w), lambda g, p, qt, kt, kd: (kt[p], ng + g)),
            pl.BlockSpec((t, hb * w), lambda g, p, qt, kt, kd: (kt[p], 2 * ng + g)),
            pl.BlockSpec((None, hb, t, t), lambda g, p, qt, kt, kd: (jnp.where(kd[p] == KIND_OFFDIAG, 1, 0), g, 0, 0)),
            pl.BlockSpec((t, 1), lambda g, p, qt, kt, kd: (qt[p], 0)),
            pl.BlockSpec((1, t), lambda g, p, qt, kt, kd: (0, kt[p])),
            pl.BlockSpec(memory_space=pltpu.SMEM),
            pl.BlockSpec(memory_space=pltpu.SMEM),
            pl.BlockSpec((1, w), lambda g, p, qt, kt, kd: (0, 0)),
        ],
        out_specs=pl.BlockSpec((t, hb * w), lambda g, p, qt, kt, kd: (qt[p], g)),
        scratch_shapes=[pltpu.VMEM((2 * hb, t, LANE), F32), pltpu.VMEM((2 * hb, t, LANE), F32),
                        pltpu.VMEM((2 * hb, t, w), F32)],
    )
    est = hb * (2 * (4 * t * w * 2 + t * t * 4) + 2 * t * w * 4 + 4 * t * LANE * 4) + 10 * t * t * 4
    return pl.pallas_call(
        functools.partial(_diff_attn_kernel, t=t, hd=hd, hb=hb, out_scale=out_scale),
        grid_spec=grid_spec,
        out_shape=jax.ShapeDtypeStruct((s, heads * w), BF16),
        compiler_params=_cparams(("arbitrary", "arbitrary"), est),
        name="diff_attention",
    )(qi_tab, ki_tab, kind_tab, qkv, qkv, qkv, bias_tiles, pos.reshape(s, 1), pos.reshape(1, s), rel_bias, lam, sub_g)


def _rope_slot(x, c, s1, s2):
    return x * c + pltpu.roll(x, LANE - 32, 1) * s1 + pltpu.roll(x, 32, 1) * s2


def _rope_tables(pos, rope_dim):
    half = rope_dim // 2
    inv = ROPE_THETA ** (-jnp.arange(half, dtype=F32) / half)
    ang = pos.astype(F32)[:, None] * inv
    cos, sin = jnp.cos(ang), jnp.sin(ang)
    z = jnp.zeros_like(cos)
    pad = jnp.zeros((pos.shape[0], LANE - 2 * half), F32)
    c = jnp.concatenate([cos, cos, pad], axis=1)
    s1 = jnp.concatenate([-sin, z, pad], axis=1)
    s2 = jnp.concatenate([z, sin, pad], axis=1)
    return c, s1, s2


def _rope_rows_kernel(x_ref, c_ref, s1_ref, s2_ref, o_ref):
    o_ref[...] = _rope_slot(x_ref[...].astype(F32), c_ref[...], s1_ref[...], s2_ref[...]).astype(o_ref.dtype)


def rope_rows(x, col_block, tables, cfg):
    m = x.shape[0]
    tm = min(cfg.lin_tm, m)
    row = pl.BlockSpec((tm, LANE), lambda i: (i, 0))
    return pl.pallas_call(
        _rope_rows_kernel,
        grid=(m // tm,),
        in_specs=[pl.BlockSpec((tm, LANE), lambda i: (i, col_block)), row, row, row],
        out_specs=row,
        out_shape=jax.ShapeDtypeStruct((m, LANE), BF16),
        compiler_params=_cparams(("arbitrary",), 10 * tm * LANE * 4),
        name="mla_rope_key",
    )(x, *tables)


def _mla_attn_kernel(qi_ref, ki_ref, kind_ref, q_ref, kv_ref, kr_ref, c_ref, s1_ref, s2_ref, pq_ref, pk_ref, o_ref,
                     qs_ref, kc_ref, m_ref, l_ref, acc_ref, *, t, hb):
    pair = pl.program_id(1)
    qi, ki, kind = qi_ref[pair], ki_ref[pair], kind_ref[pair]
    w = 2 * LANE

    @pl.when(ki == 0)
    def _():
        m_ref[...] = jnp.full(m_ref.shape, -jnp.inf, F32)
        l_ref[...] = jnp.zeros(l_ref.shape, F32)
        acc_ref[...] = jnp.zeros(acc_ref.shape, F32)
        for h in range(hb):
            qs_ref[h, :, :LANE] = q_ref[:, h * w:h * w + LANE]
            qr = _rope_slot(q_ref[:, h * w + LANE:(h + 1) * w].astype(F32), c_ref[...], s1_ref[...], s2_ref[...])
            qs_ref[h, :, LANE:] = qr.astype(BF16)

    def step(adjust):
        for h in range(hb):
            kc_ref[h, :, :LANE] = kv_ref[:, h * w:h * w + LANE]
            kc_ref[h, :, LANE:] = kr_ref[...]
            s = lax.dot_general(qs_ref[h], kc_ref[h], (((1,), (1,)), ((), ())), preferred_element_type=F32)
            _online_softmax_step(adjust(s), kv_ref[:, h * w + LANE:(h + 1) * w], m_ref.at[h], l_ref.at[h], acc_ref.at[h])

    @pl.when(kind == KIND_PLAIN)
    def _():
        step(lambda s: s)

    @pl.when(kind == KIND_GENERAL)
    def _():
        rel = pq_ref[...] - pk_ref[...]
        struct = _structural_ok(qi, ki, t)

        def adjust(s):
            s = jnp.where(rel >= 0, s, NEG_INF)
            return jnp.where(struct, s, -jnp.inf)
        step(adjust)

    @pl.when(ki == qi)
    def _():
        for h in range(hb):
            l = jnp.sum(l_ref[h], axis=-1, keepdims=True)
            o_ref[:, h * LANE:(h + 1) * LANE] = (acc_ref[h] / l).astype(o_ref.dtype)


def mla_attention(q_cat, kv, kr_rot, tables, kinds2d, pos, cfg):
    s = q_cat.shape[0]
    t = min(cfg.attn_t, s)
    nq = s // t
    heads, hb = cfg.mla_heads, cfg.mla_hb
    assert heads % hb == 0
    w = 2 * LANE
    qi_tab, ki_tab, kind_tab = _attn_pairs(kinds2d, nq)
    qrow = pl.BlockSpec((t, LANE), lambda g, p, qt, kt, kd: (qt[p], 0))
    grid_spec = pltpu.PrefetchScalarGridSpec(
        num_scalar_prefetch=3,
        grid=(heads // hb, qi_tab.shape[0]),
        in_specs=[
            pl.BlockSpec((t, hb * w), lambda g, p, qt, kt, kd: (qt[p], g)),
            pl.BlockSpec((t, hb * w), lambda g, p, qt, kt, kd: (kt[p], g)),
            pl.BlockSpec((t, LANE), lambda g, p, qt, kt, kd: (kt[p], 0)),
            qrow, qrow, qrow,
            pl.BlockSpec((t, 1), lambda g, p, qt, kt, kd: (qt[p], 0)),
            pl.BlockSpec((1, t), lambda g, p, qt, kt, kd: (0, kt[p])),
        ],
        out_specs=pl.BlockSpec((t, hb * LANE), lambda g, p, qt, kt, kd: (qt[p], g)),
        scratch_shapes=[pltpu.VMEM((hb, t, w), BF16), pltpu.VMEM((hb, t, w), BF16),
                        pltpu.VMEM((hb, t, LANE), F32), pltpu.VMEM((hb, t, LANE), F32), pltpu.VMEM((hb, t, LANE), F32)],
    )
    est = hb * (2 * (2 * t * w * 2 + t * LANE * 2) + 2 * t * w * 2 + 3 * t * LANE * 4) + 8 * t * LANE * 4 + 10 * t * t * 4
    return pl.pallas_call(
        functools.partial(_mla_attn_kernel, t=t, hb=hb),
        grid_spec=grid_spec,
        out_shape=jax.ShapeDtypeStruct((s, heads * cfg.mla_v), BF16),
        compiler_params=_cparams(("arbitrary", "arbitrary"), est),
        name="mla_attention",
    )(qi_tab, ki_tab, kind_tab, q_cat, kv, kr_rot, *tables, pos.reshape(s, 1), pos.reshape(1, s))


def _router_kernel(h_ref, g_ref, sc_ref, sh_ref, wr_ref, br_ref, u_ref, idx_ref, gate_ref, *, top_k):
    u = _rmsnorm_f32(h_ref[...], g_ref[...]) * (1.0 + sc_ref[...]) + sh_ref[...]
    u_ref[...] = u
    logits = jnp.dot(u, wr_ref[...], preferred_element_type=F32, precision=lax.Precision.HIGHEST) + br_ref[...]
    n_e = logits.shape[-1]
    lane = lax.broadcasted_iota(I32, logits.shape, 1)
    vals, idxs = [], []
    cur = logits
    for _ in range(top_k):
        mx = jnp.max(cur, axis=-1, keepdims=True)
        ix = jnp.min(jnp.where(cur == mx, lane, n_e), axis=-1, keepdims=True)
        vals.append(mx)
        idxs.append(ix)
        cur = jnp.where(lane == ix, -jnp.inf, cur)
    v = jnp.concatenate(vals, axis=1)
    e = jnp.exp(v - vals[0])
    gate_ref[...] = e / jnp.sum(e, axis=-1, keepdims=True)
    idx_ref[...] = jnp.concatenate(idxs, axis=1)


def router(h, g, sc, sh, w_router, b_router, cfg):
    t, d = h.shape
    e = w_router.shape[1]
    tm = min(cfg.router_tm, t)
    vec = pl.BlockSpec((1, d), lambda i: (0, 0))
    return pl.pallas_call(
        functools.partial(_router_kernel, top_k=cfg.top_k),
        grid=(t // tm,),
        in_specs=[pl.BlockSpec((tm, d), lambda i: (i, 0)), vec, vec, vec,
                  pl.BlockSpec((d, e), lambda i: (0, 0)), pl.BlockSpec((1, e), lambda i: (0, 0))],
        out_specs=[pl.BlockSpec((tm, d), lambda i: (i, 0)),
                   pl.BlockSpec((tm, cfg.top_k), lambda i: (i, 0)),
                   pl.BlockSpec((tm, cfg.top_k), lambda i: (i, 0))],
        out_shape=[jax.ShapeDtypeStruct((t, d), F32),
                   jax.ShapeDtypeStruct((t, cfg.top_k), I32),
                   jax.ShapeDtypeStruct((t, cfg.top_k), F32)],
        compiler_params=_cparams(("arbitrary",), 6 * tm * d * 4 + 2 * d * LANE * 4),
        name="moe_router",
    )(h, g, sc, sh, w_router, b_router)


def _routing_tables(idx, cfg):
    t, k = idx.shape
    a = t * k
    e = cfg.n_experts
    r = cfg.moe_rows
    p_rows = a + e * MOE_BLOCK
    g_max = e + a // r + 1
    e_flat = idx.reshape(a)
    onehot = (e_flat[:, None] == jnp.arange(e, dtype=I32)[None, :]).astype(I32)
    csum = jnp.cumsum(onehot, axis=0)
    counts = csum[-1]
    rank = jnp.sum(onehot * csum, axis=1) - 1
    padded = ((counts + MOE_BLOCK - 1) // MOE_BLOCK) * MOE_BLOCK
    pad_start = jnp.cumsum(padded) - padded
    pos = (pad_start[e_flat] + rank).astype(I32)
    tok_sorted = jnp.zeros((p_rows,), I32).at[pos].set(jnp.arange(a, dtype=I32) // k)
    n_grp = (padded + r - 1) // r
    cum = jnp.cumsum(n_grp)
    n_groups = cum[-1]
    gid = jnp.arange(g_max, dtype=I32)
    last = jnp.maximum(n_groups - 1, 0)
    gid_c = jnp.minimum(gid, last)
    g_exp = jnp.minimum(jnp.searchsorted(cum, gid_c, side="right"), e - 1).astype(I32)
    local = gid_c - (cum - n_grp)[g_exp]
    g_row = (pad_start[g_exp] + local * r).astype(I32)
    g_n = jnp.clip(padded[g_exp] - local * r, 0, r).astype(I32)
    g_n = jnp.where(gid < n_groups, g_n, 0)
    return tok_sorted, pos, g_exp, g_row, g_n, n_groups.reshape(1).astype(I32), p_rows, g_max


def _moe_kernel(tok_ref, gexp_ref, grow_ref, gn_ref, ng_ref,
                u_hbm, wg_ref, wu_ref, bg_ref, bu_ref, wd_ref, bd_ref, ys_hbm,
                xg_ref, xb_ref, yst_ref, gsem, osem, *, tf, nc):
    del gexp_ref, tf
    g, c = pl.program_id(0), pl.program_id(1)
    ng = ng_ref[0]
    sub = 2 * MOE_BLOCK

    def start_gather(group, blk_lo, blk_hi):
        base = grow_ref[group]

        def body(blk, carry):
            r0 = pl.multiple_of(blk * MOE_BLOCK, MOE_BLOCK)
            for i in range(MOE_BLOCK):
                tok = tok_ref[base + r0 + i]
                pltpu.make_async_copy(u_hbm.at[pl.ds(tok, 1)], xg_ref.at[pl.ds(r0 + i, 1)], gsem).start()
            return carry
        lax.fori_loop(blk_lo, blk_hi, body, 0)

    def wait_rows(ref, n, sem):
        n = pl.multiple_of(n, MOE_BLOCK)
        pltpu.make_async_copy(ref.at[pl.ds(0, n)], ref.at[pl.ds(0, n)], sem).wait()

    @pl.when(g < ng)
    def _():
        n = gn_ref[g]
        row0 = grow_ref[g]

        @pl.when(c == 0)
        def _():
            @pl.when(g == 0)
            def _():
                start_gather(0, 0, n // MOE_BLOCK)
            wait_rows(xg_ref, n, gsem)

            def cast(i, carry):
                r0 = pl.multiple_of(i * MOE_BLOCK, MOE_BLOCK)
                xb_ref[pl.ds(r0, MOE_BLOCK), :] = xg_ref[pl.ds(r0, MOE_BLOCK), :].astype(BF16)
                yst_ref[pl.ds(r0, MOE_BLOCK), :] = jnp.broadcast_to(bd_ref[...], (MOE_BLOCK, bd_ref.shape[-1]))
                return carry

            @pl.when(g > 0)
            def _():
                wait_rows(yst_ref, gn_ref[jnp.maximum(g - 1, 0)], osem)
            lax.fori_loop(0, n // MOE_BLOCK, cast, 0)

        @pl.when((c > 0) & (g + 1 < ng))
        def _():
            nxt = jnp.minimum(g + 1, ng - 1)
            nb_next = gn_ref[nxt] // MOE_BLOCK
            per = (nb_next + (nc - 2)) // (nc - 1)
            start_gather(nxt, jnp.minimum((c - 1) * per, nb_next), jnp.minimum(c * per, nb_next))

        def block(r0, m):
            x = xb_ref[pl.ds(r0, m), :]
            hg = jnp.dot(x, wg_ref[...].astype(BF16), preferred_element_type=F32)
            hu = jnp.dot(x, wu_ref[...].astype(BF16), preferred_element_type=F32)
            gate = jnp.minimum(hg + bg_ref[...], SWIGLU_LIMIT)
            up = jnp.clip(hu + bu_ref[...], -SWIGLU_LIMIT, SWIGLU_LIMIT)
            act = ((up + 1.0) * (gate * jax.nn.sigmoid(gate * SWIGLU_ALPHA))).astype(BF16)
            yst_ref[pl.ds(r0, m), :] += jnp.dot(act, wd_ref[...].astype(BF16), preferred_element_type=F32)

            @pl.when(c == nc - 1)
            def _():
                dst0 = pl.multiple_of(row0 + r0, MOE_BLOCK)
                pltpu.make_async_copy(yst_ref.at[pl.ds(r0, m)], ys_hbm.at[pl.ds(dst0, m)], osem).start()

        def loop_body(i, carry):
            block(pl.multiple_of(i * sub, sub), sub)
            return carry
        lax.fori_loop(0, n // sub, loop_body, 0)

        @pl.when(n % sub != 0)
        def _():
            block(pl.multiple_of((n // sub) * sub, sub), MOE_BLOCK)

        @pl.when((c == nc - 1) & (g == ng - 1))
        def _():
            wait_rows(yst_ref, n, osem)
            yst_ref[pl.ds(0, MOE_BLOCK), :] = jnp.zeros((MOE_BLOCK, yst_ref.shape[-1]), F32)
            first = (row0 + n) // MOE_BLOCK
            n_fill = ys_hbm.shape[0] // MOE_BLOCK - first

            def fill(i, carry):
                dst0 = pl.multiple_of((first + i) * MOE_BLOCK, MOE_BLOCK)
                pltpu.make_async_copy(yst_ref.at[pl.ds(0, MOE_BLOCK)], ys_hbm.at[pl.ds(dst0, MOE_BLOCK)], osem).start()
                return carry
            lax.fori_loop(0, n_fill, fill, 0)

            def drain(i, carry):
                pltpu.make_async_copy(yst_ref.at[pl.ds(0, MOE_BLOCK)], yst_ref.at[pl.ds(0, MOE_BLOCK)], osem).wait()
                return carry
            lax.fori_loop(0, n_fill, drain, 0)


def moe_experts(u, tok_sorted, g_exp, g_row, g_n, n_groups, layer, w_gu, b_gu, w_down, b_down, p_rows, g_max, cfg):
    t, d = u.shape
    n_layers, e, _, f2 = w_gu.shape
    f = f2 // 2
    tf = min(cfg.moe_tf, f)
    nc = f // tf
    assert nc >= 2
    rows = cfg.moe_rows

    def chunk(g, c, ng):
        return jnp.where(g < ng[0], c, nc - 1)

    grid_spec = pltpu.PrefetchScalarGridSpec(
        num_scalar_prefetch=5,
        grid=(g_max, nc),
        in_specs=[
            pl.BlockSpec(memory_space=pl.ANY),
            pl.BlockSpec((None, None, d, tf), lambda g, c, tk, ge, gr, gn, ng: (layer, ge[g], 0, chunk(g, c, ng))),
            pl.BlockSpec((None, None, d, tf), lambda g, c, tk, ge, gr, gn, ng: (layer, ge[g], 0, nc + chunk(g, c, ng))),
            pl.BlockSpec((None, None, 1, tf), lambda g, c, tk, ge, gr, gn, ng: (layer, ge[g], 0, chunk(g, c, ng))),
            pl.BlockSpec((None, None, 1, tf), lambda g, c, tk, ge, gr, gn, ng: (layer, ge[g], 0, nc + chunk(g, c, ng))),
            pl.BlockSpec((None, None, tf, d), lambda g, c, tk, ge, gr, gn, ng: (layer, ge[g], chunk(g, c, ng), 0)),
            pl.BlockSpec((None, None, 1, d), lambda g, c, tk, ge, gr, gn, ng: (layer, ge[g], 0, 0)),
        ],
        out_specs=pl.BlockSpec(memory_space=pl.ANY),
        scratch_shapes=[
            pltpu.VMEM((rows, d), F32), pltpu.VMEM((rows, d), BF16), pltpu.VMEM((rows, d), F32),
            pltpu.SemaphoreType.DMA(()), pltpu.SemaphoreType.DMA(()),
        ],
    )
    est = rows * d * 10 + 2 * 3 * d * tf * 4 + 3 * d * tf * 2
    return pl.pallas_call(
        functools.partial(_moe_kernel, tf=tf, nc=nc),
        grid_spec=grid_spec,
        out_shape=jax.ShapeDtypeStruct((p_rows, d), F32),
        compiler_params=_cparams(("arbitrary", "arbitrary"), est),
        name="moe_experts",
    )(tok_sorted, g_exp, g_row, g_n, n_groups,
      u, w_gu, w_gu, b_gu.reshape(n_layers, e, 1, f2), b_gu.reshape(n_layers, e, 1, f2), w_down,
      b_down.reshape(n_layers, e, 1, d))


def _combine_kernel(pos_ref, ys_hbm, h_ref, gate_ref, g2_ref, fg_ref, o_ref, rows_ref, sem, *, tm, top_k, final_norm):
    i = pl.program_id(0)
    n_tiles = pl.num_programs(0)
    unroll = 64

    def start_tile(tile, slot):
        def body(ch, carry):
            j0 = pl.multiple_of(ch * unroll, unroll)
            for u in range(unroll):
                src = pos_ref[tile * (tm * top_k) + j0 + u]
                pltpu.make_async_copy(ys_hbm.at[pl.ds(src, 1)], rows_ref.at[slot, pl.ds(j0 + u, 1)], sem.at[slot]).start()
            return carry
        lax.fori_loop(0, tm * top_k // unroll, body, 0)

    @pl.when(i == 0)
    def _():
        start_tile(0, 0)

    @pl.when(i + 1 < n_tiles)
    def _():
        start_tile(i + 1, (i + 1) % 2)

    slot = i % 2
    pltpu.make_async_copy(rows_ref.at[slot], rows_ref.at[slot], sem.at[slot]).wait()
    gates = gate_ref[...]
    acc = rows_ref[slot, pl.ds(0, tm), :] * gates[:, 0:1]
    for k in range(1, top_k):
        acc = acc + rows_ref[slot, pl.ds(k * tm, tm), :] * gates[:, k:k + 1]
    out = h_ref[...] + g2_ref[...] * acc
    if final_norm:
        out = _rmsnorm_f32(out, fg_ref[...])
    o_ref[...] = out


def moe_combine(ys, pos, h, gates, g2, final_g, cfg, *, final_norm):
    t, d = h.shape
    top_k = cfg.top_k
    tm = min(cfg.comb_tm, t)
    assert (tm * top_k) % 64 == 0
    pos = pos.reshape(t // tm, tm, top_k).transpose(0, 2, 1).reshape(-1)
    vec = pl.BlockSpec((1, d), lambda i, p: (0, 0))
    grid_spec = pltpu.PrefetchScalarGridSpec(
        num_scalar_prefetch=1,
        grid=(t // tm,),
        in_specs=[pl.BlockSpec(memory_space=pl.ANY),
                  pl.BlockSpec((tm, d), lambda i, p: (i, 0)),
                  pl.BlockSpec((tm, top_k), lambda i, p: (i, 0)), vec, vec],
        out_specs=pl.BlockSpec((tm, d), lambda i, p: (i, 0)),
        scratch_shapes=[pltpu.VMEM((2, tm * top_k, d), F32), pltpu.SemaphoreType.DMA((2,))],
    )
    return pl.pallas_call(
        functools.partial(_combine_kernel, tm=tm, top_k=top_k, final_norm=final_norm),
        grid_spec=grid_spec,
        out_shape=jax.ShapeDtypeStruct((t, d), F32),
        compiler_params=_cparams(("arbitrary",), 2 * tm * top_k * d * 4 + 6 * tm * d * 4),
        name="moe_combine",
    )(pos, ys, h, gates, g2, final_g)


def moe_layer(h, g, sc, sh, g2, w_router, b_router, layer, w_gu, b_gu, w_down, b_down, final_g, cfg, *, final_norm):
    u, idx, gates = router(h, g, sc, sh, w_router, b_router.reshape(1, -1), cfg)
    tok_sorted, pos, g_exp, g_row, g_n, n_groups, p_rows, g_max = _routing_tables(idx, cfg)
    ys = moe_experts(u, tok_sorted, g_exp, g_row, g_n, n_groups, layer, w_gu, b_gu, w_down, b_down, p_rows, g_max, cfg)
    return moe_combine(ys, pos, h, gates, g2, final_g, cfg, final_norm=final_norm)


def _diff_mixer(h, pos, g, sc, sh, gate, w_qkv, lq1, lk1, lq2, lk2, sub_g, w_o, rel_bias, layer_idx, cfg):
    heads, hd = cfg.diff_heads, cfg.diff_head_dim
    qk_w = heads * 2 * hd
    lam_init = 0.8 - 0.6 * math.exp(-0.3 * (layer_idx - 1))
    lam = (jnp.exp(jnp.sum(lq1 * lk1)) - jnp.exp(jnp.sum(lq2 * lk2)) + lam_init).reshape(1).astype(F32)
    n = w_qkv.shape[1]
    col_scale = jnp.concatenate([jnp.full((qk_w,), hd ** -0.5 * LOG2E, F32), jnp.ones((n - qk_w,), F32)]).reshape(1, n)
    qkv = norm_linear(h, 0, h.shape[1], g, sc, sh, w_qkv.astype(BF16), col_scale, BF16, cfg,
                      modulate=True, name="diff_qkv_proj")
    t = min(cfg.attn_t, h.shape[0])
    tiles = diff_bias_tiles(rel_bias, heads, t)
    kinds = _block_kinds(pos, t, REL_FAR, toeplitz=True)
    o = diff_attention(qkv, tiles, kinds, pos, rel_bias, lam, sub_g.reshape(1, -1), 1.0 - lam_init, cfg)
    return linear_residual(o, w_o.astype(BF16), h, gate, cfg, name="diff_out_proj")


def _mla_mixer(h, pos, g, sc, sh, gate, w_in, q_norm_g, kv_norm_g, w_uq, w_ukv, w_o, cfg):
    d = h.shape[1]
    heads, nope, rope, qr, kvr = cfg.mla_heads, cfg.mla_nope, cfg.mla_rope, cfg.mla_q_rank, cfg.mla_kv_rank
    assert nope == LANE and cfg.mla_v == LANE and rope <= LANE and qr % LANE == 0 and kvr == qr
    w_in_p = jnp.concatenate([w_in, jnp.zeros((d, LANE - rope), w_in.dtype)], axis=1).astype(BF16)
    ones = lambda n: jnp.ones((1, n), F32)
    z = norm_linear(h, 0, d, g, sc, sh, w_in_p, ones(w_in_p.shape[1]), F32, cfg, modulate=True, name="mla_down_proj")
    w_q = w_uq.reshape(qr, heads, nope + rope)
    w_q = jnp.concatenate([w_q, jnp.zeros((qr, heads, 2 * LANE - nope - rope), w_uq.dtype)], axis=2)
    w_q = w_q.reshape(qr, heads * 2 * LANE).astype(BF16)
    zeros_k = jnp.zeros((1, qr), F32)
    q_scale = jnp.full((1, heads * 2 * LANE), (nope + rope) ** -0.5 * LOG2E, F32)
    q_cat = norm_linear(z, 0, qr, q_norm_g.reshape(1, -1), zeros_k, zeros_k, w_q, q_scale, BF16, cfg,
                        modulate=False, name="mla_q_up_proj")
    kv = norm_linear(z, 1, kvr, kv_norm_g.reshape(1, -1), zeros_k, zeros_k, w_ukv.astype(BF16),
                     ones(w_ukv.shape[1]), BF16, cfg, modulate=False, name="mla_kv_up_proj")
    tables = _rope_tables(pos, rope)
    kr_rot = rope_rows(z, (qr + kvr) // LANE, tables, cfg)
    t = min(cfg.attn_t, h.shape[0])
    kinds = _block_kinds(pos, t, 0, toeplitz=False)
    o = mla_attention(q_cat, kv, kr_rot, tables, kinds, pos, cfg)
    return linear_residual(o, w_o.astype(BF16), h, gate, cfg, name="mla_out_proj")


def _forward(cfg, x, c, positions, ada_w, ada_b, norm1_g, norm2_g, final_g, rel_bias,
             diff_w_qkv, diff_lq1, diff_lk1, diff_lq2, diff_lk2, diff_sub_g, diff_w_o,
             mla_w_in, mla_q_norm_g, mla_kv_norm_g, mla_w_uq, mla_w_ukv, mla_w_o,
             router_w, router_b, exp_w_gu, exp_b_gu, exp_w_down, exp_b_down):
    b, s, d = x.shape
    assert b == 1, "kernels are written for a single sequence"
    h = x.reshape(s, d)
    pos = positions.reshape(s).astype(I32)
    mod = ada_modulation(c.reshape(d, 1), ada_w, ada_b, cfg)
    fg = final_g.reshape(1, d)
    for i in range(cfg.depth):
        sh1, sc1, g1, sh2, sc2, g2 = [mod[i, :, j * d:(j + 1) * d] for j in range(6)]
        n1 = norm1_g[i].reshape(1, d)
        j = i // 2
        if i % 2 == 0:
            h = _diff_mixer(h, pos, n1, sc1, sh1, g1, diff_w_qkv[j], diff_lq1[j], diff_lk1[j], diff_lq2[j],
                            diff_lk2[j], diff_sub_g[j], diff_w_o[j], rel_bias, i + 1, cfg)
        else:
            h = _mla_mixer(h, pos, n1, sc1, sh1, g1, mla_w_in[j], mla_q_norm_g[j], mla_kv_norm_g[j],
                           mla_w_uq[j], mla_w_ukv[j], mla_w_o[j], cfg)
        h = moe_layer(h, norm2_g[i].reshape(1, d), sc2, sh2, g2, router_w[i], router_b[i], i, exp_w_gu, exp_b_gu,
                      exp_w_down, exp_b_down, fg, cfg, final_norm=(i == cfg.depth - 1))
    return h.reshape(b, s, d)


def kernel(x, c, positions, ada_w, ada_b, norm1_g, norm2_g, final_g, rel_bias, diff_w_qkv, diff_lq1, diff_lk1, diff_lq2, diff_lk2, diff_sub_g, diff_w_o, mla_w_in, mla_q_norm_g, mla_kv_norm_g, mla_w_uq, mla_w_ukv, mla_w_o, router_w, router_b, exp_w_gu, exp_b_gu, exp_w_down, exp_b_down):
    return _forward(Cfg(), x, c, positions, ada_w, ada_b, norm1_g, norm2_g, final_g, rel_bias,
                    diff_w_qkv, diff_lq1, diff_lk1, diff_lq2, diff_lk2, diff_sub_g, diff_w_o,
                    mla_w_in, mla_q_norm_g, mla_kv_norm_g, mla_w_uq, mla_w_ukv, mla_w_o,
                    router_w, router_b, exp_w_gu, exp_b_gu, exp_w_down, exp_b_down)
```

```python
import dataclasses
import functools
import math

import jax
import jax.numpy as jnp
import numpy as np
from jax import lax
from jax.experimental import pallas as pl
from jax.experimental.pallas import tpu as pltpu

F32 = jnp.float32
BF16 = jnp.bfloat16
I32 = jnp.int32

RMS_EPS = 1e-6
NEG_INF = -1e30
LOG2E = math.log2(math.e)
ROPE_THETA = 10000.0
REL_BUCKETS = 32
REL_MAX_DIST = 128
REL_FAR = 113
SWIGLU_LIMIT = 7.0
SWIGLU_ALPHA = 1.702
Q_BLOCK = 128
MOE_BLOCK = 128
LANE = 128
V7X_VMEM_BYTES = 64 * 1024 * 1024


@dataclasses.dataclass(frozen=True)
class Cfg:
    d_model: int = 2048
    seq: int = 8192
    depth: int = 2
    diff_heads: int = 8
    diff_head_dim: int = 128
    mla_heads: int = 16
    mla_q_rank: int = 512
    mla_kv_rank: int = 512
    mla_nope: int = 128
    mla_rope: int = 64
    mla_v: int = 128
    n_experts: int = 32
    top_k: int = 4
    d_ff: int = 2048
    ada_tn: int = 1024
    lin_tm: int = 1024
    lin_tn: int = 1024
    diff_t: int = 512
    diff_hb: int = 4
    mla_t: int = 1024
    mla_hb: int = 4
    router_tm: int = 256
    moe_rows: int = 1280
    moe_tf: int = 256
    comb_tm: int = 128


def _vmem_limit(nbytes):
    return int(min(nbytes * 1.25 + (6 << 20), V7X_VMEM_BYTES - (6 << 20)))


def _cparams(sem, nbytes):
    return pltpu.CompilerParams(dimension_semantics=sem, vmem_limit_bytes=_vmem_limit(nbytes))


def _ada_kernel(c_ref, w_ref, b_ref, o_ref):
    c = c_ref[...]
    cs = c * jax.nn.sigmoid(c)
    o_ref[...] = jnp.sum(w_ref[...] * cs, axis=0, keepdims=True) + b_ref[...]


def ada_modulation(c_col, ada_w, ada_b, cfg):
    depth, d, n = ada_w.shape
    tn = min(cfg.ada_tn, n)
    assert n % tn == 0
    return pl.pallas_call(
        _ada_kernel,
        grid=(depth, n // tn),
        in_specs=[
            pl.BlockSpec((d, 1), lambda l, j: (0, 0)),
            pl.BlockSpec((None, d, tn), lambda l, j: (l, 0, j)),
            pl.BlockSpec((None, 1, tn), lambda l, j: (l, 0, j)),
        ],
        out_specs=pl.BlockSpec((None, 1, tn), lambda l, j: (l, 0, j)),
        out_shape=jax.ShapeDtypeStruct((depth, 1, n), F32),
        compiler_params=_cparams(("arbitrary", "arbitrary"), 2 * d * tn * 4 + d * tn * 4),
        name="ada_modulation",
    )(c_col, ada_w, ada_b.reshape(depth, 1, n))


def _rmsnorm_f32(x, g):
    return x * lax.rsqrt(jnp.mean(x * x, axis=-1, keepdims=True) + RMS_EPS) * g


def _norm_linear_kernel(x_ref, g_ref, sc_ref, sh_ref, w_ref, cs_ref, o_ref, xn_ref, *, modulate):
    @pl.when(pl.program_id(1) == 0)
    def _():
        y = _rmsnorm_f32(x_ref[...].astype(F32), g_ref[...])
        if modulate:
            y = y * (1.0 + sc_ref[...]) + sh_ref[...]
        xn_ref[...] = y.astype(BF16)

    acc = jnp.dot(xn_ref[...], w_ref[...], preferred_element_type=F32)
    o_ref[...] = (acc * cs_ref[...]).astype(o_ref.dtype)


def norm_linear(x, x_col_block, k, g, sc, sh, w_bf, col_scale, out_dtype, cfg, *, modulate, name):
    m = x.shape[0]
    n = w_bf.shape[1]
    tm = min(cfg.lin_tm, m)
    tn = n if n <= cfg.lin_tn or n % cfg.lin_tn else cfg.lin_tn
    assert m % tm == 0 and n % tn == 0
    out_b = jnp.dtype(out_dtype).itemsize
    est = 2 * tm * k * 4 + tm * k * 2 + 2 * k * tn * 2 + 2 * tm * tn * out_b + tm * k * 4
    return pl.pallas_call(
        functools.partial(_norm_linear_kernel, modulate=modulate),
        grid=(m // tm, n // tn),
        in_specs=[
            pl.BlockSpec((tm, k), lambda i, j: (i, x_col_block)),
            pl.BlockSpec((1, k), lambda i, j: (0, 0)),
            pl.BlockSpec((1, k), lambda i, j: (0, 0)),
            pl.BlockSpec((1, k), lambda i, j: (0, 0)),
            pl.BlockSpec((k, tn), lambda i, j: (0, j)),
            pl.BlockSpec((1, tn), lambda i, j: (0, j)),
        ],
        out_specs=pl.BlockSpec((tm, tn), lambda i, j: (i, j)),
        out_shape=jax.ShapeDtypeStruct((m, n), out_dtype),
        scratch_shapes=[pltpu.VMEM((tm, k), BF16)],
        compiler_params=_cparams(("arbitrary", "arbitrary"), est),
        name=name,
    )(x, g, sc, sh, w_bf, col_scale)


def _linear_res_kernel(a_ref, w_ref, h_ref, g_ref, o_ref):
    acc = jnp.dot(a_ref[...], w_ref[...], preferred_element_type=F32)
    o_ref[...] = h_ref[...] + g_ref[...] * acc


def linear_residual(a_bf, w_bf, h, gate, cfg, *, name):
    m, k = a_bf.shape
    n = w_bf.shape[1]
    tm = min(cfg.lin_tm, m)
    tn = min(cfg.lin_tn, n)
    assert m % tm == 0 and n % tn == 0
    est = 2 * tm * k * 2 + 2 * k * tn * 2 + 4 * tm * tn * 4
    return pl.pallas_call(
        _linear_res_kernel,
        grid=(m // tm, n // tn),
        in_specs=[
            pl.BlockSpec((tm, k), lambda i, j: (i, 0)),
            pl.BlockSpec((k, tn), lambda i, j: (0, j)),
            pl.BlockSpec((tm, tn), lambda i, j: (i, j)),
            pl.BlockSpec((1, tn), lambda i, j: (0, j)),
        ],
        out_specs=pl.BlockSpec((tm, tn), lambda i, j: (i, j)),
        out_shape=jax.ShapeDtypeStruct((m, n), F32),
        compiler_params=_cparams(("arbitrary", "arbitrary"), est),
        name=name,
    )(a_bf, w_bf, h, gate)


KIND_SKIP, KIND_PLAIN, KIND_DIAG, KIND_OFFDIAG, KIND_GENERAL = 0, 1, 2, 3, 4


def _block_kinds(pos, t, far_dist, toeplitz):
    s = pos.shape[0]
    nb = s // t
    pb = pos.reshape(nb, t)
    pmin, pmax = pb.min(axis=1), pb.max(axis=1)
    consecutive = jnp.all(pb == pb[:, :1] + jnp.arange(t, dtype=pos.dtype)[None, :], axis=1)
    qi = jnp.arange(nb)[:, None]
    ki = jnp.arange(nb)[None, :]
    gap = pmin[:, None] - pmax[None, :]
    plain = (ki < qi) & (gap >= (far_dist if far_dist else 0))
    kinds = jnp.where(plain, KIND_PLAIN, KIND_GENERAL)
    if toeplitz:
        both = consecutive[:, None] & consecutive[None, :]
        d = pb[:, 0][:, None] - pb[:, 0][None, :]
        kinds = jnp.where(~plain & both & (ki == qi - 1) & (d == t), KIND_OFFDIAG, kinds)
        kinds = jnp.where(both & (ki == qi) & (d == 0), KIND_DIAG, kinds)
    return jnp.where(ki > qi, KIND_SKIP, kinds).astype(I32)


def _attn_pairs(kinds2d, nq):
    qi = np.concatenate([np.full(q + 1, q, np.int32) for q in range(nq)])
    ki = np.concatenate([np.arange(q + 1, dtype=np.int32) for q in range(nq)])
    return jnp.asarray(qi), jnp.asarray(ki), kinds2d[qi, ki].astype(I32)


def _t5_bias_minus_last(rel, rb_ref, h):
    n = jnp.maximum(rel, 0)
    max_exact = REL_BUCKETS // 2
    nf = jnp.maximum(n, 1).astype(F32)
    large = max_exact + (jnp.log(nf / max_exact) / math.log(REL_MAX_DIST / max_exact)
                         * (REL_BUCKETS - max_exact)).astype(I32)
    large = jnp.minimum(large, REL_BUCKETS - 1)
    bucket = jnp.where(n < max_exact, n, large)
    last = rb_ref[REL_BUCKETS - 1, h]
    out = jnp.zeros(rel.shape, F32)
    for b in range(REL_BUCKETS - 1):
        out = jnp.where(bucket == b, rb_ref[b, h] - last, out)
    return out


def _bias_tile_kernel(rb_ref, o_ref, *, t):
    kind = pl.program_id(0)
    h = pl.program_id(1)
    rel = kind * t + lax.broadcasted_iota(I32, (t, t), 0) - lax.broadcasted_iota(I32, (t, t), 1)
    bias = _t5_bias_minus_last(rel, rb_ref, h) * LOG2E
    o_ref[...] = jnp.where(rel >= 0, bias, NEG_INF)


def diff_bias_tiles(rel_bias, heads, t):
    return pl.pallas_call(
        functools.partial(_bias_tile_kernel, t=t),
        grid=(2, heads),
        in_specs=[pl.BlockSpec(memory_space=pltpu.SMEM)],
        out_specs=pl.BlockSpec((None, None, t, t), lambda k, h: (k, h, 0, 0)),
        out_shape=jax.ShapeDtypeStruct((2, heads, t, t), F32),
        compiler_params=_cparams(("arbitrary", "arbitrary"), 8 * t * t * 4),
        name="diff_bias_tiles",
    )(rel_bias)


def _structural_ok(qi, ki, t):
    q_idx = qi * t + lax.broadcasted_iota(I32, (t, t), 0)
    k_idx = ki * t + lax.broadcasted_iota(I32, (t, t), 1)
    return k_idx < ((q_idx // Q_BLOCK) + 1) * Q_BLOCK


def _online_softmax_step(s, v, m_ref, l_ref, acc_ref):
    nl = s.shape[1] // LANE
    m_prev = m_ref[...]
    m_new = jnp.maximum(m_prev, jnp.max(s, axis=-1, keepdims=True))
    alpha = jnp.exp2(m_prev - m_new)
    p = jnp.exp2(s - jnp.tile(m_new, (1, nl)))
    psum = p[:, :LANE]
    for j in range(1, nl):
        psum = psum + p[:, j * LANE:(j + 1) * LANE]
    l_ref[...] = alpha * l_ref[...] + psum
    pv = jnp.dot(p.astype(BF16), v, preferred_element_type=F32)
    acc_ref[...] = jnp.tile(alpha, (1, v.shape[1] // LANE)) * acc_ref[...] + pv
    m_ref[...] = m_new


def _diff_attn_kernel(qi_ref, ki_ref, kind_ref, q_ref, k_ref, v_ref, bt_ref, pq_ref, pk_ref, rb_ref, lam_ref, sg_ref,
                      o_ref, m_ref, l_ref, acc_ref, *, t, hd, hb, out_scale):
    hg, pair = pl.program_id(0), pl.program_id(1)
    qi, ki, kind = qi_ref[pair], ki_ref[pair], kind_ref[pair]
    w = 2 * hd

    @pl.when(ki == 0)
    def _():
        m_ref[...] = jnp.full(m_ref.shape, -jnp.inf, F32)
        l_ref[...] = jnp.zeros(l_ref.shape, F32)
        acc_ref[...] = jnp.zeros(acc_ref.shape, F32)

    def step(adjust):
        for h in range(hb):
            v = v_ref[:, h * w:(h + 1) * w]
            for mp in range(2):
                c0 = h * w + mp * hd
                s = lax.dot_general(q_ref[:, c0:c0 + hd], k_ref[:, c0:c0 + hd], (((1,), (1,)), ((), ())),
                                    preferred_element_type=F32)
                j = 2 * h + mp
                _online_softmax_step(adjust(s, h), v, m_ref.at[j], l_ref.at[j], acc_ref.at[j])

    @pl.when(kind == KIND_PLAIN)
    def _():
        step(lambda s, h: s)

    @pl.when((kind == KIND_DIAG) | (kind == KIND_OFFDIAG))
    def _():
        step(lambda s, h: s + bt_ref[h])

    @pl.when(kind == KIND_GENERAL)
    def _():
        rel = pq_ref[...] - pk_ref[...]
        struct = _structural_ok(qi, ki, t)

        def adjust(s, h):
            add = _t5_bias_minus_last(rel, rb_ref, hg * hb + h) * LOG2E
            s = jnp.where(rel >= 0, s + add, NEG_INF)
            return jnp.where(struct, s, -jnp.inf)
        step(adjust)

    @pl.when(ki == qi)
    def _():
        for h in range(hb):
            l0 = jnp.sum(l_ref[2 * h], axis=-1, keepdims=True)
            l1 = jnp.sum(l_ref[2 * h + 1], axis=-1, keepdims=True)
            o = acc_ref[2 * h] / l0 - lam_ref[0] * (acc_ref[2 * h + 1] / l1)
            o_ref[:, h * w:(h + 1) * w] = (_rmsnorm_f32(o, sg_ref[...]) * out_scale).astype(o_ref.dtype)


def diff_attention(qkv, bias_tiles, kinds2d, pos, rel_bias, lam, sub_g, out_scale, cfg):
    s = qkv.shape[0]
    t = min(cfg.diff_t, s)
    nq = s // t
    heads, hd, hb = cfg.diff_heads, cfg.diff_head_dim, cfg.diff_hb
    assert heads % hb == 0
    ng = heads // hb
    w = 2 * hd
    qi_tab, ki_tab, kind_tab = _attn_pairs(kinds2d, nq)

    grid_spec = pltpu.PrefetchScalarGridSpec(
        num_scalar_prefetch=3,
        grid=(ng, qi_tab.shape[0]),
        in_specs=[
            pl.BlockSpec((t, hb * w), lambda g, p, qt, kt, kd: (qt[p], g)),
            pl.BlockSpec((t, hb * w), lambda g, p, qt, kt, kd: (kt[p], ng + g)),
            pl.BlockSpec((t, hb * w), lambda g, p, qt, kt, kd: (kt[p], 2 * ng + g)),
            pl.BlockSpec((None, hb, t, t), lambda g, p, qt, kt, kd: (jnp.where(kd[p] == KIND_OFFDIAG, 1, 0), g, 0, 0)),
            pl.BlockSpec((t, 1), lambda g, p, qt, kt, kd: (qt[p], 0)),
            pl.BlockSpec((1, t), lambda g, p, qt, kt, kd: (0, kt[p])),
            pl.BlockSpec(memory_space=pltpu.SMEM),
            pl.BlockSpec(memory_space=pltpu.SMEM),
            pl.BlockSpec((1, w), lambda g, p, qt, kt, kd: (0, 0)),
        ],
        out_specs=pl.BlockSpec((t, hb * w), lambda g, p, qt, kt, kd: (qt[p], g)),
        scratch_shapes=[pltpu.VMEM((2 * hb, t, LANE), F32), pltpu.VMEM((2 * hb, t, LANE), F32),
                        pltpu.VMEM((2 * hb, t, w), F32)],
    )
    est = hb * (2 * (4 * t * w * 2 + t * t * 4) + 2 * t * w * 4 + 4 * t * LANE * 4) + 10 * t * t * 4
    return pl.pallas_call(
        functools.partial(_diff_attn_kernel, t=t, hd=hd, hb=hb, out_scale=out_scale),
        grid_spec=grid_spec,
        out_shape=jax.ShapeDtypeStruct((s, heads * w), BF16),
        compiler_params=_cparams(("arbitrary", "arbitrary"), est),
        name="diff_attention",
    )(qi_tab, ki_tab, kind_tab, qkv, qkv, qkv, bias_tiles, pos.reshape(s, 1), pos.reshape(1, s), rel_bias, lam, sub_g)


def _rope_slot(x, c, s1, s2):
    return x * c + pltpu.roll(x, LANE - 32, 1) * s1 + pltpu.roll(x, 32, 1) * s2


def _rope_tables(pos, rope_dim):
    half = rope_dim // 2
    inv = ROPE_THETA ** (-jnp.arange(half, dtype=F32) / half)
    ang = pos.astype(F32)[:, None] * inv
    cos, sin = jnp.cos(ang), jnp.sin(ang)
    z = jnp.zeros_like(cos)
    pad = jnp.zeros((pos.shape[0], LANE - 2 * half), F32)
    c = jnp.concatenate([cos, cos, pad], axis=1)
    s1 = jnp.concatenate([-sin, z, pad], axis=1)
    s2 = jnp.concatenate([z, sin, pad], axis=1)
    return c, s1, s2


def _rope_rows_kernel(x_ref, c_ref, s1_ref, s2_ref, o_ref):
    o_ref[...] = _rope_slot(x_ref[...].astype(F32), c_ref[...], s1_ref[...], s2_ref[...]).astype(o_ref.dtype)


def rope_rows(x, col_block, tables, cfg):
    m = x.shape[0]
    tm = min(cfg.lin_tm, m)
    row = pl.BlockSpec((tm, LANE), lambda i: (i, 0))
    return pl.pallas_call(
        _rope_rows_kernel,
        grid=(m // tm,),
        in_specs=[pl.BlockSpec((tm, LANE), lambda i: (i, col_block)), row, row, row],
        out_specs=row,
        out_shape=jax.ShapeDtypeStruct((m, LANE), BF16),
        compiler_params=_cparams(("arbitrary",), 10 * tm * LANE * 4),
        name="mla_rope_key",
    )(x, *tables)


def _mla_attn_kernel(qi_ref, ki_ref, kind_ref, q_ref, kv_ref, kr_ref, c_ref, s1_ref, s2_ref, pq_ref, pk_ref, o_ref,
                     qs_ref, kc_ref, m_ref, l_ref, acc_ref, *, t, hb):
    pair = pl.program_id(1)
    qi, ki, kind = qi_ref[pair], ki_ref[pair], kind_ref[pair]
    w = 2 * LANE

    @pl.when(ki == 0)
    def _():
        m_ref[...] = jnp.full(m_ref.shape, -jnp.inf, F32)
        l_ref[...] = jnp.zeros(l_ref.shape, F32)
        acc_ref[...] = jnp.zeros(acc_ref.shape, F32)
        for h in range(hb):
            qs_ref[h, :, :LANE] = q_ref[:, h * w:h * w + LANE]
            qr = _rope_slot(q_ref[:, h * w + LANE:(h + 1) * w].astype(F32), c_ref[...], s1_ref[...], s2_ref[...])
            qs_ref[h, :, LANE:] = qr.astype(BF16)

    def step(adjust):
        for h in range(hb):
            kc_ref[h, :, :LANE] = kv_ref[:, h * w:h * w + LANE]
            kc_ref[h, :, LANE:] = kr_ref[...]
            s = lax.dot_general(qs_ref[h], kc_ref[h], (((1,), (1,)), ((), ())), preferred_element_type=F32)
            _online_softmax_step(adjust(s), kv_ref[:, h * w + LANE:(h + 1) * w], m_ref.at[h], l_ref.at[h], acc_ref.at[h])

    @pl.when(kind == KIND_PLAIN)
    def _():
        step(lambda s: s)

    @pl.when(kind == KIND_GENERAL)
    def _():
        rel = pq_ref[...] - pk_ref[...]
        struct = _structural_ok(qi, ki, t)

        def adjust(s):
            s = jnp.where(rel >= 0, s, NEG_INF)
            return jnp.where(struct, s, -jnp.inf)
        step(adjust)

    @pl.when(ki == qi)
    def _():
        for h in range(hb):
            l = jnp.sum(l_ref[h], axis=-1, keepdims=True)
            o_ref[:, h * LANE:(h + 1) * LANE] = (acc_ref[h] / l).astype(o_ref.dtype)


def mla_attention(q_cat, kv, kr_rot, tables, kinds2d, pos, cfg):
    s = q_cat.shape[0]
    t = min(cfg.mla_t, s)
    nq = s // t
    heads, hb = cfg.mla_heads, cfg.mla_hb
    assert heads % hb == 0
    w = 2 * LANE
    qi_tab, ki_tab, kind_tab = _attn_pairs(kinds2d, nq)
    qrow = pl.BlockSpec((t, LANE), lambda g, p, qt, kt, kd: (qt[p], 0))
    grid_spec = pltpu.PrefetchScalarGridSpec(
        num_scalar_prefetch=3,
        grid=(heads // hb, qi_tab.shape[0]),
        in_specs=[
            pl.BlockSpec((t, hb * w), lambda g, p, qt, kt, kd: (qt[p], g)),
            pl.BlockSpec((t, hb * w), lambda g, p, qt, kt, kd: (kt[p], g)),
            pl.BlockSpec((t, LANE), lambda g, p, qt, kt, kd: (kt[p], 0)),
            qrow, qrow, qrow,
            pl.BlockSpec((t, 1), lambda g, p, qt, kt, kd: (qt[p], 0)),
            pl.BlockSpec((1, t), lambda g, p, qt, kt, kd: (0, kt[p])),
        ],
        out_specs=pl.BlockSpec((t, hb * LANE), lambda g, p, qt, kt, kd: (qt[p], g)),
        scratch_shapes=[pltpu.VMEM((hb, t, w), BF16), pltpu.VMEM((hb, t, w), BF16),
                        pltpu.VMEM((hb, t, LANE), F32), pltpu.VMEM((hb, t, LANE), F32), pltpu.VMEM((hb, t, LANE), F32)],
    )
    est = hb * (2 * (2 * t * w * 2 + t * LANE * 2) + 2 * t * w * 2 + 3 * t * LANE * 4) + 8 * t * LANE * 4 + 10 * t * t * 4
    return pl.pallas_call(
        functools.partial(_mla_attn_kernel, t=t, hb=hb),
        grid_spec=grid_spec,
        out_shape=jax.ShapeDtypeStruct((s, heads * cfg.mla_v), BF16),
        compiler_params=_cparams(("arbitrary", "arbitrary"), est),
        name="mla_attention",
    )(qi_tab, ki_tab, kind_tab, q_cat, kv, kr_rot, *tables, pos.reshape(s, 1), pos.reshape(1, s))


def _router_kernel(h_ref, g_ref, sc_ref, sh_ref, wr_ref, br_ref, u_ref, idx_ref, gate_ref, *, top_k):
    u = _rmsnorm_f32(h_ref[...], g_ref[...]) * (1.0 + sc_ref[...]) + sh_ref[...]
    u_ref[...] = u
    logits = jnp.dot(u, wr_ref[...], preferred_element_type=F32, precision=lax.Precision.HIGHEST) + br_ref[...]
    n_e = logits.shape[-1]
    lane = lax.broadcasted_iota(I32, logits.shape, 1)
    vals, idxs = [], []
    cur = logits
    for _ in range(top_k):
        mx = jnp.max(cur, axis=-1, keepdims=True)
        ix = jnp.min(jnp.where(cur == mx, lane, n_e), axis=-1, keepdims=True)
        vals.append(mx)
        idxs.append(ix)
        cur = jnp.where(lane == ix, -jnp.inf, cur)
    v = jnp.concatenate(vals, axis=1)
    e = jnp.exp(v - vals[0])
    gate_ref[...] = e / jnp.sum(e, axis=-1, keepdims=True)
    idx_ref[...] = jnp.concatenate(idxs, axis=1)


def router(h, g, sc, sh, w_router, b_router, cfg):
    t, d = h.shape
    e = w_router.shape[1]
    tm = min(cfg.router_tm, t)
    vec = pl.BlockSpec((1, d), lambda i: (0, 0))
    return pl.pallas_call(
        functools.partial(_router_kernel, top_k=cfg.top_k),
        grid=(t // tm,),
        in_specs=[pl.BlockSpec((tm, d), lambda i: (i, 0)), vec, vec, vec,
                  pl.BlockSpec((d, e), lambda i: (0, 0)), pl.BlockSpec((1, e), lambda i: (0, 0))],
        out_specs=[pl.BlockSpec((tm, d), lambda i: (i, 0)),
                   pl.BlockSpec((tm, cfg.top_k), lambda i: (i, 0)),
                   pl.BlockSpec((tm, cfg.top_k), lambda i: (i, 0))],
        out_shape=[jax.ShapeDtypeStruct((t, d), F32),
                   jax.ShapeDtypeStruct((t, cfg.top_k), I32),
                   jax.ShapeDtypeStruct((t, cfg.top_k), F32)],
        compiler_params=_cparams(("arbitrary",), 6 * tm * d * 4 + 2 * d * LANE * 4),
        name="moe_router",
    )(h, g, sc, sh, w_router, b_router)


def _row_table_kernel(pos_ref, o_ref, *, top_k, unroll):
    n_rows = o_ref.shape[0]
    n_assign = pos_ref.shape[0]

    def zero(i, carry):
        for u in range(unroll):
            o_ref[i * unroll + u] = 0
        return carry
    lax.fori_loop(0, n_rows // unroll, zero, 0)

    def scatter(i, carry):
        for u in range(unroll):
            a = i * unroll + u
            o_ref[pos_ref[a]] = a // top_k
        return carry
    lax.fori_loop(0, n_assign // unroll, scatter, 0)


def moe_row_table(pos, n_rows, top_k):
    unroll = 8
    assert n_rows % unroll == 0 and pos.shape[0] % unroll == 0
    return pl.pallas_call(
        functools.partial(_row_table_kernel, top_k=top_k, unroll=unroll),
        in_specs=[pl.BlockSpec(memory_space=pltpu.SMEM)],
        out_specs=pl.BlockSpec(memory_space=pltpu.SMEM),
        out_shape=jax.ShapeDtypeStruct((n_rows,), I32),
        name="moe_row_table",
    )(pos)


def _routing_tables(idx, cfg):
    t, k = idx.shape
    a = t * k
    e = cfg.n_experts
    r = cfg.moe_rows
    p_rows = a + e * MOE_BLOCK
    g_max = e + a // r + 1
    e_flat = idx.reshape(a)
    onehot = (e_flat[:, None] == jnp.arange(e, dtype=I32)[None, :]).astype(I32)
    csum = jnp.cumsum(onehot, axis=0)
    counts = csum[-1]
    rank = jnp.sum(onehot * csum, axis=1) - 1
    padded = ((counts + MOE_BLOCK - 1) // MOE_BLOCK) * MOE_BLOCK
    pad_start = jnp.cumsum(padded) - padded
    pos = (pad_start[e_flat] + rank).astype(I32)
    tok_sorted = moe_row_table(pos, p_rows + 2 * r, k)
    n_grp = (padded + r - 1) // r
    cum = jnp.cumsum(n_grp)
    n_groups = cum[-1]
    gid = jnp.arange(g_max, dtype=I32)
    last = jnp.maximum(n_groups - 1, 0)
    gid_c = jnp.minimum(gid, last)
    g_exp = jnp.minimum(jnp.searchsorted(cum, gid_c, side="right"), e - 1).astype(I32)
    local = gid_c - (cum - n_grp)[g_exp]
    g_row = (pad_start[g_exp] + local * r).astype(I32)
    g_n = jnp.clip(padded[g_exp] - local * r, 0, r).astype(I32)
    g_n = jnp.where(gid < n_groups, g_n, 0)
    return tok_sorted, pos, g_exp, g_row, g_n, n_groups.astype(I32), p_rows


MOE_ARM_BLOCKS = (8, 4, 2, 1)
MOE_GATHER_SHARE = 4


def _moe_kernel(tok_ref, gexp_ref, grow_ref, gn_ref,
                u_hbm, wg_ref, wu_ref, bg_ref, bu_ref, wd_ref, bd_ref, ys_hbm,
                xg_ref, xb_ref, yst_ref, cnt_ref, gsem, osem):
    del gexp_ref
    g, c = pl.program_id(0), pl.program_id(1)
    ng, nc = pl.num_programs(0), pl.num_programs(1)
    n = gn_ref[g]
    row0 = grow_ref[g]
    has_next = g + 1 < ng
    nxt = jnp.minimum(g + 1, gn_ref.shape[0] - 1)
    n_next = jnp.where(has_next, gn_ref[nxt], 0)
    base_next = grow_ref[nxt]

    def gather_row(base, idx):
        tok = tok_ref[base + idx]
        pltpu.make_async_copy(u_hbm.at[pl.ds(tok, 1)], xg_ref.at[pl.ds(idx, 1)], gsem).start()

    def wait_rows(ref, rows, sem):
        rows = pl.multiple_of(rows, MOE_BLOCK)
        pltpu.make_async_copy(ref.at[pl.ds(0, rows)], ref.at[pl.ds(0, rows)], sem).wait()

    @pl.when(c == 0)
    def _():
        @pl.when(g == 0)
        def _():
            def first(i, carry):
                gather_row(row0, i)
                return carry
            lax.fori_loop(0, n, first, 0)
        wait_rows(xg_ref, n, gsem)

        @pl.when(g > 0)
        def _():
            wait_rows(yst_ref, gn_ref[jnp.maximum(g - 1, 0)], osem)

        def cast(i, carry):
            r0 = pl.multiple_of(i * MOE_BLOCK, MOE_BLOCK)
            xb_ref[pl.ds(r0, MOE_BLOCK), :] = xg_ref[pl.ds(r0, MOE_BLOCK), :].astype(BF16)
            yst_ref[pl.ds(r0, MOE_BLOCK), :] = jnp.broadcast_to(bd_ref[...], (MOE_BLOCK, bd_ref.shape[-1]))
            return carry
        lax.fori_loop(0, n // MOE_BLOCK, cast, 0)
        cnt_ref[0] = 0

    def arm(r0, m, n_gather):
        x = xb_ref[pl.ds(r0, m), :]
        hg = jnp.dot(x, wg_ref[...].astype(BF16), preferred_element_type=F32)
        hu = jnp.dot(x, wu_ref[...].astype(BF16), preferred_element_type=F32)
        gate = jnp.minimum(hg + bg_ref[...], SWIGLU_LIMIT)
        up = jnp.clip(hu + bu_ref[...], -SWIGLU_LIMIT, SWIGLU_LIMIT)
        act = ((up + 1.0) * (gate * jax.nn.sigmoid(gate * SWIGLU_ALPHA))).astype(BF16)
        yst_ref[pl.ds(r0, m), :] += jnp.dot(act, wd_ref[...].astype(BF16), preferred_element_type=F32)
        cnt = cnt_ref[0]
        for i in range(n_gather):
            @pl.when(cnt + i < n_next)
            def _():
                gather_row(base_next, cnt + i)
        cnt_ref[0] = cnt + n_gather

        @pl.when(c == nc - 1)
        def _():
            dst0 = pl.multiple_of(row0 + r0, MOE_BLOCK)
            pltpu.make_async_copy(yst_ref.at[pl.ds(r0, m)], ys_hbm.at[pl.ds(dst0, m)], osem).start()

    nb = n // MOE_BLOCK
    top = MOE_ARM_BLOCKS[0]
    top_rows = top * MOE_BLOCK
    rows_cap = xg_ref.shape[0]

    @pl.when(nb >= top)
    def _():
        arm(0, top_rows, -(-rows_cap // nc))
        for blocks in range(1, rows_cap // MOE_BLOCK - top + 1):
            @pl.when(nb - top == blocks)
            def _():
                arm(top_rows, blocks * MOE_BLOCK, 0)

    @pl.when(nb < top)
    def _():
        for blocks in MOE_ARM_BLOCKS[1:]:
            @pl.when((nb & blocks) != 0)
            def _():
                done = (nb // (2 * blocks)) * (2 * blocks)
                arm(pl.multiple_of(done * MOE_BLOCK, blocks * MOE_BLOCK), blocks * MOE_BLOCK,
                    blocks * MOE_BLOCK // MOE_GATHER_SHARE)

    @pl.when(c == nc - 1)
    def _():
        def rest(i, carry):
            gather_row(base_next, i)
            return carry
        lax.fori_loop(jnp.minimum(cnt_ref[0], n_next), n_next, rest, 0)

        @pl.when(g == ng - 1)
        def _():
            wait_rows(yst_ref, n, osem)
            yst_ref[pl.ds(0, MOE_BLOCK), :] = jnp.zeros((MOE_BLOCK, yst_ref.shape[-1]), F32)
            first = (row0 + n) // MOE_BLOCK
            n_fill = ys_hbm.shape[0] // MOE_BLOCK - first

            def fill(i, carry):
                dst0 = pl.multiple_of((first + i) * MOE_BLOCK, MOE_BLOCK)
                pltpu.make_async_copy(yst_ref.at[pl.ds(0, MOE_BLOCK)], ys_hbm.at[pl.ds(dst0, MOE_BLOCK)], osem).start()
                return carry
            lax.fori_loop(0, n_fill, fill, 0)

            def drain(i, carry):
                pltpu.make_async_copy(yst_ref.at[pl.ds(0, MOE_BLOCK)], yst_ref.at[pl.ds(0, MOE_BLOCK)], osem).wait()
                return carry
            lax.fori_loop(0, n_fill, drain, 0)


def moe_experts(u, tok_sorted, g_exp, g_row, g_n, n_groups, layer, w_gu, b_gu, w_down, b_down, p_rows, cfg):
    t, d = u.shape
    n_layers, e, _, f2 = w_gu.shape
    f = f2 // 2
    tf = min(cfg.moe_tf, f)
    nc = f // tf
    rows = cfg.moe_rows
    assert rows % MOE_BLOCK == 0

    grid_spec = pltpu.PrefetchScalarGridSpec(
        num_scalar_prefetch=4,
        grid=(n_groups, nc),
        in_specs=[
            pl.BlockSpec(memory_space=pl.ANY),
            pl.BlockSpec((None, None, d, tf), lambda g, c, tk, ge, gr, gn: (layer, ge[g], 0, c)),
            pl.BlockSpec((None, None, d, tf), lambda g, c, tk, ge, gr, gn: (layer, ge[g], 0, nc + c)),
            pl.BlockSpec((None, None, 1, tf), lambda g, c, tk, ge, gr, gn: (layer, ge[g], 0, c)),
            pl.BlockSpec((None, None, 1, tf), lambda g, c, tk, ge, gr, gn: (layer, ge[g], 0, nc + c)),
            pl.BlockSpec((None, None, tf, d), lambda g, c, tk, ge, gr, gn: (layer, ge[g], c, 0)),
            pl.BlockSpec((None, None, 1, d), lambda g, c, tk, ge, gr, gn: (layer, ge[g], 0, 0)),
        ],
        out_specs=pl.BlockSpec(memory_space=pl.ANY),
        scratch_shapes=[
            pltpu.VMEM((rows, d), F32), pltpu.VMEM((rows, d), BF16), pltpu.VMEM((rows, d), F32),
            pltpu.SMEM((1,), I32), pltpu.SemaphoreType.DMA(()), pltpu.SemaphoreType.DMA(()),
        ],
    )
    est = rows * d * 10 + 2 * 3 * d * tf * 4 + 3 * d * tf * 2
    return pl.pallas_call(
        _moe_kernel,
        grid_spec=grid_spec,
        out_shape=jax.ShapeDtypeStruct((p_rows, d), F32),
        compiler_params=_cparams(("arbitrary", "arbitrary"), est),
        name="moe_experts",
    )(tok_sorted, g_exp, g_row, g_n,
      u, w_gu, w_gu, b_gu.reshape(n_layers, e, 1, f2), b_gu.reshape(n_layers, e, 1, f2), w_down,
      b_down.reshape(n_layers, e, 1, d))


def _combine_kernel(pos_ref, ys_hbm, h_ref, gate_ref, g2_ref, fg_ref, o_ref, rows_ref, sem, *, tm, top_k, final_norm):
    i = pl.program_id(0)
    n_tiles = pl.num_programs(0)
    unroll = 64

    def start_tile(tile, slot):
        def body(ch, carry):
            j0 = pl.multiple_of(ch * unroll, unroll)
            for u in range(unroll):
                src = pos_ref[tile * (tm * top_k) + j0 + u]
                pltpu.make_async_copy(ys_hbm.at[pl.ds(src, 1)], rows_ref.at[slot, pl.ds(j0 + u, 1)], sem.at[slot]).start()
            return carry
        lax.fori_loop(0, tm * top_k // unroll, body, 0)

    @pl.when(i == 0)
    def _():
        start_tile(0, 0)

    @pl.when(i + 1 < n_tiles)
    def _():
        start_tile(i + 1, (i + 1) % 2)

    slot = i % 2
    pltpu.make_async_copy(rows_ref.at[slot], rows_ref.at[slot], sem.at[slot]).wait()
    gates = gate_ref[...]
    acc = rows_ref[slot, pl.ds(0, tm), :] * gates[:, 0:1]
    for k in range(1, top_k):
        acc = acc + rows_ref[slot, pl.ds(k * tm, tm), :] * gates[:, k:k + 1]
    out = h_ref[...] + g2_ref[...] * acc
    if final_norm:
        out = _rmsnorm_f32(out, fg_ref[...])
    o_ref[...] = out


def moe_combine(ys, pos, h, gates, g2, final_g, cfg, *, final_norm):
    t, d = h.shape
    top_k = cfg.top_k
    tm = min(cfg.comb_tm, t)
    assert (tm * top_k) % 64 == 0
    pos = pos.reshape(t // tm, tm, top_k).transpose(0, 2, 1).reshape(-1)
    vec = pl.BlockSpec((1, d), lambda i, p: (0, 0))
    grid_spec = pltpu.PrefetchScalarGridSpec(
        num_scalar_prefetch=1,
        grid=(t // tm,),
        in_specs=[pl.BlockSpec(memory_space=pl.ANY),
                  pl.BlockSpec((tm, d), lambda i, p: (i, 0)),
                  pl.BlockSpec((tm, top_k), lambda i, p: (i, 0)), vec, vec],
        out_specs=pl.BlockSpec((tm, d), lambda i, p: (i, 0)),
        scratch_shapes=[pltpu.VMEM((2, tm * top_k, d), F32), pltpu.SemaphoreType.DMA((2,))],
    )
    return pl.pallas_call(
        functools.partial(_combine_kernel, tm=tm, top_k=top_k, final_norm=final_norm),
        grid_spec=grid_spec,
        out_shape=jax.ShapeDtypeStruct((t, d), F32),
        compiler_params=_cparams(("arbitrary",), 2 * tm * top_k * d * 4 + 6 * tm * d * 4),
        name="moe_combine",
    )(pos, ys, h, gates, g2, final_g)


def moe_layer(h, g, sc, sh, g2, w_router, b_router, layer, w_gu, b_gu, w_down, b_down, final_g, cfg, *, final_norm):
    u, idx, gates = router(h, g, sc, sh, w_router, b_router.reshape(1, -1), cfg)
    tok_sorted, pos, g_exp, g_row, g_n, n_groups, p_rows = _routing_tables(idx, cfg)
    ys = moe_experts(u, tok_sorted, g_exp, g_row, g_n, n_groups, layer, w_gu, b_gu, w_down, b_down, p_rows, cfg)
    return moe_combine(ys, pos, h, gates, g2, final_g, cfg, final_norm=final_norm)


def _diff_mixer(h, pos, g, sc, sh, gate, w_qkv, lq1, lk1, lq2, lk2, sub_g, w_o, rel_bias, layer_idx, cfg):
    heads, hd = cfg.diff_heads, cfg.diff_head_dim
    qk_w = heads * 2 * hd
    lam_init = 0.8 - 0.6 * math.exp(-0.3 * (layer_idx - 1))
    lam = (jnp.exp(jnp.sum(lq1 * lk1)) - jnp.exp(jnp.sum(lq2 * lk2)) + lam_init).reshape(1).astype(F32)
    n = w_qkv.shape[1]
    col_scale = jnp.concatenate([jnp.full((qk_w,), hd ** -0.5 * LOG2E, F32), jnp.ones((n - qk_w,), F32)]).reshape(1, n)
    qkv = norm_linear(h, 0, h.shape[1], g, sc, sh, w_qkv.astype(BF16), col_scale, BF16, cfg,
                      modulate=True, name="diff_qkv_proj")
    t = min(cfg.diff_t, h.shape[0])
    tiles = diff_bias_tiles(rel_bias, heads, t)
    kinds = _block_kinds(pos, t, REL_FAR, toeplitz=True)
    o = diff_attention(qkv, tiles, kinds, pos, rel_bias, lam, sub_g.reshape(1, -1), 1.0 - lam_init, cfg)
    return linear_residual(o, w_o.astype(BF16), h, gate, cfg, name="diff_out_proj")


def _mla_mixer(h, pos, g, sc, sh, gate, w_in, q_norm_g, kv_norm_g, w_uq, w_ukv, w_o, cfg):
    d = h.shape[1]
    heads, nope, rope, qr, kvr = cfg.mla_heads, cfg.mla_nope, cfg.mla_rope, cfg.mla_q_rank, cfg.mla_kv_rank
    assert nope == LANE and cfg.mla_v == LANE and rope <= LANE and qr % LANE == 0 and kvr == qr
    w_in_p = jnp.concatenate([w_in, jnp.zeros((d, LANE - rope), w_in.dtype)], axis=1).astype(BF16)
    ones = lambda n: jnp.ones((1, n), F32)
    z = norm_linear(h, 0, d, g, sc, sh, w_in_p, ones(w_in_p.shape[1]), F32, cfg, modulate=True, name="mla_down_proj")
    w_q = w_uq.reshape(qr, heads, nope + rope)
    w_q = jnp.concatenate([w_q, jnp.zeros((qr, heads, 2 * LANE - nope - rope), w_uq.dtype)], axis=2)
    w_q = w_q.reshape(qr, heads * 2 * LANE).astype(BF16)
    zeros_k = jnp.zeros((1, qr), F32)
    q_scale = jnp.full((1, heads * 2 * LANE), (nope + rope) ** -0.5 * LOG2E, F32)
    q_cat = norm_linear(z, 0, qr, q_norm_g.reshape(1, -1), zeros_k, zeros_k, w_q, q_scale, BF16, cfg,
                        modulate=False, name="mla_q_up_proj")
    kv = norm_linear(z, 1, kvr, kv_norm_g.reshape(1, -1), zeros_k, zeros_k, w_ukv.astype(BF16),
                     ones(w_ukv.shape[1]), BF16, cfg, modulate=False, name="mla_kv_up_proj")
    tables = _rope_tables(pos, rope)
    kr_rot = rope_rows(z, (qr + kvr) // LANE, tables, cfg)
    t = min(cfg.mla_t, h.shape[0])
    kinds = _block_kinds(pos, t, 0, toeplitz=False)
    o = mla_attention(q_cat, kv, kr_rot, tables, kinds, pos, cfg)
    return linear_residual(o, w_o.astype(BF16), h, gate, cfg, name="mla_out_proj")


def _forward(cfg, x, c, positions, ada_w, ada_b, norm1_g, norm2_g, final_g, rel_bias,
             diff_w_qkv, diff_lq1, diff_lk1, diff_lq2, diff_lk2, diff_sub_g, diff_w_o,
             mla_w_in, mla_q_norm_g, mla_kv_norm_g, mla_w_uq, mla_w_ukv, mla_w_o,
             router_w, router_b, exp_w_gu, exp_b_gu, exp_w_down, exp_b_down):
    b, s, d = x.shape
    assert b == 1, "kernels are written for a single sequence"
    h = x.reshape(s, d)
    pos = positions.reshape(s).astype(I32)
    mod = ada_modulation(c.reshape(d, 1), ada_w, ada_b, cfg)
    fg = final_g.reshape(1, d)
    for i in range(cfg.depth):
        sh1, sc1, g1, sh2, sc2, g2 = [mod[i, :, j * d:(j + 1) * d] for j in range(6)]
        n1 = norm1_g[i].reshape(1, d)
        j = i // 2
        if i % 2 == 0:
            h = _diff_mixer(h, pos, n1, sc1, sh1, g1, diff_w_qkv[j], diff_lq1[j], diff_lk1[j], diff_lq2[j],
                            diff_lk2[j], diff_sub_g[j], diff_w_o[j], rel_bias, i + 1, cfg)
        else:
            h = _mla_mixer(h, pos, n1, sc1, sh1, g1, mla_w_in[j], mla_q_norm_g[j], mla_kv_norm_g[j],
                           mla_w_uq[j], mla_w_ukv[j], mla_w_o[j], cfg)
        h = moe_layer(h, norm2_g[i].reshape(1, d), sc2, sh2, g2, router_w[i], router_b[i], i, exp_w_gu, exp_b_gu,
                      exp_w_down, exp_b_down, fg, cfg, final_norm=(i == cfg.depth - 1))
    return h.reshape(b, s, d)


def kernel(x, c, positions, ada_w, ada_b, norm1_g, norm2_g, final_g, rel_bias, diff_w_qkv, diff_lq1, diff_lk1, diff_lq2, diff_lk2, diff_sub_g, diff_w_o, mla_w_in, mla_q_norm_g, mla_kv_norm_g, mla_w_uq, mla_w_ukv, mla_w_o, router_w, router_b, exp_w_gu, exp_b_gu, exp_w_down, exp_b_down):
    return _forward(Cfg(), x, c, positions, ada_w, ada_b, norm1_g, norm2_g, final_g, rel_bias,
                    diff_w_qkv, diff_lq1, diff_lk1, diff_lq2, diff_lk2, diff_sub_g, diff_w_o,
                    mla_w_in, mla_q_norm_g, mla_kv_norm_g, mla_w_uq, mla_w_ukv, mla_w_o,
                    router_w, router_b, exp_w_gu, exp_b_gu, exp_w_down, exp_b_down)
```

```python
import dataclasses
import functools
import math

import jax
import jax.numpy as jnp
import numpy as np
from jax import lax
from jax.experimental import pallas as pl
from jax.experimental.pallas import tpu as pltpu

F32 = jnp.float32
BF16 = jnp.bfloat16
I32 = jnp.int32

RMS_EPS = 1e-6
NEG_INF = -1e30
LOG2E = math.log2(math.e)
ROPE_THETA = 10000.0
REL_BUCKETS = 32
REL_MAX_DIST = 128
REL_FAR = 113
SWIGLU_LIMIT = 7.0
SWIGLU_ALPHA = 1.702
Q_BLOCK = 128
MOE_BLOCK = 128
LANE = 128
V7X_VMEM_BYTES = 64 * 1024 * 1024


@dataclasses.dataclass(frozen=True)
class Cfg:
    d_model: int = 2048
    seq: int = 8192
    depth: int = 2
    diff_heads: int = 8
    diff_head_dim: int = 128
    mla_heads: int = 16
    mla_q_rank: int = 512
    mla_kv_rank: int = 512
    mla_nope: int = 128
    mla_rope: int = 64
    mla_v: int = 128
    n_experts: int = 32
    top_k: int = 4
    d_ff: int = 2048
    ada_tn: int = 1024
    lin_tm: int = 1024
    lin_tn: int = 1024
    diff_t: int = 512
    diff_hb: int = 4
    mla_t: int = 1024
    mla_hb: int = 4
    router_tm: int = 256
    moe_rows: int = 1280
    moe_tf: int = 256
    comb_tm: int = 128


def _vmem_limit(nbytes):
    return int(min(nbytes * 1.25 + (6 << 20), V7X_VMEM_BYTES - (6 << 20)))


def _cparams(sem, nbytes):
    return pltpu.CompilerParams(dimension_semantics=sem, vmem_limit_bytes=_vmem_limit(nbytes))


def _ada_kernel(c_ref, w_ref, b_ref, o_ref):
    c = c_ref[...]
    cs = c * jax.nn.sigmoid(c)
    o_ref[...] = jnp.sum(w_ref[...] * cs, axis=0, keepdims=True) + b_ref[...]


def ada_modulation(c_col, ada_w, ada_b, cfg):
    depth, d, n = ada_w.shape
    tn = min(cfg.ada_tn, n)
    assert n % tn == 0
    return pl.pallas_call(
        _ada_kernel,
        grid=(depth, n // tn),
        in_specs=[
            pl.BlockSpec((d, 1), lambda l, j: (0, 0)),
            pl.BlockSpec((None, d, tn), lambda l, j: (l, 0, j)),
            pl.BlockSpec((None, 1, tn), lambda l, j: (l, 0, j)),
        ],
        out_specs=pl.BlockSpec((None, 1, tn), lambda l, j: (l, 0, j)),
        out_shape=jax.ShapeDtypeStruct((depth, 1, n), F32),
        compiler_params=_cparams(("arbitrary", "arbitrary"), 2 * d * tn * 4 + d * tn * 4),
        name="ada_modulation",
    )(c_col, ada_w, ada_b.reshape(depth, 1, n))


def _rmsnorm_f32(x, g):
    return x * lax.rsqrt(jnp.mean(x * x, axis=-1, keepdims=True) + RMS_EPS) * g


def _norm_linear_kernel(x_ref, g_ref, sc_ref, sh_ref, w_ref, cs_ref, o_ref, xn_ref, *, modulate):
    @pl.when(pl.program_id(1) == 0)
    def _():
        y = _rmsnorm_f32(x_ref[...].astype(F32), g_ref[...])
        if modulate:
            y = y * (1.0 + sc_ref[...]) + sh_ref[...]
        xn_ref[...] = y.astype(BF16)

    acc = jnp.dot(xn_ref[...], w_ref[...], preferred_element_type=F32)
    o_ref[...] = (acc * cs_ref[...]).astype(o_ref.dtype)


def norm_linear(x, x_col_block, k, g, sc, sh, w_bf, col_scale, out_dtype, cfg, *, modulate, name):
    m = x.shape[0]
    n = w_bf.shape[1]
    tm = min(cfg.lin_tm, m)
    tn = n if n <= cfg.lin_tn or n % cfg.lin_tn else cfg.lin_tn
    assert m % tm == 0 and n % tn == 0
    out_b = jnp.dtype(out_dtype).itemsize
    est = 2 * tm * k * 4 + tm * k * 2 + 2 * k * tn * 2 + 2 * tm * tn * out_b + tm * k * 4
    return pl.pallas_call(
        functools.partial(_norm_linear_kernel, modulate=modulate),
        grid=(m // tm, n // tn),
        in_specs=[
            pl.BlockSpec((tm, k), lambda i, j: (i, x_col_block)),
            pl.BlockSpec((1, k), lambda i, j: (0, 0)),
            pl.BlockSpec((1, k), lambda i, j: (0, 0)),
            pl.BlockSpec((1, k), lambda i, j: (0, 0)),
            pl.BlockSpec((k, tn), lambda i, j: (0, j)),
            pl.BlockSpec((1, tn), lambda i, j: (0, j)),
        ],
        out_specs=pl.BlockSpec((tm, tn), lambda i, j: (i, j)),
        out_shape=jax.ShapeDtypeStruct((m, n), out_dtype),
        scratch_shapes=[pltpu.VMEM((tm, k), BF16)],
        compiler_params=_cparams(("arbitrary", "arbitrary"), est),
        name=name,
    )(x, g, sc, sh, w_bf, col_scale)


def _linear_res_kernel(a_ref, w_ref, h_ref, g_ref, o_ref):
    acc = jnp.dot(a_ref[...], w_ref[...], preferred_element_type=F32)
    o_ref[...] = h_ref[...] + g_ref[...] * acc


def linear_residual(a_bf, w_bf, h, gate, cfg, *, name):
    m, k = a_bf.shape
    n = w_bf.shape[1]
    tm = min(cfg.lin_tm, m)
    tn = min(cfg.lin_tn, n)
    assert m % tm == 0 and n % tn == 0
    est = 2 * tm * k * 2 + 2 * k * tn * 2 + 4 * tm * tn * 4
    return pl.pallas_call(
        _linear_res_kernel,
        grid=(m // tm, n // tn),
        in_specs=[
            pl.BlockSpec((tm, k), lambda i, j: (i, 0)),
            pl.BlockSpec((k, tn), lambda i, j: (0, j)),
            pl.BlockSpec((tm, tn), lambda i, j: (i, j)),
            pl.BlockSpec((1, tn), lambda i, j: (0, j)),
        ],
        out_specs=pl.BlockSpec((tm, tn), lambda i, j: (i, j)),
        out_shape=jax.ShapeDtypeStruct((m, n), F32),
        compiler_params=_cparams(("arbitrary", "arbitrary"), est),
        name=name,
    )(a_bf, w_bf, h, gate)


KIND_SKIP, KIND_PLAIN, KIND_DIAG, KIND_OFFDIAG, KIND_GENERAL = 0, 1, 2, 3, 4


def _block_kinds(pos, t, far_dist, toeplitz):
    s = pos.shape[0]
    nb = s // t
    pb = pos.reshape(nb, t)
    pmin, pmax = pb.min(axis=1), pb.max(axis=1)
    consecutive = jnp.all(pb == pb[:, :1] + jnp.arange(t, dtype=pos.dtype)[None, :], axis=1)
    qi = jnp.arange(nb)[:, None]
    ki = jnp.arange(nb)[None, :]
    gap = pmin[:, None] - pmax[None, :]
    plain = (ki < qi) & (gap >= (far_dist if far_dist else 0))
    kinds = jnp.where(plain, KIND_PLAIN, KIND_GENERAL)
    if toeplitz:
        both = consecutive[:, None] & consecutive[None, :]
        d = pb[:, 0][:, None] - pb[:, 0][None, :]
        kinds = jnp.where(~plain & both & (ki == qi - 1) & (d == t), KIND_OFFDIAG, kinds)
        kinds = jnp.where(both & (ki == qi) & (d == 0), KIND_DIAG, kinds)
    return jnp.where(ki > qi, KIND_SKIP, kinds).astype(I32)


def _attn_pairs(kinds2d, nq):
    qi = np.concatenate([np.full(q + 1, q, np.int32) for q in range(nq)])
    ki = np.concatenate([np.arange(q + 1, dtype=np.int32) for q in range(nq)])
    return jnp.asarray(qi), jnp.asarray(ki), kinds2d[qi, ki].astype(I32)


def _t5_bias_minus_last(rel, rb_ref, h):
    n = jnp.maximum(rel, 0)
    max_exact = REL_BUCKETS // 2
    nf = jnp.maximum(n, 1).astype(F32)
    large = max_exact + (jnp.log(nf / max_exact) / math.log(REL_MAX_DIST / max_exact)
                         * (REL_BUCKETS - max_exact)).astype(I32)
    large = jnp.minimum(large, REL_BUCKETS - 1)
    bucket = jnp.where(n < max_exact, n, large)
    last = rb_ref[REL_BUCKETS - 1, h]
    out = jnp.zeros(rel.shape, F32)
    for b in range(REL_BUCKETS - 1):
        out = jnp.where(bucket == b, rb_ref[b, h] - last, out)
    return out


def _bias_tile_kernel(rb_ref, o_ref, *, t):
    kind = pl.program_id(0)
    h = pl.program_id(1)
    rel = kind * t + lax.broadcasted_iota(I32, (t, t), 0) - lax.broadcasted_iota(I32, (t, t), 1)
    bias = _t5_bias_minus_last(rel, rb_ref, h) * LOG2E
    o_ref[...] = jnp.where(rel >= 0, bias, NEG_INF)


def diff_bias_tiles(rel_bias, heads, t):
    return pl.pallas_call(
        functools.partial(_bias_tile_kernel, t=t),
        grid=(2, heads),
        in_specs=[pl.BlockSpec(memory_space=pltpu.SMEM)],
        out_specs=pl.BlockSpec((None, None, t, t), lambda k, h: (k, h, 0, 0)),
        out_shape=jax.ShapeDtypeStruct((2, heads, t, t), F32),
        compiler_params=_cparams(("arbitrary", "arbitrary"), 8 * t * t * 4),
        name="diff_bias_tiles",
    )(rel_bias)


def _structural_ok(qi, ki, t):
    q_idx = qi * t + lax.broadcasted_iota(I32, (t, t), 0)
    k_idx = ki * t + lax.broadcasted_iota(I32, (t, t), 1)
    return k_idx < ((q_idx // Q_BLOCK) + 1) * Q_BLOCK


def _online_softmax_step(s, v, m_ref, l_ref, acc_ref):
    nl = s.shape[1] // LANE
    m_prev = m_ref[...]
    m_new = jnp.maximum(m_prev, jnp.max(s, axis=-1, keepdims=True))
    alpha = jnp.exp2(m_prev - m_new)
    p = jnp.exp2(s - jnp.tile(m_new, (1, nl)))
    psum = p[:, :LANE]
    for j in range(1, nl):
        psum = psum + p[:, j * LANE:(j + 1) * LANE]
    l_ref[...] = alpha * l_ref[...] + psum
    pv = jnp.dot(p.astype(BF16), v, preferred_element_type=F32)
    acc_ref[...] = jnp.tile(alpha, (1, v.shape[1] // LANE)) * acc_ref[...] + pv
    m_ref[...] = m_new


def _diff_attn_kernel(qi_ref, ki_ref, kind_ref, q_ref, k_ref, v_ref, bt_ref, pq_ref, pk_ref, rb_ref, lam_ref, sg_ref,
                      o_ref, m_ref, l_ref, acc_ref, *, t, hd, hb, out_scale):
    hg, pair = pl.program_id(0), pl.program_id(1)
    qi, ki, kind = qi_ref[pair], ki_ref[pair], kind_ref[pair]
    w = 2 * hd

    @pl.when(ki == 0)
    def _():
        m_ref[...] = jnp.full(m_ref.shape, -jnp.inf, F32)
        l_ref[...] = jnp.zeros(l_ref.shape, F32)
        acc_ref[...] = jnp.zeros(acc_ref.shape, F32)

    def step(adjust):
        for h in range(hb):
            v = v_ref[:, h * w:(h + 1) * w]
            for mp in range(2):
                c0 = h * w + mp * hd
                s = lax.dot_general(q_ref[:, c0:c0 + hd], k_ref[:, c0:c0 + hd], (((1,), (1,)), ((), ())),
                                    preferred_element_type=F32)
                j = 2 * h + mp
                _online_softmax_step(adjust(s, h), v, m_ref.at[j], l_ref.at[j], acc_ref.at[j])

    @pl.when(kind == KIND_PLAIN)
    def _():
        step(lambda s, h: s)

    @pl.when((kind == KIND_DIAG) | (kind == KIND_OFFDIAG))
    def _():
        step(lambda s, h: s + bt_ref[h])

    @pl.when(kind == KIND_GENERAL)
    def _():
        rel = pq_ref[...] - pk_ref[...]
        struct = _structural_ok(qi, ki, t)

        def adjust(s, h):
            add = _t5_bias_minus_last(rel, rb_ref, hg * hb + h) * LOG2E
            s = jnp.where(rel >= 0, s + add, NEG_INF)
            return jnp.where(struct, s, -jnp.inf)
        step(adjust)

    @pl.when(ki == qi)
    def _():
        for h in range(hb):
            l0 = jnp.sum(l_ref[2 * h], axis=-1, keepdims=True)
            l1 = jnp.sum(l_ref[2 * h + 1], axis=-1, keepdims=True)
            o = acc_ref[2 * h] / l0 - lam_ref[0] * (acc_ref[2 * h + 1] / l1)
            o_ref[:, h * w:(h + 1) * w] = (_rmsnorm_f32(o, sg_ref[...]) * out_scale).astype(o_ref.dtype)


def diff_attention(qkv, bias_tiles, kinds2d, pos, rel_bias, lam, sub_g, out_scale, cfg):
    s = qkv.shape[0]
    t = min(cfg.diff_t, s)
    nq = s // t
    heads, hd, hb = cfg.diff_heads, cfg.diff_head_dim, cfg.diff_hb
    assert heads % hb == 0
    ng = heads // hb
    w = 2 * hd
    qi_tab, ki_tab, kind_tab = _attn_pairs(kinds2d, nq)

    grid_spec = pltpu.PrefetchScalarGridSpec(
        num_scalar_prefetch=3,
        grid=(ng, qi_tab.shape[0]),
        in_specs=[
            pl.BlockSpec((t, hb * w), lambda g, p, qt, kt, kd: (qt[p], g)),
            pl.BlockSpec((t, hb * w), lambda g, p, qt, kt, kd: (kt[p], ng + g)),
            pl.BlockSpec((t, hb * w), lambda g, p, qt, kt, kd: (kt[p], 2 * ng + g)),
            pl.BlockSpec((None, hb, t, t), lambda g, p, qt, kt, kd: (jnp.where(kd[p] == KIND_OFFDIAG, 1, 0), g, 0, 0)),
            pl.BlockSpec((t, 1), lambda g, p, qt, kt, kd: (qt[p], 0)),
            pl.BlockSpec((1, t), lambda g, p, qt, kt, kd: (0, kt[p])),
            pl.BlockSpec(memory_space=pltpu.SMEM),
            pl.BlockSpec(memory_space=pltpu.SMEM),
            pl.BlockSpec((1, w), lambda g, p, qt, kt, kd: (0, 0)),
        ],
        out_specs=pl.BlockSpec((t, hb * w), lambda g, p, qt, kt, kd: (qt[p], g)),
        scratch_shapes=[pltpu.VMEM((2 * hb, t, LANE), F32), pltpu.VMEM((2 * hb, t, LANE), F32),
                        pltpu.VMEM((2 * hb, t, w), F32)],
    )
    est = hb * (2 * (4 * t * w * 2 + t * t * 4) + 2 * t * w * 4 + 4 * t * LANE * 4) + 10 * t * t * 4
    return pl.pallas_call(
        functools.partial(_diff_attn_kernel, t=t, hd=hd, hb=hb, out_scale=out_scale),
        grid_spec=grid_spec,
        out_shape=jax.ShapeDtypeStruct((s, heads * w), BF16),
        compiler_params=_cparams(("arbitrary", "arbitrary"), est),
        name="diff_attention",
    )(qi_tab, ki_tab, kind_tab, qkv, qkv, qkv, bias_tiles, pos.reshape(s, 1), pos.reshape(1, s), rel_bias, lam, sub_g)


def _rope_slot(x, c, s1, s2):
    return x * c + pltpu.roll(x, LANE - 32, 1) * s1 + pltpu.roll(x, 32, 1) * s2


def _rope_tables(pos, rope_dim):
    half = rope_dim // 2
    inv = ROPE_THETA ** (-jnp.arange(half, dtype=F32) / half)
    ang = pos.astype(F32)[:, None] * inv
    cos, sin = jnp.cos(ang), jnp.sin(ang)
    z = jnp.zeros_like(cos)
    pad = jnp.zeros((pos.shape[0], LANE - 2 * half), F32)
    c = jnp.concatenate([cos, cos, pad], axis=1)
    s1 = jnp.concatenate([-sin, z, pad], axis=1)
    s2 = jnp.concatenate([z, sin, pad], axis=1)
    return c, s1, s2


def _rope_rows_kernel(x_ref, c_ref, s1_ref, s2_ref, o_ref):
    o_ref[...] = _rope_slot(x_ref[...].astype(F32), c_ref[...], s1_ref[...], s2_ref[...]).astype(o_ref.dtype)


def rope_rows(x, col_block, tables, cfg):
    m = x.shape[0]
    tm = min(cfg.lin_tm, m)
    row = pl.BlockSpec((tm, LANE), lambda i: (i, 0))
    return pl.pallas_call(
        _rope_rows_kernel,
        grid=(m // tm,),
        in_specs=[pl.BlockSpec((tm, LANE), lambda i: (i, col_block)), row, row, row],
        out_specs=row,
        out_shape=jax.ShapeDtypeStruct((m, LANE), BF16),
        compiler_params=_cparams(("arbitrary",), 10 * tm * LANE * 4),
        name="mla_rope_key",
    )(x, *tables)


def _mla_attn_kernel(qi_ref, ki_ref, kind_ref, q_ref, kv_ref, kr_ref, c_ref, s1_ref, s2_ref, pq_ref, pk_ref, o_ref,
                     qs_ref, kc_ref, m_ref, l_ref, acc_ref, *, t, hb):
    pair = pl.program_id(1)
    qi, ki, kind = qi_ref[pair], ki_ref[pair], kind_ref[pair]
    w = 2 * LANE

    @pl.when(ki == 0)
    def _():
        m_ref[...] = jnp.full(m_ref.shape, -jnp.inf, F32)
        l_ref[...] = jnp.zeros(l_ref.shape, F32)
        acc_ref[...] = jnp.zeros(acc_ref.shape, F32)
        for h in range(hb):
            qs_ref[h, :, :LANE] = q_ref[:, h * w:h * w + LANE]
            qr = _rope_slot(q_ref[:, h * w + LANE:(h + 1) * w].astype(F32), c_ref[...], s1_ref[...], s2_ref[...])
            qs_ref[h, :, LANE:] = qr.astype(BF16)

    def step(adjust):
        for h in range(hb):
            kc_ref[h, :, :LANE] = kv_ref[:, h * w:h * w + LANE]
            kc_ref[h, :, LANE:] = kr_ref[...]
            s = lax.dot_general(qs_ref[h], kc_ref[h], (((1,), (1,)), ((), ())), preferred_element_type=F32)
            _online_softmax_step(adjust(s), kv_ref[:, h * w + LANE:(h + 1) * w], m_ref.at[h], l_ref.at[h], acc_ref.at[h])

    @pl.when(kind == KIND_PLAIN)
    def _():
        step(lambda s: s)

    @pl.when(kind == KIND_GENERAL)
    def _():
        rel = pq_ref[...] - pk_ref[...]
        struct = _structural_ok(qi, ki, t)

        def adjust(s):
            s = jnp.where(rel >= 0, s, NEG_INF)
            return jnp.where(struct, s, -jnp.inf)
        step(adjust)

    @pl.when(ki == qi)
    def _():
        for h in range(hb):
            l = jnp.sum(l_ref[h], axis=-1, keepdims=True)
            o_ref[:, h * LANE:(h + 1) * LANE] = (acc_ref[h] / l).astype(o_ref.dtype)


def mla_attention(q_cat, kv, kr_rot, tables, kinds2d, pos, cfg):
    s = q_cat.shape[0]
    t = min(cfg.mla_t, s)
    nq = s // t
    heads, hb = cfg.mla_heads, cfg.mla_hb
    assert heads % hb == 0
    w = 2 * LANE
    qi_tab, ki_tab, kind_tab = _attn_pairs(kinds2d, nq)
    qrow = pl.BlockSpec((t, LANE), lambda g, p, qt, kt, kd: (qt[p], 0))
    grid_spec = pltpu.PrefetchScalarGridSpec(
        num_scalar_prefetch=3,
        grid=(heads // hb, qi_tab.shape[0]),
        in_specs=[
            pl.BlockSpec((t, hb * w), lambda g, p, qt, kt, kd: (qt[p], g)),
            pl.BlockSpec((t, hb * w), lambda g, p, qt, kt, kd: (kt[p], g)),
            pl.BlockSpec((t, LANE), lambda g, p, qt, kt, kd: (kt[p], 0)),
            qrow, qrow, qrow,
            pl.BlockSpec((t, 1), lambda g, p, qt, kt, kd: (qt[p], 0)),
            pl.BlockSpec((1, t), lambda g, p, qt, kt, kd: (0, kt[p])),
        ],
        out_specs=pl.BlockSpec((t, hb * LANE), lambda g, p, qt, kt, kd: (qt[p], g)),
        scratch_shapes=[pltpu.VMEM((hb, t, w), BF16), pltpu.VMEM((hb, t, w), BF16),
                        pltpu.VMEM((hb, t, LANE), F32), pltpu.VMEM((hb, t, LANE), F32), pltpu.VMEM((hb, t, LANE), F32)],
    )
    est = hb * (2 * (2 * t * w * 2 + t * LANE * 2) + 2 * t * w * 2 + 3 * t * LANE * 4) + 8 * t * LANE * 4 + 10 * t * t * 4
    return pl.pallas_call(
        functools.partial(_mla_attn_kernel, t=t, hb=hb),
        grid_spec=grid_spec,
        out_shape=jax.ShapeDtypeStruct((s, heads * cfg.mla_v), BF16),
        compiler_params=_cparams(("arbitrary", "arbitrary"), est),
        name="mla_attention",
    )(qi_tab, ki_tab, kind_tab, q_cat, kv, kr_rot, *tables, pos.reshape(s, 1), pos.reshape(1, s))


def _router_kernel(h_ref, g_ref, sc_ref, sh_ref, wr_ref, br_ref, u_ref, idx_ref, gate_ref, *, top_k):
    u = _rmsnorm_f32(h_ref[...], g_ref[...]) * (1.0 + sc_ref[...]) + sh_ref[...]
    u_ref[...] = u
    w = wr_ref[...]
    u_hi, w_hi = u.astype(BF16), w.astype(BF16)
    u_lo = (u - u_hi.astype(F32)).astype(BF16)
    w_lo = (w - w_hi.astype(F32)).astype(BF16)
    logits = (jnp.dot(u_hi, w_hi, preferred_element_type=F32) + jnp.dot(u_lo, w_hi, preferred_element_type=F32)
              + jnp.dot(u_hi, w_lo, preferred_element_type=F32)) + br_ref[...]
    n_e = logits.shape[-1]
    lane = lax.broadcasted_iota(I32, logits.shape, 1)
    vals, idxs = [], []
    cur = logits
    for _ in range(top_k):
        mx = jnp.max(cur, axis=-1, keepdims=True)
        ix = jnp.min(jnp.where(cur == mx, lane, n_e), axis=-1, keepdims=True)
        vals.append(mx)
        idxs.append(ix)
        cur = jnp.where(lane == ix, -jnp.inf, cur)
    v = jnp.concatenate(vals, axis=1)
    e = jnp.exp(v - vals[0])
    gate_ref[...] = e / jnp.sum(e, axis=-1, keepdims=True)
    idx_ref[...] = jnp.concatenate(idxs, axis=1)


def router(h, g, sc, sh, w_router, b_router, cfg):
    t, d = h.shape
    e = w_router.shape[1]
    tm = min(cfg.router_tm, t)
    vec = pl.BlockSpec((1, d), lambda i: (0, 0))
    return pl.pallas_call(
        functools.partial(_router_kernel, top_k=cfg.top_k),
        grid=(t // tm,),
        in_specs=[pl.BlockSpec((tm, d), lambda i: (i, 0)), vec, vec, vec,
                  pl.BlockSpec((d, e), lambda i: (0, 0)), pl.BlockSpec((1, e), lambda i: (0, 0))],
        out_specs=[pl.BlockSpec((tm, d), lambda i: (i, 0)),
                   pl.BlockSpec((tm, cfg.top_k), lambda i: (i, 0)),
                   pl.BlockSpec((tm, cfg.top_k), lambda i: (i, 0))],
        out_shape=[jax.ShapeDtypeStruct((t, d), F32),
                   jax.ShapeDtypeStruct((t, cfg.top_k), I32),
                   jax.ShapeDtypeStruct((t, cfg.top_k), F32)],
        compiler_params=_cparams(("arbitrary",), 6 * tm * d * 4 + 2 * d * LANE * 4),
        name="moe_router",
    )(h, g, sc, sh, w_router, b_router)


def _routing_tables(idx, cfg):
    t, k = idx.shape
    a = t * k
    e = cfg.n_experts
    r = cfg.moe_rows
    p_rows = a + e * MOE_BLOCK
    g_max = e + a // r + 1
    e_flat = idx.reshape(a)
    onehot = (e_flat[:, None] == jnp.arange(e, dtype=I32)[None, :]).astype(I32)
    csum = jnp.cumsum(onehot, axis=0)
    counts = csum[-1]
    rank = jnp.sum(onehot * csum, axis=1) - 1
    padded = ((counts + MOE_BLOCK - 1) // MOE_BLOCK) * MOE_BLOCK
    pad_start = jnp.cumsum(padded) - padded
    pos = (pad_start[e_flat] + rank).astype(I32)
    tok_sorted = jnp.zeros((p_rows + 2 * r,), I32).at[pos].set(jnp.arange(a, dtype=I32) // k)
    n_grp = (padded + r - 1) // r
    cum = jnp.cumsum(n_grp)
    n_groups = cum[-1]
    gid = jnp.arange(g_max, dtype=I32)
    last = jnp.maximum(n_groups - 1, 0)
    gid_c = jnp.minimum(gid, last)
    g_exp = jnp.minimum(jnp.searchsorted(cum, gid_c, side="right"), e - 1).astype(I32)
    local = gid_c - (cum - n_grp)[g_exp]
    g_row = (pad_start[g_exp] + local * r).astype(I32)
    g_n = jnp.clip(padded[g_exp] - local * r, 0, r).astype(I32)
    g_n = jnp.where(gid < n_groups, g_n, 0)
    return tok_sorted, pos, g_exp, g_row, g_n, n_groups.astype(I32), p_rows


MOE_ARM_BLOCKS = (8, 4, 2, 1)
MOE_GATHER_SHARE = 4


def _moe_kernel(tok_ref, gexp_ref, grow_ref, gn_ref,
                u_hbm, wg_ref, wu_ref, bg_ref, bu_ref, wd_ref, bd_ref, ys_hbm,
                xg_ref, xb_ref, yst_ref, cnt_ref, gsem, osem):
    del gexp_ref
    g, c = pl.program_id(0), pl.program_id(1)
    ng, nc = pl.num_programs(0), pl.num_programs(1)
    n = gn_ref[g]
    row0 = grow_ref[g]
    has_next = g + 1 < ng
    nxt = jnp.minimum(g + 1, gn_ref.shape[0] - 1)
    n_next = jnp.where(has_next, gn_ref[nxt], 0)
    base_next = grow_ref[nxt]

    def gather_row(base, idx):
        tok = tok_ref[base + idx]
        pltpu.make_async_copy(u_hbm.at[pl.ds(tok, 1)], xg_ref.at[pl.ds(idx, 1)], gsem).start()

    def wait_rows(ref, rows, sem):
        rows = pl.multiple_of(rows, MOE_BLOCK)
        pltpu.make_async_copy(ref.at[pl.ds(0, rows)], ref.at[pl.ds(0, rows)], sem).wait()

    @pl.when(c == 0)
    def _():
        @pl.when(g == 0)
        def _():
            def first(i, carry):
                gather_row(row0, i)
                return carry
            lax.fori_loop(0, n, first, 0)
        wait_rows(xg_ref, n, gsem)

        @pl.when(g > 0)
        def _():
            wait_rows(yst_ref, gn_ref[jnp.maximum(g - 1, 0)], osem)

        def cast(i, carry):
            r0 = pl.multiple_of(i * MOE_BLOCK, MOE_BLOCK)
            xb_ref[pl.ds(r0, MOE_BLOCK), :] = xg_ref[pl.ds(r0, MOE_BLOCK), :].astype(BF16)
            yst_ref[pl.ds(r0, MOE_BLOCK), :] = jnp.broadcast_to(bd_ref[...], (MOE_BLOCK, bd_ref.shape[-1]))
            return carry
        lax.fori_loop(0, n // MOE_BLOCK, cast, 0)
        cnt_ref[0] = 0

    def arm(r0, m, n_gather):
        x = xb_ref[pl.ds(r0, m), :]
        hg = jnp.dot(x, wg_ref[...].astype(BF16), preferred_element_type=F32)
        hu = jnp.dot(x, wu_ref[...].astype(BF16), preferred_element_type=F32)
        gate = jnp.minimum(hg + bg_ref[...], SWIGLU_LIMIT)
        up = jnp.clip(hu + bu_ref[...], -SWIGLU_LIMIT, SWIGLU_LIMIT)
        act = ((up + 1.0) * (gate * jax.nn.sigmoid(gate * SWIGLU_ALPHA))).astype(BF16)
        yst_ref[pl.ds(r0, m), :] += jnp.dot(act, wd_ref[...].astype(BF16), preferred_element_type=F32)
        cnt = cnt_ref[0]
        for i in range(n_gather):
            @pl.when(cnt + i < n_next)
            def _():
                gather_row(base_next, cnt + i)
        cnt_ref[0] = cnt + n_gather

        @pl.when(c == nc - 1)
        def _():
            dst0 = pl.multiple_of(row0 + r0, MOE_BLOCK)
            pltpu.make_async_copy(yst_ref.at[pl.ds(r0, m)], ys_hbm.at[pl.ds(dst0, m)], osem).start()

    nb = n // MOE_BLOCK
    top = MOE_ARM_BLOCKS[0]
    top_rows = top * MOE_BLOCK
    rows_cap = xg_ref.shape[0]

    @pl.when(nb >= top)
    def _():
        arm(0, top_rows, -(-rows_cap // nc))
        for blocks in range(1, rows_cap // MOE_BLOCK - top + 1):
            @pl.when(nb - top == blocks)
            def _():
                arm(top_rows, blocks * MOE_BLOCK, 0)

    @pl.when(nb < top)
    def _():
        for blocks in MOE_ARM_BLOCKS[1:]:
            @pl.when((nb & blocks) != 0)
            def _():
                done = (nb // (2 * blocks)) * (2 * blocks)
                arm(pl.multiple_of(done * MOE_BLOCK, blocks * MOE_BLOCK), blocks * MOE_BLOCK,
                    blocks * MOE_BLOCK // MOE_GATHER_SHARE)

    @pl.when(c == nc - 1)
    def _():
        def rest(i, carry):
            gather_row(base_next, i)
            return carry
        lax.fori_loop(jnp.minimum(cnt_ref[0], n_next), n_next, rest, 0)

        @pl.when(g == ng - 1)
        def _():
            wait_rows(yst_ref, n, osem)
            yst_ref[pl.ds(0, MOE_BLOCK), :] = jnp.zeros((MOE_BLOCK, yst_ref.shape[-1]), F32)
            first = (row0 + n) // MOE_BLOCK
            n_fill = ys_hbm.shape[0] // MOE_BLOCK - first

            def fill(i, carry):
                dst0 = pl.multiple_of((first + i) * MOE_BLOCK, MOE_BLOCK)
                pltpu.make_async_copy(yst_ref.at[pl.ds(0, MOE_BLOCK)], ys_hbm.at[pl.ds(dst0, MOE_BLOCK)], osem).start()
                return carry
            lax.fori_loop(0, n_fill, fill, 0)

            def drain(i, carry):
                pltpu.make_async_copy(yst_ref.at[pl.ds(0, MOE_BLOCK)], yst_ref.at[pl.ds(0, MOE_BLOCK)], osem).wait()
                return carry
            lax.fori_loop(0, n_fill, drain, 0)


def moe_experts(u, tok_sorted, g_exp, g_row, g_n, n_groups, layer, w_gu, b_gu, w_down, b_down, p_rows, cfg):
    t, d = u.shape
    n_layers, e, _, f2 = w_gu.shape
    f = f2 // 2
    tf = min(cfg.moe_tf, f)
    nc = f // tf
    rows = cfg.moe_rows
    assert rows % MOE_BLOCK == 0

    grid_spec = pltpu.PrefetchScalarGridSpec(
        num_scalar_prefetch=4,
        grid=(n_groups, nc),
        in_specs=[
            pl.BlockSpec(memory_space=pl.ANY),
            pl.BlockSpec((None, None, d, tf), lambda g, c, tk, ge, gr, gn: (layer, ge[g], 0, c)),
            pl.BlockSpec((None, None, d, tf), lambda g, c, tk, ge, gr, gn: (layer, ge[g], 0, nc + c)),
            pl.BlockSpec((None, None, 1, tf), lambda g, c, tk, ge, gr, gn: (layer, ge[g], 0, c)),
            pl.BlockSpec((None, None, 1, tf), lambda g, c, tk, ge, gr, gn: (layer, ge[g], 0, nc + c)),
            pl.BlockSpec((None, None, tf, d), lambda g, c, tk, ge, gr, gn: (layer, ge[g], c, 0)),
            pl.BlockSpec((None, None, 1, d), lambda g, c, tk, ge, gr, gn: (layer, ge[g], 0, 0)),
        ],
        out_specs=pl.BlockSpec(memory_space=pl.ANY),
        scratch_shapes=[
            pltpu.VMEM((rows, d), F32), pltpu.VMEM((rows, d), BF16), pltpu.VMEM((rows, d), F32),
            pltpu.SMEM((1,), I32), pltpu.SemaphoreType.DMA(()), pltpu.SemaphoreType.DMA(()),
        ],
    )
    est = rows * d * 10 + 2 * 3 * d * tf * 4 + 3 * d * tf * 2
    return pl.pallas_call(
        _moe_kernel,
        grid_spec=grid_spec,
        out_shape=jax.ShapeDtypeStruct((p_rows, d), F32),
        compiler_params=_cparams(("arbitrary", "arbitrary"), est),
        name="moe_experts",
    )(tok_sorted, g_exp, g_row, g_n,
      u, w_gu, w_gu, b_gu.reshape(n_layers, e, 1, f2), b_gu.reshape(n_layers, e, 1, f2), w_down,
      b_down.reshape(n_layers, e, 1, d))


def _combine_kernel(pos_ref, ys_hbm, h_ref, gate_ref, g2_ref, fg_ref, o_ref, rows_ref, sem, *, tm, top_k, final_norm):
    i = pl.program_id(0)
    n_tiles = pl.num_programs(0)
    n_rows = tm * top_k

    def start_tile(tile, slot, cond):
        base = tile * n_rows
        for j in range(n_rows):
            def issue(j=j):
                src = pos_ref[base + j]
                pltpu.make_async_copy(ys_hbm.at[pl.ds(src, 1)], rows_ref.at[slot, pl.ds(j, 1)], sem.at[slot]).start()
            if cond is None:
                issue()
            else:
                pl.when(cond)(issue)

    def finish(slot):
        pltpu.make_async_copy(rows_ref.at[slot], rows_ref.at[slot], sem.at[slot]).wait()
        gates = gate_ref[...]
        acc = rows_ref[slot, pl.ds(0, tm), :] * gates[:, 0:1]
        for k in range(1, top_k):
            acc = acc + rows_ref[slot, pl.ds(k * tm, tm), :] * gates[:, k:k + 1]
        out = h_ref[...] + g2_ref[...] * acc
        if final_norm:
            out = _rmsnorm_f32(out, fg_ref[...])
        o_ref[...] = out

    @pl.when(i == 0)
    def _():
        start_tile(0, 0, None)

    for parity in range(2):
        @pl.when(i % 2 == parity)
        def _():
            start_tile(i + 1, 1 - parity, i + 1 < n_tiles)
            finish(parity)


def moe_combine(ys, pos, h, gates, g2, final_g, cfg, *, final_norm):
    t, d = h.shape
    top_k = cfg.top_k
    tm = min(cfg.comb_tm, t)
    assert (tm * top_k) % 64 == 0
    pos = pos.reshape(t // tm, tm, top_k).transpose(0, 2, 1).reshape(-1)
    vec = pl.BlockSpec((1, d), lambda i, p: (0, 0))
    grid_spec = pltpu.PrefetchScalarGridSpec(
        num_scalar_prefetch=1,
        grid=(t // tm,),
        in_specs=[pl.BlockSpec(memory_space=pl.ANY),
                  pl.BlockSpec((tm, d), lambda i, p: (i, 0)),
                  pl.BlockSpec((tm, top_k), lambda i, p: (i, 0)), vec, vec],
        out_specs=pl.BlockSpec((tm, d), lambda i, p: (i, 0)),
        scratch_shapes=[pltpu.VMEM((2, tm * top_k, d), F32), pltpu.SemaphoreType.DMA((2,))],
    )
    return pl.pallas_call(
        functools.partial(_combine_kernel, tm=tm, top_k=top_k, final_norm=final_norm),
        grid_spec=grid_spec,
        out_shape=jax.ShapeDtypeStruct((t, d), F32),
        compiler_params=_cparams(("arbitrary",), 2 * tm * top_k * d * 4 + 6 * tm * d * 4),
        name="moe_combine",
    )(pos, ys, h, gates, g2, final_g)


def moe_layer(h, g, sc, sh, g2, w_router, b_router, layer, w_gu, b_gu, w_down, b_down, final_g, cfg, *, final_norm):
    u, idx, gates = router(h, g, sc, sh, w_router, b_router.reshape(1, -1), cfg)
    tok_sorted, pos, g_exp, g_row, g_n, n_groups, p_rows = _routing_tables(idx, cfg)
    ys = moe_experts(u, tok_sorted, g_exp, g_row, g_n, n_groups, layer, w_gu, b_gu, w_down, b_down, p_rows, cfg)
    return moe_combine(ys, pos, h, gates, g2, final_g, cfg, final_norm=final_norm)


def _diff_mixer(h, pos, g, sc, sh, gate, w_qkv, lq1, lk1, lq2, lk2, sub_g, w_o, rel_bias, layer_idx, cfg):
    heads, hd = cfg.diff_heads, cfg.diff_head_dim
    qk_w = heads * 2 * hd
    lam_init = 0.8 - 0.6 * math.exp(-0.3 * (layer_idx - 1))
    lam = (jnp.exp(jnp.sum(lq1 * lk1)) - jnp.exp(jnp.sum(lq2 * lk2)) + lam_init).reshape(1).astype(F32)
    n = w_qkv.shape[1]
    col_scale = jnp.concatenate([jnp.full((qk_w,), hd ** -0.5 * LOG2E, F32), jnp.ones((n - qk_w,), F32)]).reshape(1, n)
    qkv = norm_linear(h, 0, h.shape[1], g, sc, sh, w_qkv.astype(BF16), col_scale, BF16, cfg,
                      modulate=True, name="diff_qkv_proj")
    t = min(cfg.diff_t, h.shape[0])
    tiles = diff_bias_tiles(rel_bias, heads, t)
    kinds = _block_kinds(pos, t, REL_FAR, toeplitz=True)
    o = diff_attention(qkv, tiles, kinds, pos, rel_bias, lam, sub_g.reshape(1, -1), 1.0 - lam_init, cfg)
    return linear_residual(o, w_o.astype(BF16), h, gate, cfg, name="diff_out_proj")


def _mla_mixer(h, pos, g, sc, sh, gate, w_in, q_norm_g, kv_norm_g, w_uq, w_ukv, w_o, cfg):
    d = h.shape[1]
    heads, nope, rope, qr, kvr = cfg.mla_heads, cfg.mla_nope, cfg.mla_rope, cfg.mla_q_rank, cfg.mla_kv_rank
    assert nope == LANE and cfg.mla_v == LANE and rope <= LANE and qr % LANE == 0 and kvr == qr
    w_in_p = jnp.concatenate([w_in, jnp.zeros((d, LANE - rope), w_in.dtype)], axis=1).astype(BF16)
    ones = lambda n: jnp.ones((1, n), F32)
    z = norm_linear(h, 0, d, g, sc, sh, w_in_p, ones(w_in_p.shape[1]), F32, cfg, modulate=True, name="mla_down_proj")
    w_q = w_uq.reshape(qr, heads, nope + rope)
    w_q = jnp.concatenate([w_q, jnp.zeros((qr, heads, 2 * LANE - nope - rope), w_uq.dtype)], axis=2)
    w_q = w_q.reshape(qr, heads * 2 * LANE).astype(BF16)
    zeros_k = jnp.zeros((1, qr), F32)
    q_scale = jnp.full((1, heads * 2 * LANE), (nope + rope) ** -0.5 * LOG2E, F32)
    q_cat = norm_linear(z, 0, qr, q_norm_g.reshape(1, -1), zeros_k, zeros_k, w_q, q_scale, BF16, cfg,
                        modulate=False, name="mla_q_up_proj")
    kv = norm_linear(z, 1, kvr, kv_norm_g.reshape(1, -1), zeros_k, zeros_k, w_ukv.astype(BF16),
                     ones(w_ukv.shape[1]), BF16, cfg, modulate=False, name="mla_kv_up_proj")
    tables = _rope_tables(pos, rope)
    kr_rot = rope_rows(z, (qr + kvr) // LANE, tables, cfg)
    t = min(cfg.mla_t, h.shape[0])
    kinds = _block_kinds(pos, t, 0, toeplitz=False)
    o = mla_attention(q_cat, kv, kr_rot, tables, kinds, pos, cfg)
    return linear_residual(o, w_o.astype(BF16), h, gate, cfg, name="mla_out_proj")


def _forward(cfg, x, c, positions, ada_w, ada_b, norm1_g, norm2_g, final_g, rel_bias,
             diff_w_qkv, diff_lq1, diff_lk1, diff_lq2, diff_lk2, diff_sub_g, diff_w_o,
             mla_w_in, mla_q_norm_g, mla_kv_norm_g, mla_w_uq, mla_w_ukv, mla_w_o,
             router_w, router_b, exp_w_gu, exp_b_gu, exp_w_down, exp_b_down):
    b, s, d = x.shape
    assert b == 1, "kernels are written for a single sequence"
    h = x.reshape(s, d)
    pos = positions.reshape(s).astype(I32)
    mod = ada_modulation(c.reshape(d, 1), ada_w, ada_b, cfg)
    fg = final_g.reshape(1, d)
    for i in range(cfg.depth):
        sh1, sc1, g1, sh2, sc2, g2 = [mod[i, :, j * d:(j + 1) * d] for j in range(6)]
        n1 = norm1_g[i].reshape(1, d)
        j = i // 2
        if i % 2 == 0:
            h = _diff_mixer(h, pos, n1, sc1, sh1, g1, diff_w_qkv[j], diff_lq1[j], diff_lk1[j], diff_lq2[j],
                            diff_lk2[j], diff_sub_g[j], diff_w_o[j], rel_bias, i + 1, cfg)
        else:
            h = _mla_mixer(h, pos, n1, sc1, sh1, g1, mla_w_in[j], mla_q_norm_g[j], mla_kv_norm_g[j],
                           mla_w_uq[j], mla_w_ukv[j], mla_w_o[j], cfg)
        h = moe_layer(h, norm2_g[i].reshape(1, d), sc2, sh2, g2, router_w[i], router_b[i], i, exp_w_gu, exp_b_gu,
                      exp_w_down, exp_b_down, fg, cfg, final_norm=(i == cfg.depth - 1))
    return h.reshape(b, s, d)


def kernel(x, c, positions, ada_w, ada_b, norm1_g, norm2_g, final_g, rel_bias, diff_w_qkv, diff_lq1, diff_lk1, diff_lq2, diff_lk2, diff_sub_g, diff_w_o, mla_w_in, mla_q_norm_g, mla_kv_norm_g, mla_w_uq, mla_w_ukv, mla_w_o, router_w, router_b, exp_w_gu, exp_b_gu, exp_w_down, exp_b_down):
    return _forward(Cfg(), x, c, positions, ada_w, ada_b, norm1_g, norm2_g, final_g, rel_bias,
                    diff_w_qkv, diff_lq1, diff_lk1, diff_lq2, diff_lk2, diff_sub_g, diff_w_o,
                    mla_w_in, mla_q_norm_g, mla_kv_norm_g, mla_w_uq, mla_w_ukv, mla_w_o,
                    router_w, router_b, exp_w_gu, exp_b_gu, exp_w_down, exp_b_down)
```

```python
import dataclasses
import functools
import math

import jax
import jax.numpy as jnp
import numpy as np
from jax import lax
from jax.experimental import pallas as pl
from jax.experimental.pallas import tpu as pltpu

F32 = jnp.float32
BF16 = jnp.bfloat16
I32 = jnp.int32

RMS_EPS = 1e-6
NEG_INF = -1e30
LOG2E = math.log2(math.e)
ROPE_THETA = 10000.0
REL_BUCKETS = 32
REL_MAX_DIST = 128
REL_FAR = 113
SWIGLU_LIMIT = 7.0
SWIGLU_ALPHA = 1.702
Q_BLOCK = 128
MOE_BLOCK = 128
LANE = 128
V7X_VMEM_BYTES = 64 * 1024 * 1024


@dataclasses.dataclass(frozen=True)
class Cfg:
    d_model: int = 2048
    seq: int = 8192
    depth: int = 2
    diff_heads: int = 8
    diff_head_dim: int = 128
    mla_heads: int = 16
    mla_q_rank: int = 512
    mla_kv_rank: int = 512
    mla_nope: int = 128
    mla_rope: int = 64
    mla_v: int = 128
    n_experts: int = 32
    top_k: int = 4
    d_ff: int = 2048
    ada_tn: int = 1024
    lin_tm: int = 1024
    lin_tn: int = 1024
    diff_t: int = 512
    diff_hb: int = 4
    mla_t: int = 1024
    mla_hb: int = 4
    router_tm: int = 256
    moe_rows: int = 1280
    moe_tf: int = 256
    comb_tm: int = 128


def _vmem_limit(nbytes):
    return int(min(nbytes * 1.25 + (6 << 20), V7X_VMEM_BYTES - (6 << 20)))


def _cparams(sem, nbytes):
    return pltpu.CompilerParams(dimension_semantics=sem, vmem_limit_bytes=_vmem_limit(nbytes))


def _ada_kernel(c_ref, w_ref, b_ref, o_ref):
    c = c_ref[...]
    cs = c * jax.nn.sigmoid(c)
    o_ref[...] = jnp.sum(w_ref[...] * cs, axis=0, keepdims=True) + b_ref[...]


def ada_modulation(c_col, ada_w, ada_b, cfg):
    depth, d, n = ada_w.shape
    tn = min(cfg.ada_tn, n)
    assert n % tn == 0
    return pl.pallas_call(
        _ada_kernel,
        grid=(depth, n // tn),
        in_specs=[
            pl.BlockSpec((d, 1), lambda l, j: (0, 0)),
            pl.BlockSpec((None, d, tn), lambda l, j: (l, 0, j)),
            pl.BlockSpec((None, 1, tn), lambda l, j: (l, 0, j)),
        ],
        out_specs=pl.BlockSpec((None, 1, tn), lambda l, j: (l, 0, j)),
        out_shape=jax.ShapeDtypeStruct((depth, 1, n), F32),
        compiler_params=_cparams(("arbitrary", "arbitrary"), 2 * d * tn * 4 + d * tn * 4),
        name="ada_modulation",
    )(c_col, ada_w, ada_b.reshape(depth, 1, n))


def _rmsnorm_f32(x, g):
    return x * lax.rsqrt(jnp.mean(x * x, axis=-1, keepdims=True) + RMS_EPS) * g


def _norm_linear_kernel(x_ref, g_ref, sc_ref, sh_ref, w_ref, cs_ref, o_ref, xn_ref, *, modulate):
    @pl.when(pl.program_id(1) == 0)
    def _():
        y = _rmsnorm_f32(x_ref[...].astype(F32), g_ref[...])
        if modulate:
            y = y * (1.0 + sc_ref[...]) + sh_ref[...]
        xn_ref[...] = y.astype(BF16)

    acc = jnp.dot(xn_ref[...], w_ref[...], preferred_element_type=F32)
    o_ref[...] = (acc * cs_ref[...]).astype(o_ref.dtype)


def norm_linear(x, x_col_block, k, g, sc, sh, w_bf, col_scale, out_dtype, cfg, *, modulate, name):
    m = x.shape[0]
    n = w_bf.shape[1]
    tm = min(cfg.lin_tm, m)
    tn = n if n <= cfg.lin_tn or n % cfg.lin_tn else cfg.lin_tn
    assert m % tm == 0 and n % tn == 0
    out_b = jnp.dtype(out_dtype).itemsize
    est = 2 * tm * k * 4 + tm * k * 2 + 2 * k * tn * 2 + 2 * tm * tn * out_b + tm * k * 4
    return pl.pallas_call(
        functools.partial(_norm_linear_kernel, modulate=modulate),
        grid=(m // tm, n // tn),
        in_specs=[
            pl.BlockSpec((tm, k), lambda i, j: (i, x_col_block)),
            pl.BlockSpec((1, k), lambda i, j: (0, 0)),
            pl.BlockSpec((1, k), lambda i, j: (0, 0)),
            pl.BlockSpec((1, k), lambda i, j: (0, 0)),
            pl.BlockSpec((k, tn), lambda i, j: (0, j)),
            pl.BlockSpec((1, tn), lambda i, j: (0, j)),
        ],
        out_specs=pl.BlockSpec((tm, tn), lambda i, j: (i, j)),
        out_shape=jax.ShapeDtypeStruct((m, n), out_dtype),
        scratch_shapes=[pltpu.VMEM((tm, k), BF16)],
        compiler_params=_cparams(("arbitrary", "arbitrary"), est),
        name=name,
    )(x, g, sc, sh, w_bf, col_scale)


def _linear_res_kernel(a_ref, w_ref, h_ref, g_ref, o_ref):
    acc = jnp.dot(a_ref[...], w_ref[...], preferred_element_type=F32)
    o_ref[...] = h_ref[...] + g_ref[...] * acc


def linear_residual(a_bf, w_bf, h, gate, cfg, *, name):
    m, k = a_bf.shape
    n = w_bf.shape[1]
    tm = min(cfg.lin_tm, m)
    tn = min(cfg.lin_tn, n)
    assert m % tm == 0 and n % tn == 0
    est = 2 * tm * k * 2 + 2 * k * tn * 2 + 4 * tm * tn * 4
    return pl.pallas_call(
        _linear_res_kernel,
        grid=(m // tm, n // tn),
        in_specs=[
            pl.BlockSpec((tm, k), lambda i, j: (i, 0)),
            pl.BlockSpec((k, tn), lambda i, j: (0, j)),
            pl.BlockSpec((tm, tn), lambda i, j: (i, j)),
            pl.BlockSpec((1, tn), lambda i, j: (0, j)),
        ],
        out_specs=pl.BlockSpec((tm, tn), lambda i, j: (i, j)),
        out_shape=jax.ShapeDtypeStruct((m, n), F32),
        compiler_params=_cparams(("arbitrary", "arbitrary"), est),
        name=name,
    )(a_bf, w_bf, h, gate)


KIND_SKIP, KIND_PLAIN, KIND_DIAG, KIND_OFFDIAG, KIND_GENERAL = 0, 1, 2, 3, 4


def _block_kinds(pos, t, far_dist, toeplitz):
    s = pos.shape[0]
    nb = s // t
    pb = pos.reshape(nb, t)
    pmin, pmax = pb.min(axis=1), pb.max(axis=1)
    consecutive = jnp.all(pb == pb[:, :1] + jnp.arange(t, dtype=pos.dtype)[None, :], axis=1)
    qi = jnp.arange(nb)[:, None]
    ki = jnp.arange(nb)[None, :]
    gap = pmin[:, None] - pmax[None, :]
    plain = (ki < qi) & (gap >= (far_dist if far_dist else 0))
    kinds = jnp.where(plain, KIND_PLAIN, KIND_GENERAL)
    if toeplitz:
        both = consecutive[:, None] & consecutive[None, :]
        d = pb[:, 0][:, None] - pb[:, 0][None, :]
        kinds = jnp.where(~plain & both & (ki == qi - 1) & (d == t), KIND_OFFDIAG, kinds)
        kinds = jnp.where(both & (ki == qi) & (d == 0), KIND_DIAG, kinds)
    return jnp.where(ki > qi, KIND_SKIP, kinds).astype(I32)


def _attn_pairs(kinds2d, nq):
    qi = np.concatenate([np.full(q + 1, q, np.int32) for q in range(nq)])
    ki = np.concatenate([np.arange(q + 1, dtype=np.int32) for q in range(nq)])
    return jnp.asarray(qi), jnp.asarray(ki), kinds2d[qi, ki].astype(I32)


def _t5_bias_minus_last(rel, rb_ref, h):
    n = jnp.maximum(rel, 0)
    max_exact = REL_BUCKETS // 2
    nf = jnp.maximum(n, 1).astype(F32)
    large = max_exact + (jnp.log(nf / max_exact) / math.log(REL_MAX_DIST / max_exact)
                         * (REL_BUCKETS - max_exact)).astype(I32)
    large = jnp.minimum(large, REL_BUCKETS - 1)
    bucket = jnp.where(n < max_exact, n, large)
    last = rb_ref[REL_BUCKETS - 1, h]
    out = jnp.zeros(rel.shape, F32)
    for b in range(REL_BUCKETS - 1):
        out = jnp.where(bucket == b, rb_ref[b, h] - last, out)
    return out


def _bias_tile_kernel(rb_ref, o_ref, *, t):
    kind = pl.program_id(0)
    h = pl.program_id(1)
    rel = kind * t + lax.broadcasted_iota(I32, (t, t), 0) - lax.broadcasted_iota(I32, (t, t), 1)
    bias = _t5_bias_minus_last(rel, rb_ref, h) * LOG2E
    o_ref[...] = jnp.where(rel >= 0, bias, NEG_INF)


def diff_bias_tiles(rel_bias, heads, t):
    return pl.pallas_call(
        functools.partial(_bias_tile_kernel, t=t),
        grid=(2, heads),
        in_specs=[pl.BlockSpec(memory_space=pltpu.SMEM)],
        out_specs=pl.BlockSpec((None, None, t, t), lambda k, h: (k, h, 0, 0)),
        out_shape=jax.ShapeDtypeStruct((2, heads, t, t), F32),
        compiler_params=_cparams(("arbitrary", "arbitrary"), 8 * t * t * 4),
        name="diff_bias_tiles",
    )(rel_bias)


def _structural_ok(qi, ki, t):
    q_idx = qi * t + lax.broadcasted_iota(I32, (t, t), 0)
    k_idx = ki * t + lax.broadcasted_iota(I32, (t, t), 1)
    return k_idx < ((q_idx // Q_BLOCK) + 1) * Q_BLOCK


def _online_softmax_step(s, v, m_ref, l_ref, acc_ref):
    nl = s.shape[1] // LANE
    m_prev = m_ref[...]
    m_new = jnp.maximum(m_prev, jnp.max(s, axis=-1, keepdims=True))
    alpha = jnp.exp2(m_prev - m_new)
    p = jnp.exp2(s - jnp.tile(m_new, (1, nl)))
    psum = p[:, :LANE]
    for j in range(1, nl):
        psum = psum + p[:, j * LANE:(j + 1) * LANE]
    l_ref[...] = alpha * l_ref[...] + psum
    pv = jnp.dot(p.astype(BF16), v, preferred_element_type=F32)
    acc_ref[...] = jnp.tile(alpha, (1, v.shape[1] // LANE)) * acc_ref[...] + pv
    m_ref[...] = m_new


def _diff_attn_kernel(qi_ref, ki_ref, kind_ref, q_ref, k_ref, v_ref, bt_ref, pq_ref, pk_ref, rb_ref, lam_ref, sg_ref,
                      o_ref, m_ref, l_ref, acc_ref, *, t, hd, hb, out_scale):
    hg, pair = pl.program_id(0), pl.program_id(1)
    qi, ki, kind = qi_ref[pair], ki_ref[pair], kind_ref[pair]
    w = 2 * hd

    @pl.when(ki == 0)
    def _():
        m_ref[...] = jnp.full(m_ref.shape, -jnp.inf, F32)
        l_ref[...] = jnp.zeros(l_ref.shape, F32)
        acc_ref[...] = jnp.zeros(acc_ref.shape, F32)

    def step(adjust):
        for h in range(hb):
            v = v_ref[:, h * w:(h + 1) * w]
            for mp in range(2):
                c0 = h * w + mp * hd
                s = lax.dot_general(q_ref[:, c0:c0 + hd], k_ref[:, c0:c0 + hd], (((1,), (1,)), ((), ())),
                                    preferred_element_type=F32)
                j = 2 * h + mp
                _online_softmax_step(adjust(s, h), v, m_ref.at[j], l_ref.at[j], acc_ref.at[j])

    @pl.when(kind == KIND_PLAIN)
    def _():
        step(lambda s, h: s)

    @pl.when((kind == KIND_DIAG) | (kind == KIND_OFFDIAG))
    def _():
        step(lambda s, h: s + bt_ref[h])

    @pl.when(kind == KIND_GENERAL)
    def _():
        rel = pq_ref[...] - pk_ref[...]
        struct = _structural_ok(qi, ki, t)

        def adjust(s, h):
            add = _t5_bias_minus_last(rel, rb_ref, hg * hb + h) * LOG2E
            s = jnp.where(rel >= 0, s + add, NEG_INF)
            return jnp.where(struct, s, -jnp.inf)
        step(adjust)

    @pl.when(ki == qi)
    def _():
        for h in range(hb):
            l0 = jnp.sum(l_ref[2 * h], axis=-1, keepdims=True)
            l1 = jnp.sum(l_ref[2 * h + 1], axis=-1, keepdims=True)
            o = acc_ref[2 * h] / l0 - lam_ref[0] * (acc_ref[2 * h + 1] / l1)
            o_ref[:, h * w:(h + 1) * w] = (_rmsnorm_f32(o, sg_ref[...]) * out_scale).astype(o_ref.dtype)


def diff_attention(qkv, bias_tiles, kinds2d, pos, rel_bias, lam, sub_g, out_scale, cfg):
    s = qkv.shape[0]
    t = min(cfg.diff_t, s)
    nq = s // t
    heads, hd, hb = cfg.diff_heads, cfg.diff_head_dim, cfg.diff_hb
    assert heads % hb == 0
    ng = heads // hb
    w = 2 * hd
    qi_tab, ki_tab, kind_tab = _attn_pairs(kinds2d, nq)

    grid_spec = pltpu.PrefetchScalarGridSpec(
        num_scalar_prefetch=3,
        grid=(ng, qi_tab.shape[0]),
        in_specs=[
            pl.BlockSpec((t, hb * w), lambda g, p, qt, kt, kd: (qt[p], g)),
            pl.BlockSpec((t, hb * w), lambda g, p, qt, kt, kd: (kt[p], ng + g)),
            pl.BlockSpec((t, hb * w), lambda g, p, qt, kt, kd: (kt[p], 2 * ng + g)),
            pl.BlockSpec((None, hb, t, t), lambda g, p, qt, kt, kd: (jnp.where(kd[p] == KIND_OFFDIAG, 1, 0), g, 0, 0)),
            pl.BlockSpec((t, 1), lambda g, p, qt, kt, kd: (qt[p], 0)),
            pl.BlockSpec((1, t), lambda g, p, qt, kt, kd: (0, kt[p])),
            pl.BlockSpec(memory_space=pltpu.SMEM),
            pl.BlockSpec(memory_space=pltpu.SMEM),
            pl.BlockSpec((1, w), lambda g, p, qt, kt, kd: (0, 0)),
        ],
        out_specs=pl.BlockSpec((t, hb * w), lambda g, p, qt, kt, kd: (qt[p], g)),
        scratch_shapes=[pltpu.VMEM((2 * hb, t, LANE), F32), pltpu.VMEM((2 * hb, t, LANE), F32),
                        pltpu.VMEM((2 * hb, t, w), F32)],
    )
    est = hb * (2 * (4 * t * w * 2 + t * t * 4) + 2 * t * w * 4 + 4 * t * LANE * 4) + 10 * t * t * 4
    return pl.pallas_call(
        functools.partial(_diff_attn_kernel, t=t, hd=hd, hb=hb, out_scale=out_scale),
        grid_spec=grid_spec,
        out_shape=jax.ShapeDtypeStruct((s, heads * w), BF16),
        compiler_params=_cparams(("arbitrary", "arbitrary"), est),
        name="diff_attention",
    )(qi_tab, ki_tab, kind_tab, qkv, qkv, qkv, bias_tiles, pos.reshape(s, 1), pos.reshape(1, s), rel_bias, lam, sub_g)


def _rope_slot(x, c, s1, s2):
    return x * c + pltpu.roll(x, LANE - 32, 1) * s1 + pltpu.roll(x, 32, 1) * s2


def _rope_tables(pos, rope_dim):
    half = rope_dim // 2
    inv = ROPE_THETA ** (-jnp.arange(half, dtype=F32) / half)
    ang = pos.astype(F32)[:, None] * inv
    cos, sin = jnp.cos(ang), jnp.sin(ang)
    z = jnp.zeros_like(cos)
    pad = jnp.zeros((pos.shape[0], LANE - 2 * half), F32)
    c = jnp.concatenate([cos, cos, pad], axis=1)
    s1 = jnp.concatenate([-sin, z, pad], axis=1)
    s2 = jnp.concatenate([z, sin, pad], axis=1)
    return c, s1, s2


def _rope_rows_kernel(x_ref, c_ref, s1_ref, s2_ref, o_ref):
    o_ref[...] = _rope_slot(x_ref[...].astype(F32), c_ref[...], s1_ref[...], s2_ref[...]).astype(o_ref.dtype)


def rope_rows(x, col_block, tables, cfg):
    m = x.shape[0]
    tm = min(cfg.lin_tm, m)
    row = pl.BlockSpec((tm, LANE), lambda i: (i, 0))
    return pl.pallas_call(
        _rope_rows_kernel,
        grid=(m // tm,),
        in_specs=[pl.BlockSpec((tm, LANE), lambda i: (i, col_block)), row, row, row],
        out_specs=row,
        out_shape=jax.ShapeDtypeStruct((m, LANE), BF16),
        compiler_params=_cparams(("arbitrary",), 10 * tm * LANE * 4),
        name="mla_rope_key",
    )(x, *tables)


def _mla_attn_kernel(qi_ref, ki_ref, kind_ref, q_ref, kv_ref, kr_ref, c_ref, s1_ref, s2_ref, pq_ref, pk_ref, o_ref,
                     qs_ref, kc_ref, m_ref, l_ref, acc_ref, *, t, hb):
    pair = pl.program_id(1)
    qi, ki, kind = qi_ref[pair], ki_ref[pair], kind_ref[pair]
    w = 2 * LANE

    @pl.when(ki == 0)
    def _():
        m_ref[...] = jnp.full(m_ref.shape, -jnp.inf, F32)
        l_ref[...] = jnp.zeros(l_ref.shape, F32)
        acc_ref[...] = jnp.zeros(acc_ref.shape, F32)
        for h in range(hb):
            qs_ref[h, :, :LANE] = q_ref[:, h * w:h * w + LANE]
            qr = _rope_slot(q_ref[:, h * w + LANE:(h + 1) * w].astype(F32), c_ref[...], s1_ref[...], s2_ref[...])
            qs_ref[h, :, LANE:] = qr.astype(BF16)

    def step(adjust):
        for h in range(hb):
            kc_ref[h, :, :LANE] = kv_ref[:, h * w:h * w + LANE]
            kc_ref[h, :, LANE:] = kr_ref[...]
            s = lax.dot_general(qs_ref[h], kc_ref[h], (((1,), (1,)), ((), ())), preferred_element_type=F32)
            _online_softmax_step(adjust(s), kv_ref[:, h * w + LANE:(h + 1) * w], m_ref.at[h], l_ref.at[h], acc_ref.at[h])

    @pl.when(kind == KIND_PLAIN)
    def _():
        step(lambda s: s)

    @pl.when(kind == KIND_GENERAL)
    def _():
        rel = pq_ref[...] - pk_ref[...]
        struct = _structural_ok(qi, ki, t)

        def adjust(s):
            s = jnp.where(rel >= 0, s, NEG_INF)
            return jnp.where(struct, s, -jnp.inf)
        step(adjust)

    @pl.when(ki == qi)
    def _():
        for h in range(hb):
            l = jnp.sum(l_ref[h], axis=-1, keepdims=True)
            o_ref[:, h * LANE:(h + 1) * LANE] = (acc_ref[h] / l).astype(o_ref.dtype)


def mla_attention(q_cat, kv, kr_rot, tables, kinds2d, pos, cfg):
    s = q_cat.shape[0]
    t = min(cfg.mla_t, s)
    nq = s // t
    heads, hb = cfg.mla_heads, cfg.mla_hb
    assert heads % hb == 0
    w = 2 * LANE
    qi_tab, ki_tab, kind_tab = _attn_pairs(kinds2d, nq)
    qrow = pl.BlockSpec((t, LANE), lambda g, p, qt, kt, kd: (qt[p], 0))
    grid_spec = pltpu.PrefetchScalarGridSpec(
        num_scalar_prefetch=3,
        grid=(heads // hb, qi_tab.shape[0]),
        in_specs=[
            pl.BlockSpec((t, hb * w), lambda g, p, qt, kt, kd: (qt[p], g)),
            pl.BlockSpec((t, hb * w), lambda g, p, qt, kt, kd: (kt[p], g)),
            pl.BlockSpec((t, LANE), lambda g, p, qt, kt, kd: (kt[p], 0)),
            qrow, qrow, qrow,
            pl.BlockSpec((t, 1), lambda g, p, qt, kt, kd: (qt[p], 0)),
            pl.BlockSpec((1, t), lambda g, p, qt, kt, kd: (0, kt[p])),
        ],
        out_specs=pl.BlockSpec((t, hb * LANE), lambda g, p, qt, kt, kd: (qt[p], g)),
        scratch_shapes=[pltpu.VMEM((hb, t, w), BF16), pltpu.VMEM((hb, t, w), BF16),
                        pltpu.VMEM((hb, t, LANE), F32), pltpu.VMEM((hb, t, LANE), F32), pltpu.VMEM((hb, t, LANE), F32)],
    )
    est = hb * (2 * (2 * t * w * 2 + t * LANE * 2) + 2 * t * w * 2 + 3 * t * LANE * 4) + 8 * t * LANE * 4 + 10 * t * t * 4
    return pl.pallas_call(
        functools.partial(_mla_attn_kernel, t=t, hb=hb),
        grid_spec=grid_spec,
        out_shape=jax.ShapeDtypeStruct((s, heads * cfg.mla_v), BF16),
        compiler_params=_cparams(("arbitrary", "arbitrary"), est),
        name="mla_attention",
    )(qi_tab, ki_tab, kind_tab, q_cat, kv, kr_rot, *tables, pos.reshape(s, 1), pos.reshape(1, s))


def _router_kernel(h_ref, g_ref, sc_ref, sh_ref, wr_ref, br_ref, u_ref, idx_ref, gate_ref, *, top_k):
    u = _rmsnorm_f32(h_ref[...], g_ref[...]) * (1.0 + sc_ref[...]) + sh_ref[...]
    half = u.shape[-1] // 2
    ub = u.astype(BF16).astype(F32)
    lo = lax.shift_right_logical(lax.bitcast_convert_type(ub[:, :half], I32), 16)
    hi = lax.bitcast_convert_type(ub[:, half:], I32) & (-65536)
    u_ref[...] = lo | hi
    w = wr_ref[...]
    u_hi, w_hi = u.astype(BF16), w.astype(BF16)
    u_lo = (u - u_hi.astype(F32)).astype(BF16)
    w_lo = (w - w_hi.astype(F32)).astype(BF16)
    logits = (jnp.dot(u_hi, w_hi, preferred_element_type=F32) + jnp.dot(u_lo, w_hi, preferred_element_type=F32)
              + jnp.dot(u_hi, w_lo, preferred_element_type=F32)) + br_ref[...]
    n_e = logits.shape[-1]
    lane = lax.broadcasted_iota(I32, logits.shape, 1)
    vals, idxs = [], []
    cur = logits
    for _ in range(top_k):
        mx = jnp.max(cur, axis=-1, keepdims=True)
        ix = jnp.min(jnp.where(cur == mx, lane, n_e), axis=-1, keepdims=True)
        vals.append(mx)
        idxs.append(ix)
        cur = jnp.where(lane == ix, -jnp.inf, cur)
    v = jnp.concatenate(vals, axis=1)
    e = jnp.exp(v - vals[0])
    gate_ref[...] = e / jnp.sum(e, axis=-1, keepdims=True)
    idx_ref[...] = jnp.concatenate(idxs, axis=1)


def router(h, g, sc, sh, w_router, b_router, cfg):
    t, d = h.shape
    e = w_router.shape[1]
    tm = min(cfg.router_tm, t)
    vec = pl.BlockSpec((1, d), lambda i: (0, 0))
    return pl.pallas_call(
        functools.partial(_router_kernel, top_k=cfg.top_k),
        grid=(t // tm,),
        in_specs=[pl.BlockSpec((tm, d), lambda i: (i, 0)), vec, vec, vec,
                  pl.BlockSpec((d, e), lambda i: (0, 0)), pl.BlockSpec((1, e), lambda i: (0, 0))],
        out_specs=[pl.BlockSpec((tm, d // 2), lambda i: (i, 0)),
                   pl.BlockSpec((tm, cfg.top_k), lambda i: (i, 0)),
                   pl.BlockSpec((tm, cfg.top_k), lambda i: (i, 0))],
        out_shape=[jax.ShapeDtypeStruct((t, d // 2), I32),
                   jax.ShapeDtypeStruct((t, cfg.top_k), I32),
                   jax.ShapeDtypeStruct((t, cfg.top_k), F32)],
        compiler_params=_cparams(("arbitrary",), 6 * tm * d * 4 + 2 * d * LANE * 4),
        name="moe_router",
    )(h, g, sc, sh, w_router, b_router)


def _routing_tables(idx, cfg):
    t, k = idx.shape
    a = t * k
    e = cfg.n_experts
    r = cfg.moe_rows
    p_rows = a + e * MOE_BLOCK
    g_max = e + a // r + 1
    e_flat = idx.reshape(a)
    onehot = (e_flat[:, None] == jnp.arange(e, dtype=I32)[None, :]).astype(I32)
    csum = jnp.cumsum(onehot, axis=0)
    counts = csum[-1]
    rank = jnp.sum(onehot * csum, axis=1) - 1
    padded = ((counts + MOE_BLOCK - 1) // MOE_BLOCK) * MOE_BLOCK
    pad_start = jnp.cumsum(padded) - padded
    pos = (pad_start[e_flat] + rank).astype(I32)
    tok_sorted = jnp.zeros((p_rows + 2 * r,), I32).at[pos].set(jnp.arange(a, dtype=I32) // k)
    n_grp = (padded + r - 1) // r
    cum = jnp.cumsum(n_grp)
    n_groups = cum[-1]
    gid = jnp.arange(g_max, dtype=I32)
    last = jnp.maximum(n_groups - 1, 0)
    gid_c = jnp.minimum(gid, last)
    g_exp = jnp.minimum(jnp.searchsorted(cum, gid_c, side="right"), e - 1).astype(I32)
    local = gid_c - (cum - n_grp)[g_exp]
    g_row = (pad_start[g_exp] + local * r).astype(I32)
    g_n = jnp.clip(padded[g_exp] - local * r, 0, r).astype(I32)
    g_n = jnp.where(gid < n_groups, g_n, 0)
    return tok_sorted, pos, g_exp, g_row, g_n, n_groups.astype(I32), p_rows


MOE_ARM_BLOCKS = (8, 4, 2, 1)
MOE_GATHER_SHARE = 4


def _moe_kernel(tok_ref, gexp_ref, grow_ref, gn_ref,
                u_hbm, wg_ref, wu_ref, bg_ref, bu_ref, wd_ref, bd_ref, ys_hbm,
                xg_ref, xb_ref, yst_ref, cnt_ref, gsem, osem):
    del gexp_ref
    g, c = pl.program_id(0), pl.program_id(1)
    ng, nc = pl.num_programs(0), pl.num_programs(1)
    n = gn_ref[g]
    row0 = grow_ref[g]
    has_next = g + 1 < ng
    nxt = jnp.minimum(g + 1, gn_ref.shape[0] - 1)
    n_next = jnp.where(has_next, gn_ref[nxt], 0)
    base_next = grow_ref[nxt]

    def gather_row(base, idx):
        tok = tok_ref[base + idx]
        pltpu.make_async_copy(u_hbm.at[pl.ds(tok, 1)], xg_ref.at[pl.ds(idx, 1)], gsem).start()

    def wait_rows(ref, rows, sem):
        rows = pl.multiple_of(rows, MOE_BLOCK)
        pltpu.make_async_copy(ref.at[pl.ds(0, rows)], ref.at[pl.ds(0, rows)], sem).wait()

    @pl.when(c == 0)
    def _():
        @pl.when(g == 0)
        def _():
            def first(i, carry):
                gather_row(row0, i)
                return carry
            lax.fori_loop(0, n, first, 0)
        wait_rows(xg_ref, n, gsem)

        @pl.when(g > 0)
        def _():
            wait_rows(yst_ref, gn_ref[jnp.maximum(g - 1, 0)], osem)

        def cast(i, carry):
            r0 = pl.multiple_of(i * MOE_BLOCK, MOE_BLOCK)
            words = xg_ref[pl.ds(r0, MOE_BLOCK), :]
            half = words.shape[-1]
            lo = lax.bitcast_convert_type(lax.shift_left(words, 16), F32)
            hi = lax.bitcast_convert_type(words & (-65536), F32)
            xb_ref[pl.ds(r0, MOE_BLOCK), :half] = lo.astype(BF16)
            xb_ref[pl.ds(r0, MOE_BLOCK), half:] = hi.astype(BF16)
            yst_ref[pl.ds(r0, MOE_BLOCK), :] = jnp.broadcast_to(bd_ref[...], (MOE_BLOCK, bd_ref.shape[-1]))
            return carry
        lax.fori_loop(0, n // MOE_BLOCK, cast, 0)
        cnt_ref[0] = 0

    def arm(r0, m, n_gather):
        x = xb_ref[pl.ds(r0, m), :]
        hg = jnp.dot(x, wg_ref[...].astype(BF16), preferred_element_type=F32)
        hu = jnp.dot(x, wu_ref[...].astype(BF16), preferred_element_type=F32)
        gate = jnp.minimum(hg + bg_ref[...], SWIGLU_LIMIT)
        up = jnp.clip(hu + bu_ref[...], -SWIGLU_LIMIT, SWIGLU_LIMIT)
        act = ((up + 1.0) * (gate * jax.nn.sigmoid(gate * SWIGLU_ALPHA))).astype(BF16)
        yst_ref[pl.ds(r0, m), :] += jnp.dot(act, wd_ref[...].astype(BF16), preferred_element_type=F32)
        cnt = cnt_ref[0]
        for i in range(n_gather):
            @pl.when(cnt + i < n_next)
            def _():
                gather_row(base_next, cnt + i)
        cnt_ref[0] = cnt + n_gather

        @pl.when(c == nc - 1)
        def _():
            dst0 = pl.multiple_of(row0 + r0, MOE_BLOCK)
            pltpu.make_async_copy(yst_ref.at[pl.ds(r0, m)], ys_hbm.at[pl.ds(dst0, m)], osem).start()

    nb = n // MOE_BLOCK
    top = MOE_ARM_BLOCKS[0]
    top_rows = top * MOE_BLOCK
    rows_cap = xg_ref.shape[0]

    @pl.when(nb >= top)
    def _():
        arm(0, top_rows, -(-rows_cap // nc))
        for blocks in range(1, rows_cap // MOE_BLOCK - top + 1):
            @pl.when(nb - top == blocks)
            def _():
                arm(top_rows, blocks * MOE_BLOCK, 0)

    @pl.when(nb < top)
    def _():
        for blocks in MOE_ARM_BLOCKS[1:]:
            @pl.when((nb & blocks) != 0)
            def _():
                done = (nb // (2 * blocks)) * (2 * blocks)
                arm(pl.multiple_of(done * MOE_BLOCK, blocks * MOE_BLOCK), blocks * MOE_BLOCK,
                    blocks * MOE_BLOCK // MOE_GATHER_SHARE)

    @pl.when(c == nc - 1)
    def _():
        def rest(i, carry):
            gather_row(base_next, i)
            return carry
        lax.fori_loop(jnp.minimum(cnt_ref[0], n_next), n_next, rest, 0)

        @pl.when(g == ng - 1)
        def _():
            wait_rows(yst_ref, n, osem)
            yst_ref[pl.ds(0, MOE_BLOCK), :] = jnp.zeros((MOE_BLOCK, yst_ref.shape[-1]), F32)
            first = (row0 + n) // MOE_BLOCK
            n_fill = ys_hbm.shape[0] // MOE_BLOCK - first

            def fill(i, carry):
                dst0 = pl.multiple_of((first + i) * MOE_BLOCK, MOE_BLOCK)
                pltpu.make_async_copy(yst_ref.at[pl.ds(0, MOE_BLOCK)], ys_hbm.at[pl.ds(dst0, MOE_BLOCK)], osem).start()
                return carry
            lax.fori_loop(0, n_fill, fill, 0)

            def drain(i, carry):
                pltpu.make_async_copy(yst_ref.at[pl.ds(0, MOE_BLOCK)], yst_ref.at[pl.ds(0, MOE_BLOCK)], osem).wait()
                return carry
            lax.fori_loop(0, n_fill, drain, 0)


def moe_experts(u, tok_sorted, g_exp, g_row, g_n, n_groups, layer, w_gu, b_gu, w_down, b_down, p_rows, cfg):
    n_layers, e, d, f2 = w_gu.shape
    assert u.shape[1] * 2 == d and u.dtype == I32
    f = f2 // 2
    tf = min(cfg.moe_tf, f)
    nc = f // tf
    rows = cfg.moe_rows
    assert rows % MOE_BLOCK == 0

    grid_spec = pltpu.PrefetchScalarGridSpec(
        num_scalar_prefetch=4,
        grid=(n_groups, nc),
        in_specs=[
            pl.BlockSpec(memory_space=pl.ANY),
            pl.BlockSpec((None, None, d, tf), lambda g, c, tk, ge, gr, gn: (layer, ge[g], 0, c)),
            pl.BlockSpec((None, None, d, tf), lambda g, c, tk, ge, gr, gn: (layer, ge[g], 0, nc + c)),
            pl.BlockSpec((None, None, 1, tf), lambda g, c, tk, ge, gr, gn: (layer, ge[g], 0, c)),
            pl.BlockSpec((None, None, 1, tf), lambda g, c, tk, ge, gr, gn: (layer, ge[g], 0, nc + c)),
            pl.BlockSpec((None, None, tf, d), lambda g, c, tk, ge, gr, gn: (layer, ge[g], c, 0)),
            pl.BlockSpec((None, None, 1, d), lambda g, c, tk, ge, gr, gn: (layer, ge[g], 0, 0)),
        ],
        out_specs=pl.BlockSpec(memory_space=pl.ANY),
        scratch_shapes=[
            pltpu.VMEM((rows, d // 2), I32), pltpu.VMEM((rows, d), BF16), pltpu.VMEM((rows, d), F32),
            pltpu.SMEM((1,), I32), pltpu.SemaphoreType.DMA(()), pltpu.SemaphoreType.DMA(()),
        ],
    )
    est = rows * d * 8 + 2 * 3 * d * tf * 4 + 3 * d * tf * 2
    return pl.pallas_call(
        _moe_kernel,
        grid_spec=grid_spec,
        out_shape=jax.ShapeDtypeStruct((p_rows, d), F32),
        compiler_params=_cparams(("arbitrary", "arbitrary"), est),
        name="moe_experts",
    )(tok_sorted, g_exp, g_row, g_n,
      u, w_gu, w_gu, b_gu.reshape(n_layers, e, 1, f2), b_gu.reshape(n_layers, e, 1, f2), w_down,
      b_down.reshape(n_layers, e, 1, d))


def _combine_kernel(pos_ref, ys_hbm, h_ref, gate_ref, g2_ref, fg_ref, o_ref, rows_ref, sem, *, tm, top_k, final_norm):
    i = pl.program_id(0)
    n_tiles = pl.num_programs(0)
    n_rows = tm * top_k

    def start_tile(tile, slot, cond):
        base = tile * n_rows
        for j in range(n_rows):
            def issue(j=j):
                src = pos_ref[base + j]
                pltpu.make_async_copy(ys_hbm.at[pl.ds(src, 1)], rows_ref.at[slot, pl.ds(j, 1)], sem.at[slot]).start()
            if cond is None:
                issue()
            else:
                pl.when(cond)(issue)

    def finish(slot):
        pltpu.make_async_copy(rows_ref.at[slot], rows_ref.at[slot], sem.at[slot]).wait()
        gates = gate_ref[...]
        acc = rows_ref[slot, pl.ds(0, tm), :] * gates[:, 0:1]
        for k in range(1, top_k):
            acc = acc + rows_ref[slot, pl.ds(k * tm, tm), :] * gates[:, k:k + 1]
        out = h_ref[...] + g2_ref[...] * acc
        if final_norm:
            out = _rmsnorm_f32(out, fg_ref[...])
        o_ref[...] = out

    @pl.when(i == 0)
    def _():
        start_tile(0, 0, None)

    for parity in range(2):
        @pl.when(i % 2 == parity)
        def _():
            start_tile(i + 1, 1 - parity, i + 1 < n_tiles)
            finish(parity)


def moe_combine(ys, pos, h, gates, g2, final_g, cfg, *, final_norm):
    t, d = h.shape
    top_k = cfg.top_k
    tm = min(cfg.comb_tm, t)
    assert (tm * top_k) % 64 == 0
    pos = pos.reshape(t // tm, tm, top_k).transpose(0, 2, 1).reshape(-1)
    vec = pl.BlockSpec((1, d), lambda i, p: (0, 0))
    grid_spec = pltpu.PrefetchScalarGridSpec(
        num_scalar_prefetch=1,
        grid=(t // tm,),
        in_specs=[pl.BlockSpec(memory_space=pl.ANY),
                  pl.BlockSpec((tm, d), lambda i, p: (i, 0)),
                  pl.BlockSpec((tm, top_k), lambda i, p: (i, 0)), vec, vec],
        out_specs=pl.BlockSpec((tm, d), lambda i, p: (i, 0)),
        scratch_shapes=[pltpu.VMEM((2, tm * top_k, d), F32), pltpu.SemaphoreType.DMA((2,))],
    )
    return pl.pallas_call(
        functools.partial(_combine_kernel, tm=tm, top_k=top_k, final_norm=final_norm),
        grid_spec=grid_spec,
        out_shape=jax.ShapeDtypeStruct((t, d), F32),
        compiler_params=_cparams(("arbitrary",), 2 * tm * top_k * d * 4 + 6 * tm * d * 4),
        name="moe_combine",
    )(pos, ys, h, gates, g2, final_g)


def moe_layer(h, g, sc, sh, g2, w_router, b_router, layer, w_gu, b_gu, w_down, b_down, final_g, cfg, *, final_norm):
    u, idx, gates = router(h, g, sc, sh, w_router, b_router.reshape(1, -1), cfg)
    tok_sorted, pos, g_exp, g_row, g_n, n_groups, p_rows = _routing_tables(idx, cfg)
    ys = moe_experts(u, tok_sorted, g_exp, g_row, g_n, n_groups, layer, w_gu, b_gu, w_down, b_down, p_rows, cfg)
    return moe_combine(ys, pos, h, gates, g2, final_g, cfg, final_norm=final_norm)


def _diff_mixer(h, pos, g, sc, sh, gate, w_qkv, lq1, lk1, lq2, lk2, sub_g, w_o, rel_bias, layer_idx, cfg):
    heads, hd = cfg.diff_heads, cfg.diff_head_dim
    qk_w = heads * 2 * hd
    lam_init = 0.8 - 0.6 * math.exp(-0.3 * (layer_idx - 1))
    lam = (jnp.exp(jnp.sum(lq1 * lk1)) - jnp.exp(jnp.sum(lq2 * lk2)) + lam_init).reshape(1).astype(F32)
    n = w_qkv.shape[1]
    col_scale = jnp.concatenate([jnp.full((qk_w,), hd ** -0.5 * LOG2E, F32), jnp.ones((n - qk_w,), F32)]).reshape(1, n)
    qkv = norm_linear(h, 0, h.shape[1], g, sc, sh, w_qkv.astype(BF16), col_scale, BF16, cfg,
                      modulate=True, name="diff_qkv_proj")
    t = min(cfg.diff_t, h.shape[0])
    tiles = diff_bias_tiles(rel_bias, heads, t)
    kinds = _block_kinds(pos, t, REL_FAR, toeplitz=True)
    o = diff_attention(qkv, tiles, kinds, pos, rel_bias, lam, sub_g.reshape(1, -1), 1.0 - lam_init, cfg)
    return linear_residual(o, w_o.astype(BF16), h, gate, cfg, name="diff_out_proj")


def _mla_mixer(h, pos, g, sc, sh, gate, w_in, q_norm_g, kv_norm_g, w_uq, w_ukv, w_o, cfg):
    d = h.shape[1]
    heads, nope, rope, qr, kvr = cfg.mla_heads, cfg.mla_nope, cfg.mla_rope, cfg.mla_q_rank, cfg.mla_kv_rank
    assert nope == LANE and cfg.mla_v == LANE and rope <= LANE and qr % LANE == 0 and kvr == qr
    w_in_p = jnp.concatenate([w_in, jnp.zeros((d, LANE - rope), w_in.dtype)], axis=1).astype(BF16)
    ones = lambda n: jnp.ones((1, n), F32)
    z = norm_linear(h, 0, d, g, sc, sh, w_in_p, ones(w_in_p.shape[1]), F32, cfg, modulate=True, name="mla_down_proj")
    w_q = w_uq.reshape(qr, heads, nope + rope)
    w_q = jnp.concatenate([w_q, jnp.zeros((qr, heads, 2 * LANE - nope - rope), w_uq.dtype)], axis=2)
    w_q = w_q.reshape(qr, heads * 2 * LANE).astype(BF16)
    zeros_k = jnp.zeros((1, qr), F32)
    q_scale = jnp.full((1, heads * 2 * LANE), (nope + rope) ** -0.5 * LOG2E, F32)
    q_cat = norm_linear(z, 0, qr, q_norm_g.reshape(1, -1), zeros_k, zeros_k, w_q, q_scale, BF16, cfg,
                        modulate=False, name="mla_q_up_proj")
    kv = norm_linear(z, 1, kvr, kv_norm_g.reshape(1, -1), zeros_k, zeros_k, w_ukv.astype(BF16),
                     ones(w_ukv.shape[1]), BF16, cfg, modulate=False, name="mla_kv_up_proj")
    tables = _rope_tables(pos, rope)
    kr_rot = rope_rows(z, (qr + kvr) // LANE, tables, cfg)
    t = min(cfg.mla_t, h.shape[0])
    kinds = _block_kinds(pos, t, 0, toeplitz=False)
    o = mla_attention(q_cat, kv, kr_rot, tables, kinds, pos, cfg)
    return linear_residual(o, w_o.astype(BF16), h, gate, cfg, name="mla_out_proj")


def _forward(cfg, x, c, positions, ada_w, ada_b, norm1_g, norm2_g, final_g, rel_bias,
             diff_w_qkv, diff_lq1, diff_lk1, diff_lq2, diff_lk2, diff_sub_g, diff_w_o,
             mla_w_in, mla_q_norm_g, mla_kv_norm_g, mla_w_uq, mla_w_ukv, mla_w_o,
             router_w, router_b, exp_w_gu, exp_b_gu, exp_w_down, exp_b_down):
    b, s, d = x.shape
    assert b == 1, "kernels are written for a single sequence"
    h = x.reshape(s, d)
    pos = positions.reshape(s).astype(I32)
    mod = ada_modulation(c.reshape(d, 1), ada_w, ada_b, cfg)
    fg = final_g.reshape(1, d)
    for i in range(cfg.depth):
        sh1, sc1, g1, sh2, sc2, g2 = [mod[i, :, j * d:(j + 1) * d] for j in range(6)]
        n1 = norm1_g[i].reshape(1, d)
        j = i // 2
        if i % 2 == 0:
            h = _diff_mixer(h, pos, n1, sc1, sh1, g1, diff_w_qkv[j], diff_lq1[j], diff_lk1[j], diff_lq2[j],
                            diff_lk2[j], diff_sub_g[j], diff_w_o[j], rel_bias, i + 1, cfg)
        else:
            h = _mla_mixer(h, pos, n1, sc1, sh1, g1, mla_w_in[j], mla_q_norm_g[j], mla_kv_norm_g[j],
                           mla_w_uq[j], mla_w_ukv[j], mla_w_o[j], cfg)
        h = moe_layer(h, norm2_g[i].reshape(1, d), sc2, sh2, g2, router_w[i], router_b[i], i, exp_w_gu, exp_b_gu,
                      exp_w_down, exp_b_down, fg, cfg, final_norm=(i == cfg.depth - 1))
    return h.reshape(b, s, d)


def kernel(x, c, positions, ada_w, ada_b, norm1_g, norm2_g, final_g, rel_bias, diff_w_qkv, diff_lq1, diff_lk1, diff_lq2, diff_lk2, diff_sub_g, diff_w_o, mla_w_in, mla_q_norm_g, mla_kv_norm_g, mla_w_uq, mla_w_ukv, mla_w_o, router_w, router_b, exp_w_gu, exp_b_gu, exp_w_down, exp_b_down):
    return _forward(Cfg(), x, c, positions, ada_w, ada_b, norm1_g, norm2_g, final_g, rel_bias,
                    diff_w_qkv, diff_lq1, diff_lk1, diff_lq2, diff_lk2, diff_sub_g, diff_w_o,
                    mla_w_in, mla_q_norm_g, mla_kv_norm_g, mla_w_uq, mla_w_ukv, mla_w_o,
                    router_w, router_b, exp_w_gu, exp_b_gu, exp_w_down, exp_b_down)
```

```python
import dataclasses
import functools
import math

import jax
import jax.numpy as jnp
import numpy as np
from jax import lax
from jax.experimental import pallas as pl
from jax.experimental.pallas import tpu as pltpu

F32 = jnp.float32
BF16 = jnp.bfloat16
I32 = jnp.int32

RMS_EPS = 1e-6
NEG_INF = -1e30
LOG2E = math.log2(math.e)
ROPE_THETA = 10000.0
REL_BUCKETS = 32
REL_MAX_DIST = 128
REL_FAR = 113
SWIGLU_LIMIT = 7.0
SWIGLU_ALPHA = 1.702
Q_BLOCK = 128
MOE_BLOCK = 128
LANE = 128
V7X_VMEM_BYTES = 64 * 1024 * 1024


@dataclasses.dataclass(frozen=True)
class Cfg:
    d_model: int = 2048
    seq: int = 8192
    depth: int = 2
    diff_heads: int = 8
    diff_head_dim: int = 128
    mla_heads: int = 16
    mla_q_rank: int = 512
    mla_kv_rank: int = 512
    mla_nope: int = 128
    mla_rope: int = 64
    mla_v: int = 128
    n_experts: int = 32
    top_k: int = 4
    d_ff: int = 2048
    ada_tn: int = 1024
    lin_tm: int = 1024
    lin_tn: int = 1024
    diff_t: int = 512
    diff_hb: int = 4
    mla_t: int = 1024
    mla_hb: int = 4
    router_tm: int = 256
    moe_rows: int = 1280
    moe_tf: int = 256
    comb_tm: int = 128


def _vmem_limit(nbytes):
    return int(min(nbytes * 1.25 + (6 << 20), V7X_VMEM_BYTES - (6 << 20)))


def _cparams(sem, nbytes):
    return pltpu.CompilerParams(dimension_semantics=sem, vmem_limit_bytes=_vmem_limit(nbytes))


def _ada_kernel(c_ref, w_ref, b_ref, o_ref):
    c = c_ref[...]
    cs = c * jax.nn.sigmoid(c)
    o_ref[...] = jnp.sum(w_ref[...] * cs, axis=0, keepdims=True) + b_ref[...]


def ada_modulation(c_col, ada_w, ada_b, cfg):
    depth, d, n = ada_w.shape
    tn = min(cfg.ada_tn, n)
    assert n % tn == 0
    return pl.pallas_call(
        _ada_kernel,
        grid=(depth, n // tn),
        in_specs=[
            pl.BlockSpec((d, 1), lambda l, j: (0, 0)),
            pl.BlockSpec((None, d, tn), lambda l, j: (l, 0, j)),
            pl.BlockSpec((None, 1, tn), lambda l, j: (l, 0, j)),
        ],
        out_specs=pl.BlockSpec((None, 1, tn), lambda l, j: (l, 0, j)),
        out_shape=jax.ShapeDtypeStruct((depth, 1, n), F32),
        compiler_params=_cparams(("arbitrary", "arbitrary"), 2 * d * tn * 4 + d * tn * 4),
        name="ada_modulation",
    )(c_col, ada_w, ada_b.reshape(depth, 1, n))


def _rmsnorm_f32(x, g):
    return x * lax.rsqrt(jnp.mean(x * x, axis=-1, keepdims=True) + RMS_EPS) * g


def _norm_linear_kernel(x_ref, g_ref, sc_ref, sh_ref, w_ref, cs_ref, o_ref, xn_ref, *, modulate):
    @pl.when(pl.program_id(1) == 0)
    def _():
        y = _rmsnorm_f32(x_ref[...].astype(F32), g_ref[...])
        if modulate:
            y = y * (1.0 + sc_ref[...]) + sh_ref[...]
        xn_ref[...] = y.astype(BF16)

    acc = jnp.dot(xn_ref[...], w_ref[...], preferred_element_type=F32)
    o_ref[...] = (acc * cs_ref[...]).astype(o_ref.dtype)


def norm_linear(x, x_col_block, k, g, sc, sh, w_bf, col_scale, out_dtype, cfg, *, modulate, name):
    m = x.shape[0]
    n = w_bf.shape[1]
    tm = min(cfg.lin_tm, m)
    tn = n if n <= cfg.lin_tn or n % cfg.lin_tn else cfg.lin_tn
    assert m % tm == 0 and n % tn == 0
    out_b = jnp.dtype(out_dtype).itemsize
    est = 2 * tm * k * 4 + tm * k * 2 + 2 * k * tn * 2 + 2 * tm * tn * out_b + tm * k * 4
    return pl.pallas_call(
        functools.partial(_norm_linear_kernel, modulate=modulate),
        grid=(m // tm, n // tn),
        in_specs=[
            pl.BlockSpec((tm, k), lambda i, j: (i, x_col_block)),
            pl.BlockSpec((1, k), lambda i, j: (0, 0)),
            pl.BlockSpec((1, k), lambda i, j: (0, 0)),
            pl.BlockSpec((1, k), lambda i, j: (0, 0)),
            pl.BlockSpec((k, tn), lambda i, j: (0, j)),
            pl.BlockSpec((1, tn), lambda i, j: (0, j)),
        ],
        out_specs=pl.BlockSpec((tm, tn), lambda i, j: (i, j)),
        out_shape=jax.ShapeDtypeStruct((m, n), out_dtype),
        scratch_shapes=[pltpu.VMEM((tm, k), BF16)],
        compiler_params=_cparams(("arbitrary", "arbitrary"), est),
        name=name,
    )(x, g, sc, sh, w_bf, col_scale)


def _linear_res_kernel(a_ref, w_ref, h_ref, g_ref, o_ref):
    acc = jnp.dot(a_ref[...], w_ref[...], preferred_element_type=F32)
    o_ref[...] = h_ref[...] + g_ref[...] * acc


def linear_residual(a_bf, w_bf, h, gate, cfg, *, name):
    m, k = a_bf.shape
    n = w_bf.shape[1]
    tm = min(cfg.lin_tm, m)
    tn = min(cfg.lin_tn, n)
    assert m % tm == 0 and n % tn == 0
    est = 2 * tm * k * 2 + 2 * k * tn * 2 + 4 * tm * tn * 4
    return pl.pallas_call(
        _linear_res_kernel,
        grid=(m // tm, n // tn),
        in_specs=[
            pl.BlockSpec((tm, k), lambda i, j: (i, 0)),
            pl.BlockSpec((k, tn), lambda i, j: (0, j)),
            pl.BlockSpec((tm, tn), lambda i, j: (i, j)),
            pl.BlockSpec((1, tn), lambda i, j: (0, j)),
        ],
        out_specs=pl.BlockSpec((tm, tn), lambda i, j: (i, j)),
        out_shape=jax.ShapeDtypeStruct((m, n), F32),
        compiler_params=_cparams(("arbitrary", "arbitrary"), est),
        name=name,
    )(a_bf, w_bf, h, gate)


KIND_SKIP, KIND_PLAIN, KIND_DIAG, KIND_OFFDIAG, KIND_GENERAL = 0, 1, 2, 3, 4


def _block_kinds(pos, t, far_dist, toeplitz):
    s = pos.shape[0]
    nb = s // t
    pb = pos.reshape(nb, t)
    pmin, pmax = pb.min(axis=1), pb.max(axis=1)
    consecutive = jnp.all(pb == pb[:, :1] + jnp.arange(t, dtype=pos.dtype)[None, :], axis=1)
    qi = jnp.arange(nb)[:, None]
    ki = jnp.arange(nb)[None, :]
    gap = pmin[:, None] - pmax[None, :]
    plain = (ki < qi) & (gap >= (far_dist if far_dist else 0))
    kinds = jnp.where(plain, KIND_PLAIN, KIND_GENERAL)
    if toeplitz:
        both = consecutive[:, None] & consecutive[None, :]
        d = pb[:, 0][:, None] - pb[:, 0][None, :]
        kinds = jnp.where(~plain & both & (ki == qi - 1) & (d == t), KIND_OFFDIAG, kinds)
        kinds = jnp.where(both & (ki == qi) & (d == 0), KIND_DIAG, kinds)
    return jnp.where(ki > qi, KIND_SKIP, kinds).astype(I32)


def _attn_pairs(kinds2d, nq):
    qi = np.concatenate([np.full(q + 1, q, np.int32) for q in range(nq)])
    ki = np.concatenate([np.arange(q + 1, dtype=np.int32) for q in range(nq)])
    return jnp.asarray(qi), jnp.asarray(ki), kinds2d[qi, ki].astype(I32)


def _t5_bias_minus_last(rel, rb_ref, h):
    n = jnp.maximum(rel, 0)
    max_exact = REL_BUCKETS // 2
    nf = jnp.maximum(n, 1).astype(F32)
    large = max_exact + (jnp.log(nf / max_exact) / math.log(REL_MAX_DIST / max_exact)
                         * (REL_BUCKETS - max_exact)).astype(I32)
    large = jnp.minimum(large, REL_BUCKETS - 1)
    bucket = jnp.where(n < max_exact, n, large)
    last = rb_ref[REL_BUCKETS - 1, h]
    out = jnp.zeros(rel.shape, F32)
    for b in range(REL_BUCKETS - 1):
        out = jnp.where(bucket == b, rb_ref[b, h] - last, out)
    return out


def _bias_tile_kernel(rb_ref, o_ref, *, t):
    kind = pl.program_id(0)
    h = pl.program_id(1)
    rel = kind * t + lax.broadcasted_iota(I32, (t, t), 0) - lax.broadcasted_iota(I32, (t, t), 1)
    bias = _t5_bias_minus_last(rel, rb_ref, h) * LOG2E
    o_ref[...] = jnp.where(rel >= 0, bias, NEG_INF)


def diff_bias_tiles(rel_bias, heads, t):
    return pl.pallas_call(
        functools.partial(_bias_tile_kernel, t=t),
        grid=(2, heads),
        in_specs=[pl.BlockSpec(memory_space=pltpu.SMEM)],
        out_specs=pl.BlockSpec((None, None, t, t), lambda k, h: (k, h, 0, 0)),
        out_shape=jax.ShapeDtypeStruct((2, heads, t, t), F32),
        compiler_params=_cparams(("arbitrary", "arbitrary"), 8 * t * t * 4),
        name="diff_bias_tiles",
    )(rel_bias)


def _structural_ok(qi, ki, t):
    q_idx = qi * t + lax.broadcasted_iota(I32, (t, t), 0)
    k_idx = ki * t + lax.broadcasted_iota(I32, (t, t), 1)
    return k_idx < ((q_idx // Q_BLOCK) + 1) * Q_BLOCK


def _online_softmax_step(s, v, m_ref, l_ref, acc_ref):
    nl = s.shape[1] // LANE
    m_prev = m_ref[...]
    m_new = jnp.maximum(m_prev, jnp.max(s, axis=-1, keepdims=True))
    alpha = jnp.exp2(m_prev - m_new)
    p = jnp.exp2(s - jnp.tile(m_new, (1, nl)))
    psum = p[:, :LANE]
    for j in range(1, nl):
        psum = psum + p[:, j * LANE:(j + 1) * LANE]
    l_ref[...] = alpha * l_ref[...] + psum
    pv = jnp.dot(p.astype(BF16), v, preferred_element_type=F32)
    acc_ref[...] = jnp.tile(alpha, (1, v.shape[1] // LANE)) * acc_ref[...] + pv
    m_ref[...] = m_new


def _diff_attn_kernel(qi_ref, ki_ref, kind_ref, q_ref, k_ref, v_ref, bt_ref, pq_ref, pk_ref, rb_ref, lam_ref, sg_ref,
                      o_ref, m_ref, l_ref, acc_ref, *, t, hd, hb, out_scale):
    hg, pair = pl.program_id(0), pl.program_id(1)
    qi, ki, kind = qi_ref[pair], ki_ref[pair], kind_ref[pair]
    w = 2 * hd

    @pl.when(ki == 0)
    def _():
        m_ref[...] = jnp.full(m_ref.shape, -jnp.inf, F32)
        l_ref[...] = jnp.zeros(l_ref.shape, F32)
        acc_ref[...] = jnp.zeros(acc_ref.shape, F32)

    def step(adjust):
        for h in range(hb):
            v = v_ref[:, h * w:(h + 1) * w]
            for mp in range(2):
                c0 = h * w + mp * hd
                s = lax.dot_general(q_ref[:, c0:c0 + hd], k_ref[:, c0:c0 + hd], (((1,), (1,)), ((), ())),
                                    preferred_element_type=F32)
                j = 2 * h + mp
                _online_softmax_step(adjust(s, h), v, m_ref.at[j], l_ref.at[j], acc_ref.at[j])

    @pl.when(kind == KIND_PLAIN)
    def _():
        step(lambda s, h: s)

    @pl.when((kind == KIND_DIAG) | (kind == KIND_OFFDIAG))
    def _():
        step(lambda s, h: s + bt_ref[h])

    @pl.when(kind == KIND_GENERAL)
    def _():
        rel = pq_ref[...] - pk_ref[...]
        struct = _structural_ok(qi, ki, t)

        def adjust(s, h):
            add = _t5_bias_minus_last(rel, rb_ref, hg * hb + h) * LOG2E
            s = jnp.where(rel >= 0, s + add, NEG_INF)
            return jnp.where(struct, s, -jnp.inf)
        step(adjust)

    @pl.when(ki == qi)
    def _():
        for h in range(hb):
            l0 = jnp.sum(l_ref[2 * h], axis=-1, keepdims=True)
            l1 = jnp.sum(l_ref[2 * h + 1], axis=-1, keepdims=True)
            o = acc_ref[2 * h] / l0 - lam_ref[0] * (acc_ref[2 * h + 1] / l1)
            o_ref[:, h * w:(h + 1) * w] = (_rmsnorm_f32(o, sg_ref[...]) * out_scale).astype(o_ref.dtype)


def diff_attention(qkv, bias_tiles, kinds2d, pos, rel_bias, lam, sub_g, out_scale, cfg):
    s = qkv.shape[0]
    t = min(cfg.diff_t, s)
    nq = s // t
    heads, hd, hb = cfg.diff_heads, cfg.diff_head_dim, cfg.diff_hb
    assert heads % hb == 0
    ng = heads // hb
    w = 2 * hd
    qi_tab, ki_tab, kind_tab = _attn_pairs(kinds2d, nq)

    grid_spec = pltpu.PrefetchScalarGridSpec(
        num_scalar_prefetch=3,
        grid=(ng, qi_tab.shape[0]),
        in_specs=[
            pl.BlockSpec((t, hb * w), lambda g, p, qt, kt, kd: (qt[p], g)),
            pl.BlockSpec((t, hb * w), lambda g, p, qt, kt, kd: (kt[p], ng + g)),
            pl.BlockSpec((t, hb * w), lambda g, p, qt, kt, kd: (kt[p], 2 * ng + g)),
            pl.BlockSpec((None, hb, t, t), lambda g, p, qt, kt, kd: (jnp.where(kd[p] == KIND_OFFDIAG, 1, 0), g, 0, 0)),
            pl.BlockSpec((t, 1), lambda g, p, qt, kt, kd: (qt[p], 0)),
            pl.BlockSpec((1, t), lambda g, p, qt, kt, kd: (0, kt[p])),
            pl.BlockSpec(memory_space=pltpu.SMEM),
            pl.BlockSpec(memory_space=pltpu.SMEM),
            pl.BlockSpec((1, w), lambda g, p, qt, kt, kd: (0, 0)),
        ],
        out_specs=pl.BlockSpec((t, hb * w), lambda g, p, qt, kt, kd: (qt[p], g)),
        scratch_shapes=[pltpu.VMEM((2 * hb, t, LANE), F32), pltpu.VMEM((2 * hb, t, LANE), F32),
                        pltpu.VMEM((2 * hb, t, w), F32)],
    )
    est = hb * (2 * (4 * t * w * 2 + t * t * 4) + 2 * t * w * 4 + 4 * t * LANE * 4) + 10 * t * t * 4
    return pl.pallas_call(
        functools.partial(_diff_attn_kernel, t=t, hd=hd, hb=hb, out_scale=out_scale),
        grid_spec=grid_spec,
        out_shape=jax.ShapeDtypeStruct((s, heads * w), BF16),
        compiler_params=_cparams(("arbitrary", "arbitrary"), est),
        name="diff_attention",
    )(qi_tab, ki_tab, kind_tab, qkv, qkv, qkv, bias_tiles, pos.reshape(s, 1), pos.reshape(1, s), rel_bias, lam, sub_g)


def _rope_slot(x, c, s1, s2):
    return x * c + pltpu.roll(x, LANE - 32, 1) * s1 + pltpu.roll(x, 32, 1) * s2


def _rope_tables(pos, rope_dim):
    half = rope_dim // 2
    inv = ROPE_THETA ** (-jnp.arange(half, dtype=F32) / half)
    ang = pos.astype(F32)[:, None] * inv
    cos, sin = jnp.cos(ang), jnp.sin(ang)
    z = jnp.zeros_like(cos)
    pad = jnp.zeros((pos.shape[0], LANE - 2 * half), F32)
    c = jnp.concatenate([cos, cos, pad], axis=1)
    s1 = jnp.concatenate([-sin, z, pad], axis=1)
    s2 = jnp.concatenate([z, sin, pad], axis=1)
    return c, s1, s2


def _rope_rows_kernel(x_ref, c_ref, s1_ref, s2_ref, o_ref):
    o_ref[...] = _rope_slot(x_ref[...].astype(F32), c_ref[...], s1_ref[...], s2_ref[...]).astype(o_ref.dtype)


def rope_rows(x, col_block, tables, cfg):
    m = x.shape[0]
    tm = min(cfg.lin_tm, m)
    row = pl.BlockSpec((tm, LANE), lambda i: (i, 0))
    return pl.pallas_call(
        _rope_rows_kernel,
        grid=(m // tm,),
        in_specs=[pl.BlockSpec((tm, LANE), lambda i: (i, col_block)), row, row, row],
        out_specs=row,
        out_shape=jax.ShapeDtypeStruct((m, LANE), BF16),
        compiler_params=_cparams(("arbitrary",), 10 * tm * LANE * 4),
        name="mla_rope_key",
    )(x, *tables)


def _mla_attn_kernel(qi_ref, ki_ref, kind_ref, q_ref, kv_ref, kr_ref, c_ref, s1_ref, s2_ref, pq_ref, pk_ref, o_ref,
                     qs_ref, kc_ref, m_ref, l_ref, acc_ref, *, t, hb):
    pair = pl.program_id(1)
    qi, ki, kind = qi_ref[pair], ki_ref[pair], kind_ref[pair]
    w = 2 * LANE

    @pl.when(ki == 0)
    def _():
        m_ref[...] = jnp.full(m_ref.shape, -jnp.inf, F32)
        l_ref[...] = jnp.zeros(l_ref.shape, F32)
        acc_ref[...] = jnp.zeros(acc_ref.shape, F32)
        for h in range(hb):
            qs_ref[h, :, :LANE] = q_ref[:, h * w:h * w + LANE]
            qr = _rope_slot(q_ref[:, h * w + LANE:(h + 1) * w].astype(F32), c_ref[...], s1_ref[...], s2_ref[...])
            qs_ref[h, :, LANE:] = qr.astype(BF16)

    def step(adjust):
        for h in range(hb):
            kc_ref[h, :, :LANE] = kv_ref[:, h * w:h * w + LANE]
            kc_ref[h, :, LANE:] = kr_ref[...]
            s = lax.dot_general(qs_ref[h], kc_ref[h], (((1,), (1,)), ((), ())), preferred_element_type=F32)
            _online_softmax_step(adjust(s), kv_ref[:, h * w + LANE:(h + 1) * w], m_ref.at[h], l_ref.at[h], acc_ref.at[h])

    @pl.when(kind == KIND_PLAIN)
    def _():
        step(lambda s: s)

    @pl.when(kind == KIND_DIAG)
    def _():
        causal = lax.broadcasted_iota(I32, (t, t), 1) <= lax.broadcasted_iota(I32, (t, t), 0)
        step(lambda s: jnp.where(causal, s, NEG_INF))

    @pl.when(kind == KIND_GENERAL)
    def _():
        rel = pq_ref[...] - pk_ref[...]
        struct = _structural_ok(qi, ki, t)

        def adjust(s):
            s = jnp.where(rel >= 0, s, NEG_INF)
            return jnp.where(struct, s, -jnp.inf)
        step(adjust)

    @pl.when(ki == qi)
    def _():
        for h in range(hb):
            l = jnp.sum(l_ref[h], axis=-1, keepdims=True)
            o_ref[:, h * LANE:(h + 1) * LANE] = (acc_ref[h] / l).astype(o_ref.dtype)


def mla_attention(q_cat, kv, kr_rot, tables, kinds2d, pos, cfg):
    s = q_cat.shape[0]
    t = min(cfg.mla_t, s)
    nq = s // t
    heads, hb = cfg.mla_heads, cfg.mla_hb
    assert heads % hb == 0
    w = 2 * LANE
    qi_tab, ki_tab, kind_tab = _attn_pairs(kinds2d, nq)
    qrow = pl.BlockSpec((t, LANE), lambda g, p, qt, kt, kd: (qt[p], 0))
    grid_spec = pltpu.PrefetchScalarGridSpec(
        num_scalar_prefetch=3,
        grid=(heads // hb, qi_tab.shape[0]),
        in_specs=[
            pl.BlockSpec((t, hb * w), lambda g, p, qt, kt, kd: (qt[p], g)),
            pl.BlockSpec((t, hb * w), lambda g, p, qt, kt, kd: (kt[p], g)),
            pl.BlockSpec((t, LANE), lambda g, p, qt, kt, kd: (kt[p], 0)),
            qrow, qrow, qrow,
            pl.BlockSpec((t, 1), lambda g, p, qt, kt, kd: (qt[p], 0)),
            pl.BlockSpec((1, t), lambda g, p, qt, kt, kd: (0, kt[p])),
        ],
        out_specs=pl.BlockSpec((t, hb * LANE), lambda g, p, qt, kt, kd: (qt[p], g)),
        scratch_shapes=[pltpu.VMEM((hb, t, w), BF16), pltpu.VMEM((hb, t, w), BF16),
                        pltpu.VMEM((hb, t, LANE), F32), pltpu.VMEM((hb, t, LANE), F32), pltpu.VMEM((hb, t, LANE), F32)],
    )
    est = hb * (2 * (2 * t * w * 2 + t * LANE * 2) + 2 * t * w * 2 + 3 * t * LANE * 4) + 8 * t * LANE * 4 + 10 * t * t * 4
    return pl.pallas_call(
        functools.partial(_mla_attn_kernel, t=t, hb=hb),
        grid_spec=grid_spec,
        out_shape=jax.ShapeDtypeStruct((s, heads * cfg.mla_v), BF16),
        compiler_params=_cparams(("arbitrary", "arbitrary"), est),
        name="mla_attention",
    )(qi_tab, ki_tab, kind_tab, q_cat, kv, kr_rot, *tables, pos.reshape(s, 1), pos.reshape(1, s))


def _router_kernel(h_ref, g_ref, sc_ref, sh_ref, wr_ref, br_ref, u_ref, idx_ref, gate_ref, *, top_k):
    u = _rmsnorm_f32(h_ref[...], g_ref[...]) * (1.0 + sc_ref[...]) + sh_ref[...]
    half = u.shape[-1] // 2
    ub = u.astype(BF16).astype(F32)
    lo = lax.shift_right_logical(lax.bitcast_convert_type(ub[:, :half], I32), 16)
    hi = lax.bitcast_convert_type(ub[:, half:], I32) & (-65536)
    u_ref[...] = lo | hi
    w = wr_ref[...]
    u_hi, w_hi = u.astype(BF16), w.astype(BF16)
    u_lo = (u - u_hi.astype(F32)).astype(BF16)
    w_lo = (w - w_hi.astype(F32)).astype(BF16)
    logits = (jnp.dot(u_hi, w_hi, preferred_element_type=F32) + jnp.dot(u_lo, w_hi, preferred_element_type=F32)
              + jnp.dot(u_hi, w_lo, preferred_element_type=F32)) + br_ref[...]
    n_e = logits.shape[-1]
    lane = lax.broadcasted_iota(I32, logits.shape, 1)
    vals, idxs = [], []
    cur = logits
    for _ in range(top_k):
        mx = jnp.max(cur, axis=-1, keepdims=True)
        ix = jnp.min(jnp.where(cur == mx, lane, n_e), axis=-1, keepdims=True)
        vals.append(mx)
        idxs.append(ix)
        cur = jnp.where(lane == ix, -jnp.inf, cur)
    v = jnp.concatenate(vals, axis=1)
    e = jnp.exp(v - vals[0])
    gate_ref[...] = e / jnp.sum(e, axis=-1, keepdims=True)
    idx_ref[...] = jnp.concatenate(idxs, axis=1)


def router(h, g, sc, sh, w_router, b_router, cfg):
    t, d = h.shape
    e = w_router.shape[1]
    tm = min(cfg.router_tm, t)
    vec = pl.BlockSpec((1, d), lambda i: (0, 0))
    return pl.pallas_call(
        functools.partial(_router_kernel, top_k=cfg.top_k),
        grid=(t // tm,),
        in_specs=[pl.BlockSpec((tm, d), lambda i: (i, 0)), vec, vec, vec,
                  pl.BlockSpec((d, e), lambda i: (0, 0)), pl.BlockSpec((1, e), lambda i: (0, 0))],
        out_specs=[pl.BlockSpec((tm, d // 2), lambda i: (i, 0)),
                   pl.BlockSpec((tm, cfg.top_k), lambda i: (i, 0)),
                   pl.BlockSpec((tm, cfg.top_k), lambda i: (i, 0))],
        out_shape=[jax.ShapeDtypeStruct((t, d // 2), I32),
                   jax.ShapeDtypeStruct((t, cfg.top_k), I32),
                   jax.ShapeDtypeStruct((t, cfg.top_k), F32)],
        compiler_params=_cparams(("arbitrary",), 6 * tm * d * 4 + 2 * d * LANE * 4),
        name="moe_router",
    )(h, g, sc, sh, w_router, b_router)


def _routing_tables(idx, cfg):
    t, k = idx.shape
    a = t * k
    e = cfg.n_experts
    r = cfg.moe_rows
    p_rows = a + e * MOE_BLOCK
    g_max = e + a // r + 1
    e_flat = idx.reshape(a)
    onehot = (e_flat[:, None] == jnp.arange(e, dtype=I32)[None, :]).astype(I32)
    csum = jnp.cumsum(onehot, axis=0)
    counts = csum[-1]
    rank = jnp.sum(onehot * csum, axis=1) - 1
    padded = ((counts + MOE_BLOCK - 1) // MOE_BLOCK) * MOE_BLOCK
    pad_start = jnp.cumsum(padded) - padded
    pos = (pad_start[e_flat] + rank).astype(I32)
    tok_sorted = jnp.zeros((p_rows + 2 * r,), I32).at[pos].set(jnp.arange(a, dtype=I32) // k)
    n_grp = (padded + r - 1) // r
    cum = jnp.cumsum(n_grp)
    n_groups = cum[-1]
    gid = jnp.arange(g_max, dtype=I32)
    last = jnp.maximum(n_groups - 1, 0)
    gid_c = jnp.minimum(gid, last)
    g_exp = jnp.minimum(jnp.searchsorted(cum, gid_c, side="right"), e - 1).astype(I32)
    local = gid_c - (cum - n_grp)[g_exp]
    g_row = (pad_start[g_exp] + local * r).astype(I32)
    g_n = jnp.clip(padded[g_exp] - local * r, 0, r).astype(I32)
    g_n = jnp.where(gid < n_groups, g_n, 0)
    return tok_sorted, pos, g_exp, g_row, g_n, n_groups.astype(I32), p_rows


MOE_ARM_BLOCKS = (8, 4, 2, 1)
MOE_GATHER_SHARE = 4


def _moe_kernel(tok_ref, gexp_ref, grow_ref, gn_ref,
                u_hbm, wg_ref, wu_ref, bg_ref, bu_ref, wd_ref, bd_ref, ys_hbm,
                xg_ref, xb_ref, yst_ref, cnt_ref, gsem, osem):
    del gexp_ref
    g, c = pl.program_id(0), pl.program_id(1)
    ng, nc = pl.num_programs(0), pl.num_programs(1)
    n = gn_ref[g]
    row0 = grow_ref[g]
    has_next = g + 1 < ng
    nxt = jnp.minimum(g + 1, gn_ref.shape[0] - 1)
    n_next = jnp.where(has_next, gn_ref[nxt], 0)
    base_next = grow_ref[nxt]

    def gather_row(base, idx):
        tok = tok_ref[base + idx]
        pltpu.make_async_copy(u_hbm.at[pl.ds(tok, 1)], xg_ref.at[pl.ds(idx, 1)], gsem).start()

    def wait_rows(ref, rows, sem):
        rows = pl.multiple_of(rows, MOE_BLOCK)
        pltpu.make_async_copy(ref.at[pl.ds(0, rows)], ref.at[pl.ds(0, rows)], sem).wait()

    @pl.when(c == 0)
    def _():
        @pl.when(g == 0)
        def _():
            def first(i, carry):
                gather_row(row0, i)
                return carry
            lax.fori_loop(0, n, first, 0)
        wait_rows(xg_ref, n, gsem)

        def cast(i, carry):
            r0 = pl.multiple_of(i * MOE_BLOCK, MOE_BLOCK)
            words = xg_ref[pl.ds(r0, MOE_BLOCK), :]
            half = words.shape[-1]
            lo = lax.bitcast_convert_type(lax.shift_left(words, 16), F32)
            hi = lax.bitcast_convert_type(words & (-65536), F32)
            xb_ref[pl.ds(r0, MOE_BLOCK), :half] = lo.astype(BF16)
            xb_ref[pl.ds(r0, MOE_BLOCK), half:] = hi.astype(BF16)
            return carry
        lax.fori_loop(0, n // MOE_BLOCK, cast, 0)

        @pl.when(g > 0)
        def _():
            wait_rows(yst_ref, gn_ref[jnp.maximum(g - 1, 0)], osem)

        def init(i, carry):
            r0 = pl.multiple_of(i * MOE_BLOCK, MOE_BLOCK)
            yst_ref[pl.ds(r0, MOE_BLOCK), :] = jnp.broadcast_to(bd_ref[...], (MOE_BLOCK, bd_ref.shape[-1]))
            return carry
        lax.fori_loop(0, n // MOE_BLOCK, init, 0)
        cnt_ref[0] = 0

    def arm(r0, m, n_gather):
        x = xb_ref[pl.ds(r0, m), :]
        hg = jnp.dot(x, wg_ref[...].astype(BF16), preferred_element_type=F32)
        hu = jnp.dot(x, wu_ref[...].astype(BF16), preferred_element_type=F32)
        gate = jnp.minimum(hg + bg_ref[...], SWIGLU_LIMIT)
        up = jnp.clip(hu + bu_ref[...], -SWIGLU_LIMIT, SWIGLU_LIMIT)
        act = ((up + 1.0) * (gate * jax.nn.sigmoid(gate * SWIGLU_ALPHA))).astype(BF16)
        yst_ref[pl.ds(r0, m), :] += jnp.dot(act, wd_ref[...].astype(BF16), preferred_element_type=F32)
        cnt = cnt_ref[0]
        for i in range(n_gather):
            @pl.when(cnt + i < n_next)
            def _():
                gather_row(base_next, cnt + i)
        cnt_ref[0] = cnt + n_gather

        @pl.when(c == nc - 1)
        def _():
            dst0 = pl.multiple_of(row0 + r0, MOE_BLOCK)
            pltpu.make_async_copy(yst_ref.at[pl.ds(r0, m)], ys_hbm.at[pl.ds(dst0, m)], osem).start()

    nb = n // MOE_BLOCK
    top = MOE_ARM_BLOCKS[0]
    top_rows = top * MOE_BLOCK
    rows_cap = xg_ref.shape[0]

    @pl.when(nb >= top)
    def _():
        arm(0, top_rows, -(-rows_cap // nc))
        for blocks in range(1, rows_cap // MOE_BLOCK - top + 1):
            @pl.when(nb - top == blocks)
            def _():
                arm(top_rows, blocks * MOE_BLOCK, 0)

    @pl.when(nb < top)
    def _():
        for blocks in MOE_ARM_BLOCKS[1:]:
            @pl.when((nb & blocks) != 0)
            def _():
                done = (nb // (2 * blocks)) * (2 * blocks)
                arm(pl.multiple_of(done * MOE_BLOCK, blocks * MOE_BLOCK), blocks * MOE_BLOCK,
                    blocks * MOE_BLOCK // MOE_GATHER_SHARE)

    @pl.when(c == nc - 1)
    def _():
        def rest(i, carry):
            gather_row(base_next, i)
            return carry
        lax.fori_loop(jnp.minimum(cnt_ref[0], n_next), n_next, rest, 0)

        @pl.when(g == ng - 1)
        def _():
            wait_rows(yst_ref, n, osem)
            yst_ref[pl.ds(0, MOE_BLOCK), :] = jnp.zeros((MOE_BLOCK, yst_ref.shape[-1]), F32)
            first = (row0 + n) // MOE_BLOCK
            n_fill = ys_hbm.shape[0] // MOE_BLOCK - first

            def fill(i, carry):
                dst0 = pl.multiple_of((first + i) * MOE_BLOCK, MOE_BLOCK)
                pltpu.make_async_copy(yst_ref.at[pl.ds(0, MOE_BLOCK)], ys_hbm.at[pl.ds(dst0, MOE_BLOCK)], osem).start()
                return carry
            lax.fori_loop(0, n_fill, fill, 0)

            def drain(i, carry):
                pltpu.make_async_copy(yst_ref.at[pl.ds(0, MOE_BLOCK)], yst_ref.at[pl.ds(0, MOE_BLOCK)], osem).wait()
                return carry
            lax.fori_loop(0, n_fill, drain, 0)


def moe_experts(u, tok_sorted, g_exp, g_row, g_n, n_groups, layer, w_gu, b_gu, w_down, b_down, p_rows, cfg):
    n_layers, e, d, f2 = w_gu.shape
    assert u.shape[1] * 2 == d and u.dtype == I32
    f = f2 // 2
    tf = min(cfg.moe_tf, f)
    nc = f // tf
    rows = cfg.moe_rows
    assert rows % MOE_BLOCK == 0

    grid_spec = pltpu.PrefetchScalarGridSpec(
        num_scalar_prefetch=4,
        grid=(n_groups, nc),
        in_specs=[
            pl.BlockSpec(memory_space=pl.ANY),
            pl.BlockSpec((None, None, d, tf), lambda g, c, tk, ge, gr, gn: (layer, ge[g], 0, c)),
            pl.BlockSpec((None, None, d, tf), lambda g, c, tk, ge, gr, gn: (layer, ge[g], 0, nc + c)),
            pl.BlockSpec((None, None, 1, tf), lambda g, c, tk, ge, gr, gn: (layer, ge[g], 0, c)),
            pl.BlockSpec((None, None, 1, tf), lambda g, c, tk, ge, gr, gn: (layer, ge[g], 0, nc + c)),
            pl.BlockSpec((None, None, tf, d), lambda g, c, tk, ge, gr, gn: (layer, ge[g], c, 0)),
            pl.BlockSpec((None, None, 1, d), lambda g, c, tk, ge, gr, gn: (layer, ge[g], 0, 0)),
        ],
        out_specs=pl.BlockSpec(memory_space=pl.ANY),
        scratch_shapes=[
            pltpu.VMEM((rows, d // 2), I32), pltpu.VMEM((rows, d), BF16), pltpu.VMEM((rows, d), F32),
            pltpu.SMEM((1,), I32), pltpu.SemaphoreType.DMA(()), pltpu.SemaphoreType.DMA(()),
        ],
    )
    est = rows * d * 8 + 2 * 3 * d * tf * 4 + 3 * d * tf * 2
    return pl.pallas_call(
        _moe_kernel,
        grid_spec=grid_spec,
        out_shape=jax.ShapeDtypeStruct((p_rows, d), F32),
        compiler_params=_cparams(("arbitrary", "arbitrary"), est),
        name="moe_experts",
    )(tok_sorted, g_exp, g_row, g_n,
      u, w_gu, w_gu, b_gu.reshape(n_layers, e, 1, f2), b_gu.reshape(n_layers, e, 1, f2), w_down,
      b_down.reshape(n_layers, e, 1, d))


def _combine_kernel(pos_ref, ys_hbm, h_ref, gate_ref, g2_ref, fg_ref, o_ref, rows_ref, sem, *, tm, top_k, final_norm):
    i = pl.program_id(0)
    n_tiles = pl.num_programs(0)
    n_rows = tm * top_k

    def start_tile(tile, slot, cond):
        base = tile * n_rows
        for j in range(n_rows):
            def issue(j=j):
                src = pos_ref[base + j]
                pltpu.make_async_copy(ys_hbm.at[pl.ds(src, 1)], rows_ref.at[slot, pl.ds(j, 1)], sem.at[slot]).start()
            if cond is None:
                issue()
            else:
                pl.when(cond)(issue)

    def finish(slot):
        pltpu.make_async_copy(rows_ref.at[slot], rows_ref.at[slot], sem.at[slot]).wait()
        gates = gate_ref[...]
        acc = rows_ref[slot, pl.ds(0, tm), :] * gates[:, 0:1]
        for k in range(1, top_k):
            acc = acc + rows_ref[slot, pl.ds(k * tm, tm), :] * gates[:, k:k + 1]
        out = h_ref[...] + g2_ref[...] * acc
        if final_norm:
            out = _rmsnorm_f32(out, fg_ref[...])
        o_ref[...] = out

    @pl.when(i == 0)
    def _():
        start_tile(0, 0, None)

    for parity in range(2):
        @pl.when(i % 2 == parity)
        def _():
            start_tile(i + 1, 1 - parity, i + 1 < n_tiles)
            finish(parity)


def moe_combine(ys, pos, h, gates, g2, final_g, cfg, *, final_norm):
    t, d = h.shape
    top_k = cfg.top_k
    tm = min(cfg.comb_tm, t)
    assert (tm * top_k) % 64 == 0
    pos = pos.reshape(t // tm, tm, top_k).transpose(0, 2, 1).reshape(-1)
    vec = pl.BlockSpec((1, d), lambda i, p: (0, 0))
    grid_spec = pltpu.PrefetchScalarGridSpec(
        num_scalar_prefetch=1,
        grid=(t // tm,),
        in_specs=[pl.BlockSpec(memory_space=pl.ANY),
                  pl.BlockSpec((tm, d), lambda i, p: (i, 0)),
                  pl.BlockSpec((tm, top_k), lambda i, p: (i, 0)), vec, vec],
        out_specs=pl.BlockSpec((tm, d), lambda i, p: (i, 0)),
        scratch_shapes=[pltpu.VMEM((2, tm * top_k, d), F32), pltpu.SemaphoreType.DMA((2,))],
    )
    return pl.pallas_call(
        functools.partial(_combine_kernel, tm=tm, top_k=top_k, final_norm=final_norm),
        grid_spec=grid_spec,
        out_shape=jax.ShapeDtypeStruct((t, d), F32),
        compiler_params=_cparams(("arbitrary",), 2 * tm * top_k * d * 4 + 6 * tm * d * 4),
        name="moe_combine",
    )(pos, ys, h, gates, g2, final_g)


def moe_layer(h, g, sc, sh, g2, w_router, b_router, layer, w_gu, b_gu, w_down, b_down, final_g, cfg, *, final_norm):
    u, idx, gates = router(h, g, sc, sh, w_router, b_router.reshape(1, -1), cfg)
    tok_sorted, pos, g_exp, g_row, g_n, n_groups, p_rows = _routing_tables(idx, cfg)
    ys = moe_experts(u, tok_sorted, g_exp, g_row, g_n, n_groups, layer, w_gu, b_gu, w_down, b_down, p_rows, cfg)
    return moe_combine(ys, pos, h, gates, g2, final_g, cfg, final_norm=final_norm)


def _diff_mixer(h, pos, g, sc, sh, gate, w_qkv, lq1, lk1, lq2, lk2, sub_g, w_o, rel_bias, layer_idx, cfg):
    heads, hd = cfg.diff_heads, cfg.diff_head_dim
    qk_w = heads * 2 * hd
    lam_init = 0.8 - 0.6 * math.exp(-0.3 * (layer_idx - 1))
    lam = (jnp.exp(jnp.sum(lq1 * lk1)) - jnp.exp(jnp.sum(lq2 * lk2)) + lam_init).reshape(1).astype(F32)
    n = w_qkv.shape[1]
    col_scale = jnp.concatenate([jnp.full((qk_w,), hd ** -0.5 * LOG2E, F32), jnp.ones((n - qk_w,), F32)]).reshape(1, n)
    qkv = norm_linear(h, 0, h.shape[1], g, sc, sh, w_qkv.astype(BF16), col_scale, BF16, cfg,
                      modulate=True, name="diff_qkv_proj")
    t = min(cfg.diff_t, h.shape[0])
    tiles = diff_bias_tiles(rel_bias, heads, t)
    kinds = _block_kinds(pos, t, REL_FAR, toeplitz=True)
    o = diff_attention(qkv, tiles, kinds, pos, rel_bias, lam, sub_g.reshape(1, -1), 1.0 - lam_init, cfg)
    return linear_residual(o, w_o.astype(BF16), h, gate, cfg, name="diff_out_proj")


def _mla_mixer(h, pos, g, sc, sh, gate, w_in, q_norm_g, kv_norm_g, w_uq, w_ukv, w_o, cfg):
    d = h.shape[1]
    heads, nope, rope, qr, kvr = cfg.mla_heads, cfg.mla_nope, cfg.mla_rope, cfg.mla_q_rank, cfg.mla_kv_rank
    assert nope == LANE and cfg.mla_v == LANE and rope <= LANE and qr % LANE == 0 and kvr == qr
    w_in_p = jnp.concatenate([w_in, jnp.zeros((d, LANE - rope), w_in.dtype)], axis=1).astype(BF16)
    ones = lambda n: jnp.ones((1, n), F32)
    z = norm_linear(h, 0, d, g, sc, sh, w_in_p, ones(w_in_p.shape[1]), F32, cfg, modulate=True, name="mla_down_proj")
    w_q = w_uq.reshape(qr, heads, nope + rope)
    w_q = jnp.concatenate([w_q, jnp.zeros((qr, heads, 2 * LANE - nope - rope), w_uq.dtype)], axis=2)
    w_q = w_q.reshape(qr, heads * 2 * LANE).astype(BF16)
    zeros_k = jnp.zeros((1, qr), F32)
    q_scale = jnp.full((1, heads * 2 * LANE), (nope + rope) ** -0.5 * LOG2E, F32)
    q_cat = norm_linear(z, 0, qr, q_norm_g.reshape(1, -1), zeros_k, zeros_k, w_q, q_scale, BF16, cfg,
                        modulate=False, name="mla_q_up_proj")
    kv = norm_linear(z, 1, kvr, kv_norm_g.reshape(1, -1), zeros_k, zeros_k, w_ukv.astype(BF16),
                     ones(w_ukv.shape[1]), BF16, cfg, modulate=False, name="mla_kv_up_proj")
    tables = _rope_tables(pos, rope)
    kr_rot = rope_rows(z, (qr + kvr) // LANE, tables, cfg)
    t = min(cfg.mla_t, h.shape[0])
    kinds = _block_kinds(pos, t, 0, toeplitz=True)
    o = mla_attention(q_cat, kv, kr_rot, tables, kinds, pos, cfg)
    return linear_residual(o, w_o.astype(BF16), h, gate, cfg, name="mla_out_proj")


def _forward(cfg, x, c, positions, ada_w, ada_b, norm1_g, norm2_g, final_g, rel_bias,
             diff_w_qkv, diff_lq1, diff_lk1, diff_lq2, diff_lk2, diff_sub_g, diff_w_o,
             mla_w_in, mla_q_norm_g, mla_kv_norm_g, mla_w_uq, mla_w_ukv, mla_w_o,
             router_w, router_b, exp_w_gu, exp_b_gu, exp_w_down, exp_b_down):
    b, s, d = x.shape
    assert b == 1, "kernels are written for a single sequence"
    h = x.reshape(s, d)
    pos = positions.reshape(s).astype(I32)
    mod = ada_modulation(c.reshape(d, 1), ada_w, ada_b, cfg)
    fg = final_g.reshape(1, d)
    for i in range(cfg.depth):
        sh1, sc1, g1, sh2, sc2, g2 = [mod[i, :, j * d:(j + 1) * d] for j in range(6)]
        n1 = norm1_g[i].reshape(1, d)
        j = i // 2
        if i % 2 == 0:
            h = _diff_mixer(h, pos, n1, sc1, sh1, g1, diff_w_qkv[j], diff_lq1[j], diff_lk1[j], diff_lq2[j],
                            diff_lk2[j], diff_sub_g[j], diff_w_o[j], rel_bias, i + 1, cfg)
        else:
            h = _mla_mixer(h, pos, n1, sc1, sh1, g1, mla_w_in[j], mla_q_norm_g[j], mla_kv_norm_g[j],
                           mla_w_uq[j], mla_w_ukv[j], mla_w_o[j], cfg)
        h = moe_layer(h, norm2_g[i].reshape(1, d), sc2, sh2, g2, router_w[i], router_b[i], i, exp_w_gu, exp_b_gu,
                      exp_w_down, exp_b_down, fg, cfg, final_norm=(i == cfg.depth - 1))
    return h.reshape(b, s, d)


def kernel(x, c, positions, ada_w, ada_b, norm1_g, norm2_g, final_g, rel_bias, diff_w_qkv, diff_lq1, diff_lk1, diff_lq2, diff_lk2, diff_sub_g, diff_w_o, mla_w_in, mla_q_norm_g, mla_kv_norm_g, mla_w_uq, mla_w_ukv, mla_w_o, router_w, router_b, exp_w_gu, exp_b_gu, exp_w_down, exp_b_down):
    return _forward(Cfg(), x, c, positions, ada_w, ada_b, norm1_g, norm2_g, final_g, rel_bias,
                    diff_w_qkv, diff_lq1, diff_lk1, diff_lq2, diff_lk2, diff_sub_g, diff_w_o,
                    mla_w_in, mla_q_norm_g, mla_kv_norm_g, mla_w_uq, mla_w_ukv, mla_w_o,
                    router_w, router_b, exp_w_gu, exp_b_gu, exp_w_down, exp_b_down)
```

```python
import dataclasses
import functools
import math

import jax
import jax.numpy as jnp
import numpy as np
from jax import lax
from jax.experimental import pallas as pl
from jax.experimental.pallas import tpu as pltpu

F32 = jnp.float32
BF16 = jnp.bfloat16
I32 = jnp.int32

RMS_EPS = 1e-6
NEG_INF = -1e30
LOG2E = math.log2(math.e)
ROPE_THETA = 10000.0
REL_BUCKETS = 32
REL_MAX_DIST = 128
REL_FAR = 113
SWIGLU_LIMIT = 7.0
SWIGLU_ALPHA = 1.702
Q_BLOCK = 128
MOE_BLOCK = 128
LANE = 128
V7X_VMEM_BYTES = 64 * 1024 * 1024


@dataclasses.dataclass(frozen=True)
class Cfg:
    d_model: int = 2048
    seq: int = 8192
    depth: int = 2
    diff_heads: int = 8
    diff_head_dim: int = 128
    mla_heads: int = 16
    mla_q_rank: int = 512
    mla_kv_rank: int = 512
    mla_nope: int = 128
    mla_rope: int = 64
    mla_v: int = 128
    n_experts: int = 32
    top_k: int = 4
    d_ff: int = 2048
    ada_tn: int = 1024
    lin_tm: int = 1024
    lin_tn: int = 1024
    diff_t: int = 512
    diff_hb: int = 4
    mla_t: int = 1024
    mla_hb: int = 4
    router_tm: int = 256
    moe_rows: int = 1280
    moe_tf: int = 256
    comb_tm: int = 128


def _vmem_limit(nbytes):
    return int(min(nbytes * 1.25 + (6 << 20), V7X_VMEM_BYTES - (6 << 20)))


def _cparams(sem, nbytes):
    return pltpu.CompilerParams(dimension_semantics=sem, vmem_limit_bytes=_vmem_limit(nbytes))


def _ada_kernel(c_ref, w_ref, b_ref, o_ref):
    c = c_ref[...]
    cs = c * jax.nn.sigmoid(c)
    o_ref[...] = jnp.sum(w_ref[...] * cs, axis=0, keepdims=True) + b_ref[...]


def ada_modulation(c_col, ada_w, ada_b, cfg):
    depth, d, n = ada_w.shape
    tn = min(cfg.ada_tn, n)
    assert n % tn == 0
    return pl.pallas_call(
        _ada_kernel,
        grid=(depth, n // tn),
        in_specs=[
            pl.BlockSpec((d, 1), lambda l, j: (0, 0)),
            pl.BlockSpec((None, d, tn), lambda l, j: (l, 0, j)),
            pl.BlockSpec((None, 1, tn), lambda l, j: (l, 0, j)),
        ],
        out_specs=pl.BlockSpec((None, 1, tn), lambda l, j: (l, 0, j)),
        out_shape=jax.ShapeDtypeStruct((depth, 1, n), F32),
        compiler_params=_cparams(("arbitrary", "arbitrary"), 2 * d * tn * 4 + d * tn * 4),
        name="ada_modulation",
    )(c_col, ada_w, ada_b.reshape(depth, 1, n))


def _rmsnorm_f32(x, g):
    return x * lax.rsqrt(jnp.mean(x * x, axis=-1, keepdims=True) + RMS_EPS) * g


def _norm_linear_kernel(x_ref, g_ref, sc_ref, sh_ref, w_ref, cs_ref, o_ref, xn_ref, *, modulate):
    @pl.when(pl.program_id(1) == 0)
    def _():
        y = _rmsnorm_f32(x_ref[...].astype(F32), g_ref[...])
        if modulate:
            y = y * (1.0 + sc_ref[...]) + sh_ref[...]
        xn_ref[...] = y.astype(BF16)

    acc = jnp.dot(xn_ref[...], w_ref[...], preferred_element_type=F32)
    o_ref[...] = (acc * cs_ref[...]).astype(o_ref.dtype)


def norm_linear(x, x_col_block, k, g, sc, sh, w_bf, col_scale, out_dtype, cfg, *, modulate, name):
    m = x.shape[0]
    n = w_bf.shape[1]
    tm = min(cfg.lin_tm, m)
    tn = n if n <= cfg.lin_tn or n % cfg.lin_tn else cfg.lin_tn
    assert m % tm == 0 and n % tn == 0
    out_b = jnp.dtype(out_dtype).itemsize
    est = 2 * tm * k * 4 + tm * k * 2 + 2 * k * tn * 2 + 2 * tm * tn * out_b + tm * k * 4
    return pl.pallas_call(
        functools.partial(_norm_linear_kernel, modulate=modulate),
        grid=(m // tm, n // tn),
        in_specs=[
            pl.BlockSpec((tm, k), lambda i, j: (i, x_col_block)),
            pl.BlockSpec((1, k), lambda i, j: (0, 0)),
            pl.BlockSpec((1, k), lambda i, j: (0, 0)),
            pl.BlockSpec((1, k), lambda i, j: (0, 0)),
            pl.BlockSpec((k, tn), lambda i, j: (0, j)),
            pl.BlockSpec((1, tn), lambda i, j: (0, j)),
        ],
        out_specs=pl.BlockSpec((tm, tn), lambda i, j: (i, j)),
        out_shape=jax.ShapeDtypeStruct((m, n), out_dtype),
        scratch_shapes=[pltpu.VMEM((tm, k), BF16)],
        compiler_params=_cparams(("arbitrary", "arbitrary"), est),
        name=name,
    )(x, g, sc, sh, w_bf, col_scale)


def _linear_res_kernel(a_ref, w_ref, h_ref, g_ref, o_ref):
    acc = jnp.dot(a_ref[...], w_ref[...], preferred_element_type=F32)
    o_ref[...] = h_ref[...] + g_ref[...] * acc


def linear_residual(a_bf, w_bf, h, gate, cfg, *, name):
    m, k = a_bf.shape
    n = w_bf.shape[1]
    tm = min(cfg.lin_tm, m)
    tn = min(cfg.lin_tn, n)
    assert m % tm == 0 and n % tn == 0
    est = 2 * tm * k * 2 + 2 * k * tn * 2 + 4 * tm * tn * 4
    return pl.pallas_call(
        _linear_res_kernel,
        grid=(m // tm, n // tn),
        in_specs=[
            pl.BlockSpec((tm, k), lambda i, j: (i, 0)),
            pl.BlockSpec((k, tn), lambda i, j: (0, j)),
            pl.BlockSpec((tm, tn), lambda i, j: (i, j)),
            pl.BlockSpec((1, tn), lambda i, j: (0, j)),
        ],
        out_specs=pl.BlockSpec((tm, tn), lambda i, j: (i, j)),
        out_shape=jax.ShapeDtypeStruct((m, n), F32),
        compiler_params=_cparams(("arbitrary", "arbitrary"), est),
        name=name,
    )(a_bf, w_bf, h, gate)


KIND_SKIP, KIND_PLAIN, KIND_DIAG, KIND_OFFDIAG, KIND_GENERAL = 0, 1, 2, 3, 4


def _block_kinds(pos, t, far_dist, toeplitz):
    s = pos.shape[0]
    nb = s // t
    pb = pos.reshape(nb, t)
    pmin, pmax = pb.min(axis=1), pb.max(axis=1)
    consecutive = jnp.all(pb == pb[:, :1] + jnp.arange(t, dtype=pos.dtype)[None, :], axis=1)
    qi = jnp.arange(nb)[:, None]
    ki = jnp.arange(nb)[None, :]
    gap = pmin[:, None] - pmax[None, :]
    plain = (ki < qi) & (gap >= (far_dist if far_dist else 0))
    kinds = jnp.where(plain, KIND_PLAIN, KIND_GENERAL)
    if toeplitz:
        both = consecutive[:, None] & consecutive[None, :]
        d = pb[:, 0][:, None] - pb[:, 0][None, :]
        kinds = jnp.where(~plain & both & (ki == qi - 1) & (d == t), KIND_OFFDIAG, kinds)
        kinds = jnp.where(both & (ki == qi) & (d == 0), KIND_DIAG, kinds)
    return jnp.where(ki > qi, KIND_SKIP, kinds).astype(I32)


def _attn_pairs(kinds2d, nq):
    qi = np.concatenate([np.full(q + 1, q, np.int32) for q in range(nq)])
    ki = np.concatenate([np.arange(q + 1, dtype=np.int32) for q in range(nq)])
    return jnp.asarray(qi), jnp.asarray(ki), kinds2d[qi, ki].astype(I32)


def _t5_bias_minus_last(rel, rb_ref, h):
    n = jnp.maximum(rel, 0)
    max_exact = REL_BUCKETS // 2
    nf = jnp.maximum(n, 1).astype(F32)
    large = max_exact + (jnp.log(nf / max_exact) / math.log(REL_MAX_DIST / max_exact)
                         * (REL_BUCKETS - max_exact)).astype(I32)
    large = jnp.minimum(large, REL_BUCKETS - 1)
    bucket = jnp.where(n < max_exact, n, large)
    last = rb_ref[REL_BUCKETS - 1, h]
    out = jnp.zeros(rel.shape, F32)
    for b in range(REL_BUCKETS - 1):
        out = jnp.where(bucket == b, rb_ref[b, h] - last, out)
    return out


def _bias_tile_kernel(rb_ref, o_ref, *, t):
    kind = pl.program_id(0)
    h = pl.program_id(1)
    rel = kind * t + lax.broadcasted_iota(I32, (t, t), 0) - lax.broadcasted_iota(I32, (t, t), 1)
    bias = _t5_bias_minus_last(rel, rb_ref, h) * LOG2E
    o_ref[...] = jnp.where(rel >= 0, bias, NEG_INF)


def diff_bias_tiles(rel_bias, heads, t):
    return pl.pallas_call(
        functools.partial(_bias_tile_kernel, t=t),
        grid=(2, heads),
        in_specs=[pl.BlockSpec(memory_space=pltpu.SMEM)],
        out_specs=pl.BlockSpec((None, None, t, t), lambda k, h: (k, h, 0, 0)),
        out_shape=jax.ShapeDtypeStruct((2, heads, t, t), F32),
        compiler_params=_cparams(("arbitrary", "arbitrary"), 8 * t * t * 4),
        name="diff_bias_tiles",
    )(rel_bias)


def _structural_ok(qi, ki, t):
    q_idx = qi * t + lax.broadcasted_iota(I32, (t, t), 0)
    k_idx = ki * t + lax.broadcasted_iota(I32, (t, t), 1)
    return k_idx < ((q_idx // Q_BLOCK) + 1) * Q_BLOCK


def _online_softmax_step(s, v, m_ref, l_ref, acc_ref):
    nl = s.shape[1] // LANE
    m_prev = m_ref[...]
    m_new = jnp.maximum(m_prev, jnp.max(s, axis=-1, keepdims=True))
    alpha = jnp.exp2(m_prev - m_new)
    p = jnp.exp2(s - jnp.tile(m_new, (1, nl)))
    psum = p[:, :LANE]
    for j in range(1, nl):
        psum = psum + p[:, j * LANE:(j + 1) * LANE]
    l_ref[...] = alpha * l_ref[...] + psum
    pv = jnp.dot(p.astype(BF16), v, preferred_element_type=F32)
    acc_ref[...] = jnp.tile(alpha, (1, v.shape[1] // LANE)) * acc_ref[...] + pv
    m_ref[...] = m_new


def _diff_attn_kernel(qi_ref, ki_ref, kind_ref, q_ref, k_ref, v_ref, bt_ref, pq_ref, pk_ref, rb_ref, lam_ref, sg_ref,
                      o_ref, m_ref, l_ref, acc_ref, *, t, hd, hb, out_scale):
    hg, pair = pl.program_id(0), pl.program_id(1)
    qi, ki, kind = qi_ref[pair], ki_ref[pair], kind_ref[pair]
    w = 2 * hd

    @pl.when(ki == 0)
    def _():
        m_ref[...] = jnp.full(m_ref.shape, -jnp.inf, F32)
        l_ref[...] = jnp.zeros(l_ref.shape, F32)
        acc_ref[...] = jnp.zeros(acc_ref.shape, F32)

    def step(adjust):
        for h in range(hb):
            v = v_ref[:, h * w:(h + 1) * w]
            for mp in range(2):
                c0 = h * w + mp * hd
                s = lax.dot_general(q_ref[:, c0:c0 + hd], k_ref[:, c0:c0 + hd], (((1,), (1,)), ((), ())),
                                    preferred_element_type=F32)
                j = 2 * h + mp
                _online_softmax_step(adjust(s, h), v, m_ref.at[j], l_ref.at[j], acc_ref.at[j])

    @pl.when(kind == KIND_PLAIN)
    def _():
        step(lambda s, h: s)

    @pl.when((kind == KIND_DIAG) | (kind == KIND_OFFDIAG))
    def _():
        step(lambda s, h: s + bt_ref[h])

    @pl.when(kind == KIND_GENERAL)
    def _():
        rel = pq_ref[...] - pk_ref[...]
        struct = _structural_ok(qi, ki, t)

        def adjust(s, h):
            add = _t5_bias_minus_last(rel, rb_ref, hg * hb + h) * LOG2E
            s = jnp.where(rel >= 0, s + add, NEG_INF)
            return jnp.where(struct, s, -jnp.inf)
        step(adjust)

    @pl.when(ki == qi)
    def _():
        for h in range(hb):
            l0 = jnp.sum(l_ref[2 * h], axis=-1, keepdims=True)
            l1 = jnp.sum(l_ref[2 * h + 1], axis=-1, keepdims=True)
            o = acc_ref[2 * h] / l0 - lam_ref[0] * (acc_ref[2 * h + 1] / l1)
            o_ref[:, h * w:(h + 1) * w] = (_rmsnorm_f32(o, sg_ref[...]) * out_scale).astype(o_ref.dtype)


def diff_attention(qkv, bias_tiles, kinds2d, pos, rel_bias, lam, sub_g, out_scale, cfg):
    s = qkv.shape[0]
    t = min(cfg.diff_t, s)
    nq = s // t
    heads, hd, hb = cfg.diff_heads, cfg.diff_head_dim, cfg.diff_hb
    assert heads % hb == 0
    ng = heads // hb
    w = 2 * hd
    qi_tab, ki_tab, kind_tab = _attn_pairs(kinds2d, nq)

    grid_spec = pltpu.PrefetchScalarGridSpec(
        num_scalar_prefetch=3,
        grid=(ng, qi_tab.shape[0]),
        in_specs=[
            pl.BlockSpec((t, hb * w), lambda g, p, qt, kt, kd: (qt[p], g)),
            pl.BlockSpec((t, hb * w), lambda g, p, qt, kt, kd: (kt[p], ng + g)),
            pl.BlockSpec((t, hb * w), lambda g, p, qt, kt, kd: (kt[p], 2 * ng + g)),
            pl.BlockSpec((None, hb, t, t), lambda g, p, qt, kt, kd: (jnp.where(kd[p] == KIND_OFFDIAG, 1, 0), g, 0, 0)),
            pl.BlockSpec((t, 1), lambda g, p, qt, kt, kd: (qt[p], 0)),
            pl.BlockSpec((1, t), lambda g, p, qt, kt, kd: (0, kt[p])),
            pl.BlockSpec(memory_space=pltpu.SMEM),
            pl.BlockSpec(memory_space=pltpu.SMEM),
            pl.BlockSpec((1, w), lambda g, p, qt, kt, kd: (0, 0)),
        ],
        out_specs=pl.BlockSpec((t, hb * w), lambda g, p, qt, kt, kd: (qt[p], g)),
        scratch_shapes=[pltpu.VMEM((2 * hb, t, LANE), F32), pltpu.VMEM((2 * hb, t, LANE), F32),
                        pltpu.VMEM((2 * hb, t, w), F32)],
    )
    est = hb * (2 * (4 * t * w * 2 + t * t * 4) + 2 * t * w * 4 + 4 * t * LANE * 4) + 10 * t * t * 4
    return pl.pallas_call(
        functools.partial(_diff_attn_kernel, t=t, hd=hd, hb=hb, out_scale=out_scale),
        grid_spec=grid_spec,
        out_shape=jax.ShapeDtypeStruct((s, heads * w), BF16),
        compiler_params=_cparams(("arbitrary", "arbitrary"), est),
        name="diff_attention",
    )(qi_tab, ki_tab, kind_tab, qkv, qkv, qkv, bias_tiles, pos.reshape(s, 1), pos.reshape(1, s), rel_bias, lam, sub_g)


def _rope_slot(x, c, s1, s2):
    return x * c + pltpu.roll(x, LANE - 32, 1) * s1 + pltpu.roll(x, 32, 1) * s2


def _rope_tables(pos, rope_dim):
    half = rope_dim // 2
    inv = ROPE_THETA ** (-jnp.arange(half, dtype=F32) / half)
    ang = pos.astype(F32)[:, None] * inv
    cos, sin = jnp.cos(ang), jnp.sin(ang)
    z = jnp.zeros_like(cos)
    pad = jnp.zeros((pos.shape[0], LANE - 2 * half), F32)
    c = jnp.concatenate([cos, cos, pad], axis=1)
    s1 = jnp.concatenate([-sin, z, pad], axis=1)
    s2 = jnp.concatenate([z, sin, pad], axis=1)
    return c, s1, s2


def _rope_rows_kernel(x_ref, c_ref, s1_ref, s2_ref, o_ref):
    o_ref[...] = _rope_slot(x_ref[...].astype(F32), c_ref[...], s1_ref[...], s2_ref[...]).astype(o_ref.dtype)


def rope_rows(x, col_block, tables, cfg):
    m = x.shape[0]
    tm = min(cfg.lin_tm, m)
    row = pl.BlockSpec((tm, LANE), lambda i: (i, 0))
    return pl.pallas_call(
        _rope_rows_kernel,
        grid=(m // tm,),
        in_specs=[pl.BlockSpec((tm, LANE), lambda i: (i, col_block)), row, row, row],
        out_specs=row,
        out_shape=jax.ShapeDtypeStruct((m, LANE), BF16),
        compiler_params=_cparams(("arbitrary",), 10 * tm * LANE * 4),
        name="mla_rope_key",
    )(x, *tables)


def _mla_attn_kernel(qi_ref, ki_ref, kind_ref, q_ref, kv_ref, kr_ref, c_ref, s1_ref, s2_ref, pq_ref, pk_ref, o_ref,
                     qs_ref, kc_ref, m_ref, l_ref, acc_ref, *, t, hb):
    pair = pl.program_id(1)
    qi, ki, kind = qi_ref[pair], ki_ref[pair], kind_ref[pair]
    w = 2 * LANE

    @pl.when(ki == 0)
    def _():
        m_ref[...] = jnp.full(m_ref.shape, -jnp.inf, F32)
        l_ref[...] = jnp.zeros(l_ref.shape, F32)
        acc_ref[...] = jnp.zeros(acc_ref.shape, F32)
        for h in range(hb):
            qs_ref[h, :, :LANE] = q_ref[:, h * w:h * w + LANE]
            qr = _rope_slot(q_ref[:, h * w + LANE:(h + 1) * w].astype(F32), c_ref[...], s1_ref[...], s2_ref[...])
            qs_ref[h, :, LANE:] = qr.astype(BF16)

    def step(adjust):
        for h in range(hb):
            kc_ref[h, :, :LANE] = kv_ref[:, h * w:h * w + LANE]
            kc_ref[h, :, LANE:] = kr_ref[...]
            s = lax.dot_general(qs_ref[h], kc_ref[h], (((1,), (1,)), ((), ())), preferred_element_type=F32)
            _online_softmax_step(adjust(s), kv_ref[:, h * w + LANE:(h + 1) * w], m_ref.at[h], l_ref.at[h], acc_ref.at[h])

    @pl.when(kind == KIND_PLAIN)
    def _():
        step(lambda s: s)

    @pl.when(kind == KIND_DIAG)
    def _():
        causal = lax.broadcasted_iota(I32, (t, t), 1) <= lax.broadcasted_iota(I32, (t, t), 0)
        step(lambda s: jnp.where(causal, s, NEG_INF))

    @pl.when(kind == KIND_GENERAL)
    def _():
        rel = pq_ref[...] - pk_ref[...]
        struct = _structural_ok(qi, ki, t)

        def adjust(s):
            s = jnp.where(rel >= 0, s, NEG_INF)
            return jnp.where(struct, s, -jnp.inf)
        step(adjust)

    @pl.when(ki == qi)
    def _():
        for h in range(hb):
            l = jnp.sum(l_ref[h], axis=-1, keepdims=True)
            o_ref[:, h * LANE:(h + 1) * LANE] = (acc_ref[h] / l).astype(o_ref.dtype)


def mla_attention(q_cat, kv, kr_rot, tables, kinds2d, pos, cfg):
    s = q_cat.shape[0]
    t = min(cfg.mla_t, s)
    nq = s // t
    heads, hb = cfg.mla_heads, cfg.mla_hb
    assert heads % hb == 0
    w = 2 * LANE
    qi_tab, ki_tab, kind_tab = _attn_pairs(kinds2d, nq)
    qrow = pl.BlockSpec((t, LANE), lambda g, p, qt, kt, kd: (qt[p], 0))
    grid_spec = pltpu.PrefetchScalarGridSpec(
        num_scalar_prefetch=3,
        grid=(heads // hb, qi_tab.shape[0]),
        in_specs=[
            pl.BlockSpec((t, hb * w), lambda g, p, qt, kt, kd: (qt[p], g)),
            pl.BlockSpec((t, hb * w), lambda g, p, qt, kt, kd: (kt[p], g)),
            pl.BlockSpec((t, LANE), lambda g, p, qt, kt, kd: (kt[p], 0)),
            qrow, qrow, qrow,
            pl.BlockSpec((t, 1), lambda g, p, qt, kt, kd: (qt[p], 0)),
            pl.BlockSpec((1, t), lambda g, p, qt, kt, kd: (0, kt[p])),
        ],
        out_specs=pl.BlockSpec((t, hb * LANE), lambda g, p, qt, kt, kd: (qt[p], g)),
        scratch_shapes=[pltpu.VMEM((hb, t, w), BF16), pltpu.VMEM((hb, t, w), BF16),
                        pltpu.VMEM((hb, t, LANE), F32), pltpu.VMEM((hb, t, LANE), F32), pltpu.VMEM((hb, t, LANE), F32)],
    )
    est = hb * (2 * (2 * t * w * 2 + t * LANE * 2) + 2 * t * w * 2 + 3 * t * LANE * 4) + 8 * t * LANE * 4 + 10 * t * t * 4
    return pl.pallas_call(
        functools.partial(_mla_attn_kernel, t=t, hb=hb),
        grid_spec=grid_spec,
        out_shape=jax.ShapeDtypeStruct((s, heads * cfg.mla_v), BF16),
        compiler_params=_cparams(("arbitrary", "arbitrary"), est),
        name="mla_attention",
    )(qi_tab, ki_tab, kind_tab, q_cat, kv, kr_rot, *tables, pos.reshape(s, 1), pos.reshape(1, s))


def _router_kernel(h_ref, g_ref, sc_ref, sh_ref, wr_ref, br_ref, u_ref, idx_ref, gate_ref, *, top_k):
    u = _rmsnorm_f32(h_ref[...], g_ref[...]) * (1.0 + sc_ref[...]) + sh_ref[...]
    half = u.shape[-1] // 2
    ub = u.astype(BF16).astype(F32)
    lo = lax.shift_right_logical(lax.bitcast_convert_type(ub[:, :half], I32), 16)
    hi = lax.bitcast_convert_type(ub[:, half:], I32) & (-65536)
    u_ref[...] = lo | hi
    w = wr_ref[...]
    u_hi, w_hi = u.astype(BF16), w.astype(BF16)
    u_lo = (u - u_hi.astype(F32)).astype(BF16)
    w_lo = (w - w_hi.astype(F32)).astype(BF16)
    logits = (jnp.dot(u_hi, w_hi, preferred_element_type=F32) + jnp.dot(u_lo, w_hi, preferred_element_type=F32)
              + jnp.dot(u_hi, w_lo, preferred_element_type=F32)) + br_ref[...]
    n_e = logits.shape[-1]
    lane = lax.broadcasted_iota(I32, logits.shape, 1)
    vals, idxs = [], []
    cur = logits
    for _ in range(top_k):
        mx = jnp.max(cur, axis=-1, keepdims=True)
        ix = jnp.min(jnp.where(cur == mx, lane, n_e), axis=-1, keepdims=True)
        vals.append(mx)
        idxs.append(ix)
        cur = jnp.where(lane == ix, -jnp.inf, cur)
    v = jnp.concatenate(vals, axis=1)
    e = jnp.exp(v - vals[0])
    gate_ref[...] = e / jnp.sum(e, axis=-1, keepdims=True)
    idx_ref[...] = jnp.concatenate(idxs, axis=1)


def router(h, g, sc, sh, w_router, b_router, cfg):
    t, d = h.shape
    e = w_router.shape[1]
    tm = min(cfg.router_tm, t)
    vec = pl.BlockSpec((1, d), lambda i: (0, 0))
    return pl.pallas_call(
        functools.partial(_router_kernel, top_k=cfg.top_k),
        grid=(t // tm,),
        in_specs=[pl.BlockSpec((tm, d), lambda i: (i, 0)), vec, vec, vec,
                  pl.BlockSpec((d, e), lambda i: (0, 0)), pl.BlockSpec((1, e), lambda i: (0, 0))],
        out_specs=[pl.BlockSpec((tm, d // 2), lambda i: (i, 0)),
                   pl.BlockSpec((tm, cfg.top_k), lambda i: (i, 0)),
                   pl.BlockSpec((tm, cfg.top_k), lambda i: (i, 0))],
        out_shape=[jax.ShapeDtypeStruct((t, d // 2), I32),
                   jax.ShapeDtypeStruct((t, cfg.top_k), I32),
                   jax.ShapeDtypeStruct((t, cfg.top_k), F32)],
        compiler_params=_cparams(("arbitrary",), 6 * tm * d * 4 + 2 * d * LANE * 4),
        name="moe_router",
    )(h, g, sc, sh, w_router, b_router)


def _routing_tables(idx, cfg):
    t, k = idx.shape
    a = t * k
    e = cfg.n_experts
    r = cfg.moe_rows
    p_rows = a + e * MOE_BLOCK
    g_max = e + a // r + 1
    e_flat = idx.reshape(a)
    onehot = (e_flat[:, None] == jnp.arange(e, dtype=I32)[None, :]).astype(I32)
    csum = jnp.cumsum(onehot, axis=0)
    counts = csum[-1]
    rank = jnp.sum(onehot * csum, axis=1) - 1
    padded = ((counts + MOE_BLOCK - 1) // MOE_BLOCK) * MOE_BLOCK
    pad_start = jnp.cumsum(padded) - padded
    pos = (pad_start[e_flat] + rank).astype(I32)
    order = jnp.argsort(e_flat, stable=True).astype(I32)
    rows_i = jnp.arange(p_rows + 2 * r, dtype=I32)[:, None]
    past = (rows_i >= (pad_start + padded)[None, :]).astype(I32)
    inside = jnp.concatenate([jnp.ones_like(past[:, :1]), past[:, :-1]], axis=1) - past
    r_row = rows_i[:, 0] - jnp.sum(past * padded[None, :], axis=1)
    src = jnp.clip(jnp.sum(past * counts[None, :], axis=1) + r_row, 0, a - 1)
    valid = r_row < jnp.sum(inside * counts[None, :], axis=1)
    tok_sorted = jnp.where(valid, order[src] // k, 0).astype(I32)
    n_grp = (padded + r - 1) // r
    cum = jnp.cumsum(n_grp)
    n_groups = cum[-1]
    gid = jnp.arange(g_max, dtype=I32)
    last = jnp.maximum(n_groups - 1, 0)
    gid_c = jnp.minimum(gid, last)
    g_exp = jnp.minimum(jnp.searchsorted(cum, gid_c, side="right"), e - 1).astype(I32)
    local = gid_c - (cum - n_grp)[g_exp]
    g_row = (pad_start[g_exp] + local * r).astype(I32)
    g_n = jnp.clip(padded[g_exp] - local * r, 0, r).astype(I32)
    g_n = jnp.where(gid < n_groups, g_n, 0)
    return tok_sorted, pos, g_exp, g_row, g_n, n_groups.astype(I32), p_rows


MOE_ARM_BLOCKS = (8, 4, 2, 1)
MOE_GATHER_SHARE = 4


def _moe_kernel(tok_ref, gexp_ref, grow_ref, gn_ref,
                u_hbm, wg_ref, wu_ref, bg_ref, bu_ref, wd_ref, bd_ref, ys_hbm,
                xg_ref, xb_ref, yst_ref, cnt_ref, gsem, osem):
    del gexp_ref
    g, c = pl.program_id(0), pl.program_id(1)
    ng, nc = pl.num_programs(0), pl.num_programs(1)
    n = gn_ref[g]
    row0 = grow_ref[g]
    has_next = g + 1 < ng
    nxt = jnp.minimum(g + 1, gn_ref.shape[0] - 1)
    n_next = jnp.where(has_next, gn_ref[nxt], 0)
    base_next = grow_ref[nxt]

    def gather_row(base, idx):
        tok = tok_ref[base + idx]
        pltpu.make_async_copy(u_hbm.at[pl.ds(tok, 1)], xg_ref.at[pl.ds(idx, 1)], gsem).start()

    def wait_rows(ref, rows, sem):
        rows = pl.multiple_of(rows, MOE_BLOCK)
        pltpu.make_async_copy(ref.at[pl.ds(0, rows)], ref.at[pl.ds(0, rows)], sem).wait()

    @pl.when(c == 0)
    def _():
        @pl.when(g == 0)
        def _():
            def first(i, carry):
                gather_row(row0, i)
                return carry
            lax.fori_loop(0, n, first, 0)
        wait_rows(xg_ref, n, gsem)

        def cast(i, carry):
            r0 = pl.multiple_of(i * MOE_BLOCK, MOE_BLOCK)
            words = xg_ref[pl.ds(r0, MOE_BLOCK), :]
            half = words.shape[-1]
            lo = lax.bitcast_convert_type(lax.shift_left(words, 16), F32)
            hi = lax.bitcast_convert_type(words & (-65536), F32)
            xb_ref[pl.ds(r0, MOE_BLOCK), :half] = lo.astype(BF16)
            xb_ref[pl.ds(r0, MOE_BLOCK), half:] = hi.astype(BF16)
            return carry
        lax.fori_loop(0, n // MOE_BLOCK, cast, 0)

        @pl.when(g > 0)
        def _():
            wait_rows(yst_ref, gn_ref[jnp.maximum(g - 1, 0)], osem)

        def init(i, carry):
            r0 = pl.multiple_of(i * MOE_BLOCK, MOE_BLOCK)
            yst_ref[pl.ds(r0, MOE_BLOCK), :] = jnp.broadcast_to(bd_ref[...], (MOE_BLOCK, bd_ref.shape[-1]))
            return carry
        lax.fori_loop(0, n // MOE_BLOCK, init, 0)
        cnt_ref[0] = 0

    def arm(r0, m, n_gather):
        x = xb_ref[pl.ds(r0, m), :]
        hg = jnp.dot(x, wg_ref[...].astype(BF16), preferred_element_type=F32)
        hu = jnp.dot(x, wu_ref[...].astype(BF16), preferred_element_type=F32)
        gate = jnp.minimum(hg + bg_ref[...], SWIGLU_LIMIT)
        up = jnp.clip(hu + bu_ref[...], -SWIGLU_LIMIT, SWIGLU_LIMIT)
        act = ((up + 1.0) * (gate * jax.nn.sigmoid(gate * SWIGLU_ALPHA))).astype(BF16)
        yst_ref[pl.ds(r0, m), :] += jnp.dot(act, wd_ref[...].astype(BF16), preferred_element_type=F32)
        cnt = cnt_ref[0]
        for i in range(n_gather):
            @pl.when(cnt + i < n_next)
            def _():
                gather_row(base_next, cnt + i)
        cnt_ref[0] = cnt + n_gather

        @pl.when(c == nc - 1)
        def _():
            dst0 = pl.multiple_of(row0 + r0, MOE_BLOCK)
            pltpu.make_async_copy(yst_ref.at[pl.ds(r0, m)], ys_hbm.at[pl.ds(dst0, m)], osem).start()

    nb = n // MOE_BLOCK
    top = MOE_ARM_BLOCKS[0]
    top_rows = top * MOE_BLOCK
    rows_cap = xg_ref.shape[0]

    @pl.when(nb >= top)
    def _():
        arm(0, top_rows, -(-rows_cap // nc))
        for blocks in range(1, rows_cap // MOE_BLOCK - top + 1):
            @pl.when(nb - top == blocks)
            def _():
                arm(top_rows, blocks * MOE_BLOCK, 0)

    @pl.when(nb < top)
    def _():
        for blocks in MOE_ARM_BLOCKS[1:]:
            @pl.when((nb & blocks) != 0)
            def _():
                done = (nb // (2 * blocks)) * (2 * blocks)
                arm(pl.multiple_of(done * MOE_BLOCK, blocks * MOE_BLOCK), blocks * MOE_BLOCK,
                    blocks * MOE_BLOCK // MOE_GATHER_SHARE)

    @pl.when(c == nc - 1)
    def _():
        def rest(i, carry):
            gather_row(base_next, i)
            return carry
        lax.fori_loop(jnp.minimum(cnt_ref[0], n_next), n_next, rest, 0)

        @pl.when(g == ng - 1)
        def _():
            wait_rows(yst_ref, n, osem)
            yst_ref[pl.ds(0, MOE_BLOCK), :] = jnp.zeros((MOE_BLOCK, yst_ref.shape[-1]), F32)
            first = (row0 + n) // MOE_BLOCK
            n_fill = ys_hbm.shape[0] // MOE_BLOCK - first

            def fill(i, carry):
                dst0 = pl.multiple_of((first + i) * MOE_BLOCK, MOE_BLOCK)
                pltpu.make_async_copy(yst_ref.at[pl.ds(0, MOE_BLOCK)], ys_hbm.at[pl.ds(dst0, MOE_BLOCK)], osem).start()
                return carry
            lax.fori_loop(0, n_fill, fill, 0)

            def drain(i, carry):
                pltpu.make_async_copy(yst_ref.at[pl.ds(0, MOE_BLOCK)], yst_ref.at[pl.ds(0, MOE_BLOCK)], osem).wait()
                return carry
            lax.fori_loop(0, n_fill, drain, 0)


def moe_experts(u, tok_sorted, g_exp, g_row, g_n, n_groups, layer, w_gu, b_gu, w_down, b_down, p_rows, cfg):
    n_layers, e, d, f2 = w_gu.shape
    assert u.shape[1] * 2 == d and u.dtype == I32
    f = f2 // 2
    tf = min(cfg.moe_tf, f)
    nc = f // tf
    rows = cfg.moe_rows
    assert rows % MOE_BLOCK == 0

    grid_spec = pltpu.PrefetchScalarGridSpec(
        num_scalar_prefetch=4,
        grid=(n_groups, nc),
        in_specs=[
            pl.BlockSpec(memory_space=pl.ANY),
            pl.BlockSpec((None, None, d, tf), lambda g, c, tk, ge, gr, gn: (layer, ge[g], 0, c)),
            pl.BlockSpec((None, None, d, tf), lambda g, c, tk, ge, gr, gn: (layer, ge[g], 0, nc + c)),
            pl.BlockSpec((None, None, 1, tf), lambda g, c, tk, ge, gr, gn: (layer, ge[g], 0, c)),
            pl.BlockSpec((None, None, 1, tf), lambda g, c, tk, ge, gr, gn: (layer, ge[g], 0, nc + c)),
            pl.BlockSpec((None, None, tf, d), lambda g, c, tk, ge, gr, gn: (layer, ge[g], c, 0)),
            pl.BlockSpec((None, None, 1, d), lambda g, c, tk, ge, gr, gn: (layer, ge[g], 0, 0)),
        ],
        out_specs=pl.BlockSpec(memory_space=pl.ANY),
        scratch_shapes=[
            pltpu.VMEM((rows, d // 2), I32), pltpu.VMEM((rows, d), BF16), pltpu.VMEM((rows, d), F32),
            pltpu.SMEM((1,), I32), pltpu.SemaphoreType.DMA(()), pltpu.SemaphoreType.DMA(()),
        ],
    )
    est = rows * d * 8 + 2 * 3 * d * tf * 4 + 3 * d * tf * 2
    return pl.pallas_call(
        _moe_kernel,
        grid_spec=grid_spec,
        out_shape=jax.ShapeDtypeStruct((p_rows, d), F32),
        compiler_params=_cparams(("arbitrary", "arbitrary"), est),
        name="moe_experts",
    )(tok_sorted, g_exp, g_row, g_n,
      u, w_gu, w_gu, b_gu.reshape(n_layers, e, 1, f2), b_gu.reshape(n_layers, e, 1, f2), w_down,
      b_down.reshape(n_layers, e, 1, d))


def _combine_kernel(pos_ref, ys_hbm, h_ref, gate_ref, g2_ref, fg_ref, o_ref, rows_ref, sem, *, tm, top_k, final_norm):
    i = pl.program_id(0)
    n_tiles = pl.num_programs(0)
    n_rows = tm * top_k

    def start_tile(tile, slot, cond):
        base = tile * n_rows
        for j in range(n_rows):
            def issue(j=j):
                src = pos_ref[base + j]
                pltpu.make_async_copy(ys_hbm.at[pl.ds(src, 1)], rows_ref.at[slot, pl.ds(j, 1)], sem.at[slot]).start()
            if cond is None:
                issue()
            else:
                pl.when(cond)(issue)

    def finish(slot):
        pltpu.make_async_copy(rows_ref.at[slot], rows_ref.at[slot], sem.at[slot]).wait()
        gates = gate_ref[...]
        acc = rows_ref[slot, pl.ds(0, tm), :] * gates[:, 0:1]
        for k in range(1, top_k):
            acc = acc + rows_ref[slot, pl.ds(k * tm, tm), :] * gates[:, k:k + 1]
        out = h_ref[...] + g2_ref[...] * acc
        if final_norm:
            out = _rmsnorm_f32(out, fg_ref[...])
        o_ref[...] = out

    @pl.when(i == 0)
    def _():
        start_tile(0, 0, None)

    for parity in range(2):
        @pl.when(i % 2 == parity)
        def _():
            start_tile(i + 1, 1 - parity, i + 1 < n_tiles)
            finish(parity)


def moe_combine(ys, pos, h, gates, g2, final_g, cfg, *, final_norm):
    t, d = h.shape
    top_k = cfg.top_k
    tm = min(cfg.comb_tm, t)
    assert (tm * top_k) % 64 == 0
    pos = pos.reshape(t // tm, tm, top_k).transpose(0, 2, 1).reshape(-1)
    vec = pl.BlockSpec((1, d), lambda i, p: (0, 0))
    grid_spec = pltpu.PrefetchScalarGridSpec(
        num_scalar_prefetch=1,
        grid=(t // tm,),
        in_specs=[pl.BlockSpec(memory_space=pl.ANY),
                  pl.BlockSpec((tm, d), lambda i, p: (i, 0)),
                  pl.BlockSpec((tm, top_k), lambda i, p: (i, 0)), vec, vec],
        out_specs=pl.BlockSpec((tm, d), lambda i, p: (i, 0)),
        scratch_shapes=[pltpu.VMEM((2, tm * top_k, d), F32), pltpu.SemaphoreType.DMA((2,))],
    )
    return pl.pallas_call(
        functools.partial(_combine_kernel, tm=tm, top_k=top_k, final_norm=final_norm),
        grid_spec=grid_spec,
        out_shape=jax.ShapeDtypeStruct((t, d), F32),
        compiler_params=_cparams(("arbitrary",), 2 * tm * top_k * d * 4 + 6 * tm * d * 4),
        name="moe_combine",
    )(pos, ys, h, gates, g2, final_g)


def moe_layer(h, g, sc, sh, g2, w_router, b_router, layer, w_gu, b_gu, w_down, b_down, final_g, cfg, *, final_norm):
    u, idx, gates = router(h, g, sc, sh, w_router, b_router.reshape(1, -1), cfg)
    tok_sorted, pos, g_exp, g_row, g_n, n_groups, p_rows = _routing_tables(idx, cfg)
    ys = moe_experts(u, tok_sorted, g_exp, g_row, g_n, n_groups, layer, w_gu, b_gu, w_down, b_down, p_rows, cfg)
    return moe_combine(ys, pos, h, gates, g2, final_g, cfg, final_norm=final_norm)


def _diff_mixer(h, pos, g, sc, sh, gate, w_qkv, lq1, lk1, lq2, lk2, sub_g, w_o, rel_bias, layer_idx, cfg):
    heads, hd = cfg.diff_heads, cfg.diff_head_dim
    qk_w = heads * 2 * hd
    lam_init = 0.8 - 0.6 * math.exp(-0.3 * (layer_idx - 1))
    lam = (jnp.exp(jnp.sum(lq1 * lk1)) - jnp.exp(jnp.sum(lq2 * lk2)) + lam_init).reshape(1).astype(F32)
    n = w_qkv.shape[1]
    col_scale = jnp.concatenate([jnp.full((qk_w,), hd ** -0.5 * LOG2E, F32), jnp.ones((n - qk_w,), F32)]).reshape(1, n)
    qkv = norm_linear(h, 0, h.shape[1], g, sc, sh, w_qkv.astype(BF16), col_scale, BF16, cfg,
                      modulate=True, name="diff_qkv_proj")
    t = min(cfg.diff_t, h.shape[0])
    tiles = diff_bias_tiles(rel_bias, heads, t)
    kinds = _block_kinds(pos, t, REL_FAR, toeplitz=True)
    o = diff_attention(qkv, tiles, kinds, pos, rel_bias, lam, sub_g.reshape(1, -1), 1.0 - lam_init, cfg)
    return linear_residual(o, w_o.astype(BF16), h, gate, cfg, name="diff_out_proj")


def _mla_mixer(h, pos, g, sc, sh, gate, w_in, q_norm_g, kv_norm_g, w_uq, w_ukv, w_o, cfg):
    d = h.shape[1]
    heads, nope, rope, qr, kvr = cfg.mla_heads, cfg.mla_nope, cfg.mla_rope, cfg.mla_q_rank, cfg.mla_kv_rank
    assert nope == LANE and cfg.mla_v == LANE and rope <= LANE and qr % LANE == 0 and kvr == qr
    w_in_p = jnp.concatenate([w_in, jnp.zeros((d, LANE - rope), w_in.dtype)], axis=1).astype(BF16)
    ones = lambda n: jnp.ones((1, n), F32)
    z = norm_linear(h, 0, d, g, sc, sh, w_in_p, ones(w_in_p.shape[1]), F32, cfg, modulate=True, name="mla_down_proj")
    w_q = w_uq.reshape(qr, heads, nope + rope)
    w_q = jnp.concatenate([w_q, jnp.zeros((qr, heads, 2 * LANE - nope - rope), w_uq.dtype)], axis=2)
    w_q = w_q.reshape(qr, heads * 2 * LANE).astype(BF16)
    zeros_k = jnp.zeros((1, qr), F32)
    q_scale = jnp.full((1, heads * 2 * LANE), (nope + rope) ** -0.5 * LOG2E, F32)
    q_cat = norm_linear(z, 0, qr, q_norm_g.reshape(1, -1), zeros_k, zeros_k, w_q, q_scale, BF16, cfg,
                        modulate=False, name="mla_q_up_proj")
    kv = norm_linear(z, 1, kvr, kv_norm_g.reshape(1, -1), zeros_k, zeros_k, w_ukv.astype(BF16),
                     ones(w_ukv.shape[1]), BF16, cfg, modulate=False, name="mla_kv_up_proj")
    tables = _rope_tables(pos, rope)
    kr_rot = rope_rows(z, (qr + kvr) // LANE, tables, cfg)
    t = min(cfg.mla_t, h.shape[0])
    kinds = _block_kinds(pos, t, 0, toeplitz=True)
    o = mla_attention(q_cat, kv, kr_rot, tables, kinds, pos, cfg)
    return linear_residual(o, w_o.astype(BF16), h, gate, cfg, name="mla_out_proj")


def _forward(cfg, x, c, positions, ada_w, ada_b, norm1_g, norm2_g, final_g, rel_bias,
             diff_w_qkv, diff_lq1, diff_lk1, diff_lq2, diff_lk2, diff_sub_g, diff_w_o,
             mla_w_in, mla_q_norm_g, mla_kv_norm_g, mla_w_uq, mla_w_ukv, mla_w_o,
             router_w, router_b, exp_w_gu, exp_b_gu, exp_w_down, exp_b_down):
    b, s, d = x.shape
    assert b == 1, "kernels are written for a single sequence"
    h = x.reshape(s, d)
    pos = positions.reshape(s).astype(I32)
    mod = ada_modulation(c.reshape(d, 1), ada_w, ada_b, cfg)
    fg = final_g.reshape(1, d)
    for i in range(cfg.depth):
        sh1, sc1, g1, sh2, sc2, g2 = [mod[i, :, j * d:(j + 1) * d] for j in range(6)]
        n1 = norm1_g[i].reshape(1, d)
        j = i // 2
        if i % 2 == 0:
            h = _diff_mixer(h, pos, n1, sc1, sh1, g1, diff_w_qkv[j], diff_lq1[j], diff_lk1[j], diff_lq2[j],
                            diff_lk2[j], diff_sub_g[j], diff_w_o[j], rel_bias, i + 1, cfg)
        else:
            h = _mla_mixer(h, pos, n1, sc1, sh1, g1, mla_w_in[j], mla_q_norm_g[j], mla_kv_norm_g[j],
                           mla_w_uq[j], mla_w_ukv[j], mla_w_o[j], cfg)
        h = moe_layer(h, norm2_g[i].reshape(1, d), sc2, sh2, g2, router_w[i], router_b[i], i, exp_w_gu, exp_b_gu,
                      exp_w_down, exp_b_down, fg, cfg, final_norm=(i == cfg.depth - 1))
    return h.reshape(b, s, d)


def kernel(x, c, positions, ada_w, ada_b, norm1_g, norm2_g, final_g, rel_bias, diff_w_qkv, diff_lq1, diff_lk1, diff_lq2, diff_lk2, diff_sub_g, diff_w_o, mla_w_in, mla_q_norm_g, mla_kv_norm_g, mla_w_uq, mla_w_ukv, mla_w_o, router_w, router_b, exp_w_gu, exp_b_gu, exp_w_down, exp_b_down):
    return _forward(Cfg(), x, c, positions, ada_w, ada_b, norm1_g, norm2_g, final_g, rel_bias,
                    diff_w_qkv, diff_lq1, diff_lk1, diff_lq2, diff_lk2, diff_sub_g, diff_w_o,
                    mla_w_in, mla_q_norm_g, mla_kv_norm_g, mla_w_uq, mla_w_ukv, mla_w_o,
                    router_w, router_b, exp_w_gu, exp_b_gu, exp_w_down, exp_b_down)
```

```python
import dataclasses
import functools
import math

import jax
import jax.numpy as jnp
import numpy as np
from jax import lax
from jax.experimental import pallas as pl
from jax.experimental.pallas import tpu as pltpu

F32 = jnp.float32
BF16 = jnp.bfloat16
I32 = jnp.int32

RMS_EPS = 1e-6
NEG_INF = -1e30
LOG2E = math.log2(math.e)
ROPE_THETA = 10000.0
REL_BUCKETS = 32
REL_MAX_DIST = 128
REL_FAR = 113
SWIGLU_LIMIT = 7.0
SWIGLU_ALPHA = 1.702
Q_BLOCK = 128
MOE_BLOCK = 128
LANE = 128
V7X_VMEM_BYTES = 64 * 1024 * 1024


@dataclasses.dataclass(frozen=True)
class Cfg:
    d_model: int = 2048
    seq: int = 8192
    depth: int = 2
    diff_heads: int = 8
    diff_head_dim: int = 128
    mla_heads: int = 16
    mla_q_rank: int = 512
    mla_kv_rank: int = 512
    mla_nope: int = 128
    mla_rope: int = 64
    mla_v: int = 128
    n_experts: int = 32
    top_k: int = 4
    d_ff: int = 2048
    ada_tn: int = 1024
    lin_tm: int = 1024
    lin_tn: int = 1024
    diff_t: int = 512
    diff_hb: int = 4
    mla_t: int = 1024
    mla_hb: int = 4
    router_tm: int = 512
    moe_rows: int = 1280
    moe_tf: int = 256
    comb_tm: int = 128


def _vmem_limit(nbytes):
    return int(min(nbytes * 1.25 + (6 << 20), V7X_VMEM_BYTES - (6 << 20)))


def _cparams(sem, nbytes):
    return pltpu.CompilerParams(dimension_semantics=sem, vmem_limit_bytes=_vmem_limit(nbytes))


def _ada_kernel(c_ref, w_ref, b_ref, o_ref):
    c = c_ref[...]
    cs = c * jax.nn.sigmoid(c)
    o_ref[...] = jnp.sum(w_ref[...] * cs, axis=0, keepdims=True) + b_ref[...]


def ada_modulation(c_col, ada_w, ada_b, cfg):
    depth, d, n = ada_w.shape
    tn = min(cfg.ada_tn, n)
    assert n % tn == 0
    return pl.pallas_call(
        _ada_kernel,
        grid=(depth, n // tn),
        in_specs=[
            pl.BlockSpec((d, 1), lambda l, j: (0, 0)),
            pl.BlockSpec((None, d, tn), lambda l, j: (l, 0, j)),
            pl.BlockSpec((None, 1, tn), lambda l, j: (l, 0, j)),
        ],
        out_specs=pl.BlockSpec((None, 1, tn), lambda l, j: (l, 0, j)),
        out_shape=jax.ShapeDtypeStruct((depth, 1, n), F32),
        compiler_params=_cparams(("arbitrary", "arbitrary"), 2 * d * tn * 4 + d * tn * 4),
        name="ada_modulation",
    )(c_col, ada_w, ada_b.reshape(depth, 1, n))


def _rmsnorm_f32(x, g):
    return x * lax.rsqrt(jnp.mean(x * x, axis=-1, keepdims=True) + RMS_EPS) * g


def _norm_linear_kernel(x_ref, g_ref, sc_ref, sh_ref, w_ref, cs_ref, o_ref, xn_ref, *, modulate):
    @pl.when(pl.program_id(1) == 0)
    def _():
        y = _rmsnorm_f32(x_ref[...].astype(F32), g_ref[...])
        if modulate:
            y = y * (1.0 + sc_ref[...]) + sh_ref[...]
        xn_ref[...] = y.astype(BF16)

    acc = jnp.dot(xn_ref[...], w_ref[...], preferred_element_type=F32)
    o_ref[...] = (acc * cs_ref[...]).astype(o_ref.dtype)


def norm_linear(x, x_col_block, k, g, sc, sh, w_bf, col_scale, out_dtype, cfg, *, modulate, name):
    m = x.shape[0]
    n = w_bf.shape[1]
    tm = min(cfg.lin_tm, m)
    tn = n if n <= cfg.lin_tn or n % cfg.lin_tn else cfg.lin_tn
    assert m % tm == 0 and n % tn == 0
    out_b = jnp.dtype(out_dtype).itemsize
    est = 2 * tm * k * 4 + tm * k * 2 + 2 * k * tn * 2 + 2 * tm * tn * out_b + tm * k * 4
    return pl.pallas_call(
        functools.partial(_norm_linear_kernel, modulate=modulate),
        grid=(m // tm, n // tn),
        in_specs=[
            pl.BlockSpec((tm, k), lambda i, j: (i, x_col_block)),
            pl.BlockSpec((1, k), lambda i, j: (0, 0)),
            pl.BlockSpec((1, k), lambda i, j: (0, 0)),
            pl.BlockSpec((1, k), lambda i, j: (0, 0)),
            pl.BlockSpec((k, tn), lambda i, j: (0, j)),
            pl.BlockSpec((1, tn), lambda i, j: (0, j)),
        ],
        out_specs=pl.BlockSpec((tm, tn), lambda i, j: (i, j)),
        out_shape=jax.ShapeDtypeStruct((m, n), out_dtype),
        scratch_shapes=[pltpu.VMEM((tm, k), BF16)],
        compiler_params=_cparams(("arbitrary", "arbitrary"), est),
        name=name,
    )(x, g, sc, sh, w_bf, col_scale)


def _linear_res_kernel(a_ref, w_ref, h_ref, g_ref, o_ref):
    acc = jnp.dot(a_ref[...], w_ref[...], preferred_element_type=F32)
    o_ref[...] = h_ref[...] + g_ref[...] * acc


def linear_residual(a_bf, w_bf, h, gate, cfg, *, name):
    m, k = a_bf.shape
    n = w_bf.shape[1]
    tm = min(cfg.lin_tm, m)
    tn = min(cfg.lin_tn, n)
    assert m % tm == 0 and n % tn == 0
    est = 2 * tm * k * 2 + 2 * k * tn * 2 + 4 * tm * tn * 4
    return pl.pallas_call(
        _linear_res_kernel,
        grid=(m // tm, n // tn),
        in_specs=[
            pl.BlockSpec((tm, k), lambda i, j: (i, 0)),
            pl.BlockSpec((k, tn), lambda i, j: (0, j)),
            pl.BlockSpec((tm, tn), lambda i, j: (i, j)),
            pl.BlockSpec((1, tn), lambda i, j: (0, j)),
        ],
        out_specs=pl.BlockSpec((tm, tn), lambda i, j: (i, j)),
        out_shape=jax.ShapeDtypeStruct((m, n), F32),
        compiler_params=_cparams(("arbitrary", "arbitrary"), est),
        name=name,
    )(a_bf, w_bf, h, gate)


KIND_SKIP, KIND_PLAIN, KIND_DIAG, KIND_OFFDIAG, KIND_GENERAL = 0, 1, 2, 3, 4


def _block_kinds(pos, t, far_dist, toeplitz):
    s = pos.shape[0]
    nb = s // t
    pb = pos.reshape(nb, t)
    pmin, pmax = pb.min(axis=1), pb.max(axis=1)
    consecutive = jnp.all(pb == pb[:, :1] + jnp.arange(t, dtype=pos.dtype)[None, :], axis=1)
    qi = jnp.arange(nb)[:, None]
    ki = jnp.arange(nb)[None, :]
    gap = pmin[:, None] - pmax[None, :]
    plain = (ki < qi) & (gap >= (far_dist if far_dist else 0))
    kinds = jnp.where(plain, KIND_PLAIN, KIND_GENERAL)
    if toeplitz:
        both = consecutive[:, None] & consecutive[None, :]
        d = pb[:, 0][:, None] - pb[:, 0][None, :]
        kinds = jnp.where(~plain & both & (ki == qi - 1) & (d == t), KIND_OFFDIAG, kinds)
        kinds = jnp.where(both & (ki == qi) & (d == 0), KIND_DIAG, kinds)
    return jnp.where(ki > qi, KIND_SKIP, kinds).astype(I32)


def _attn_pairs(kinds2d, nq):
    qi = np.concatenate([np.full(q + 1, q, np.int32) for q in range(nq)])
    ki = np.concatenate([np.arange(q + 1, dtype=np.int32) for q in range(nq)])
    return jnp.asarray(qi), jnp.asarray(ki), kinds2d[qi, ki].astype(I32)


def _t5_bias_minus_last(rel, rb_ref, h):
    n = jnp.maximum(rel, 0)
    max_exact = REL_BUCKETS // 2
    nf = jnp.maximum(n, 1).astype(F32)
    large = max_exact + (jnp.log(nf / max_exact) / math.log(REL_MAX_DIST / max_exact)
                         * (REL_BUCKETS - max_exact)).astype(I32)
    large = jnp.minimum(large, REL_BUCKETS - 1)
    bucket = jnp.where(n < max_exact, n, large)
    last = rb_ref[REL_BUCKETS - 1, h]
    out = jnp.zeros(rel.shape, F32)
    for b in range(REL_BUCKETS - 1):
        out = jnp.where(bucket == b, rb_ref[b, h] - last, out)
    return out


def _bias_tile_kernel(rb_ref, o_ref, *, t):
    kind = pl.program_id(0)
    h = pl.program_id(1)
    rel = kind * t + lax.broadcasted_iota(I32, (t, t), 0) - lax.broadcasted_iota(I32, (t, t), 1)
    bias = _t5_bias_minus_last(rel, rb_ref, h) * LOG2E
    o_ref[...] = jnp.where(rel >= 0, bias, NEG_INF)


def diff_bias_tiles(rel_bias, heads, t):
    return pl.pallas_call(
        functools.partial(_bias_tile_kernel, t=t),
        grid=(2, heads),
        in_specs=[pl.BlockSpec(memory_space=pltpu.SMEM)],
        out_specs=pl.BlockSpec((None, None, t, t), lambda k, h: (k, h, 0, 0)),
        out_shape=jax.ShapeDtypeStruct((2, heads, t, t), F32),
        compiler_params=_cparams(("arbitrary", "arbitrary"), 8 * t * t * 4),
        name="diff_bias_tiles",
    )(rel_bias)


def _structural_ok(qi, ki, t):
    q_idx = qi * t + lax.broadcasted_iota(I32, (t, t), 0)
    k_idx = ki * t + lax.broadcasted_iota(I32, (t, t), 1)
    return k_idx < ((q_idx // Q_BLOCK) + 1) * Q_BLOCK


def _online_softmax_step(s, v, m_ref, l_ref, acc_ref):
    nl = s.shape[1] // LANE
    m_prev = m_ref[...]
    m_new = jnp.maximum(m_prev, jnp.max(s, axis=-1, keepdims=True))
    alpha = jnp.exp2(m_prev - m_new)
    p = jnp.exp2(s - jnp.tile(m_new, (1, nl)))
    psum = p[:, :LANE]
    for j in range(1, nl):
        psum = psum + p[:, j * LANE:(j + 1) * LANE]
    l_ref[...] = alpha * l_ref[...] + psum
    pv = jnp.dot(p.astype(BF16), v, preferred_element_type=F32)
    acc_ref[...] = jnp.tile(alpha, (1, v.shape[1] // LANE)) * acc_ref[...] + pv
    m_ref[...] = m_new


def _diff_attn_kernel(qi_ref, ki_ref, kind_ref, q_ref, k_ref, v_ref, bt_ref, pq_ref, pk_ref, rb_ref, lam_ref, sg_ref,
                      o_ref, m_ref, l_ref, acc_ref, *, t, hd, hb, out_scale):
    hg, pair = pl.program_id(0), pl.program_id(1)
    qi, ki, kind = qi_ref[pair], ki_ref[pair], kind_ref[pair]
    w = 2 * hd

    @pl.when(ki == 0)
    def _():
        m_ref[...] = jnp.full(m_ref.shape, -jnp.inf, F32)
        l_ref[...] = jnp.zeros(l_ref.shape, F32)
        acc_ref[...] = jnp.zeros(acc_ref.shape, F32)

    def step(adjust):
        for h in range(hb):
            v = v_ref[:, h * w:(h + 1) * w]
            for mp in range(2):
                c0 = h * w + mp * hd
                s = lax.dot_general(q_ref[:, c0:c0 + hd], k_ref[:, c0:c0 + hd], (((1,), (1,)), ((), ())),
                                    preferred_element_type=F32)
                j = 2 * h + mp
                _online_softmax_step(adjust(s, h), v, m_ref.at[j], l_ref.at[j], acc_ref.at[j])

    @pl.when(kind == KIND_PLAIN)
    def _():
        step(lambda s, h: s)

    @pl.when((kind == KIND_DIAG) | (kind == KIND_OFFDIAG))
    def _():
        step(lambda s, h: s + bt_ref[h])

    @pl.when(kind == KIND_GENERAL)
    def _():
        rel = pq_ref[...] - pk_ref[...]
        struct = _structural_ok(qi, ki, t)

        def adjust(s, h):
            add = _t5_bias_minus_last(rel, rb_ref, hg * hb + h) * LOG2E
            s = jnp.where(rel >= 0, s + add, NEG_INF)
            return jnp.where(struct, s, -jnp.inf)
        step(adjust)

    @pl.when(ki == qi)
    def _():
        for h in range(hb):
            l0 = jnp.sum(l_ref[2 * h], axis=-1, keepdims=True)
            l1 = jnp.sum(l_ref[2 * h + 1], axis=-1, keepdims=True)
            o = acc_ref[2 * h] / l0 - lam_ref[0] * (acc_ref[2 * h + 1] / l1)
            o_ref[:, h * w:(h + 1) * w] = (_rmsnorm_f32(o, sg_ref[...]) * out_scale).astype(o_ref.dtype)


def diff_attention(qkv, bias_tiles, kinds2d, pos, rel_bias, lam, sub_g, out_scale, cfg):
    s = qkv.shape[0]
    t = min(cfg.diff_t, s)
    nq = s // t
    heads, hd, hb = cfg.diff_heads, cfg.diff_head_dim, cfg.diff_hb
    assert heads % hb == 0
    ng = heads // hb
    w = 2 * hd
    qi_tab, ki_tab, kind_tab = _attn_pairs(kinds2d, nq)

    grid_spec = pltpu.PrefetchScalarGridSpec(
        num_scalar_prefetch=3,
        grid=(ng, qi_tab.shape[0]),
        in_specs=[
            pl.BlockSpec((t, hb * w), lambda g, p, qt, kt, kd: (qt[p], g)),
            pl.BlockSpec((t, hb * w), lambda g, p, qt, kt, kd: (kt[p], ng + g)),
            pl.BlockSpec((t, hb * w), lambda g, p, qt, kt, kd: (kt[p], 2 * ng + g)),
            pl.BlockSpec((None, hb, t, t), lambda g, p, qt, kt, kd: (jnp.where(kd[p] == KIND_OFFDIAG, 1, 0), g, 0, 0)),
            pl.BlockSpec((t, 1), lambda g, p, qt, kt, kd: (qt[p], 0)),
            pl.BlockSpec((1, t), lambda g, p, qt, kt, kd: (0, kt[p])),
            pl.BlockSpec(memory_space=pltpu.SMEM),
            pl.BlockSpec(memory_space=pltpu.SMEM),
            pl.BlockSpec((1, w), lambda g, p, qt, kt, kd: (0, 0)),
        ],
        out_specs=pl.BlockSpec((t, hb * w), lambda g, p, qt, kt, kd: (qt[p], g)),
        scratch_shapes=[pltpu.VMEM((2 * hb, t, LANE), F32), pltpu.VMEM((2 * hb, t, LANE), F32),
                        pltpu.VMEM((2 * hb, t, w), F32)],
    )
    est = hb * (2 * (4 * t * w * 2 + t * t * 4) + 2 * t * w * 4 + 4 * t * LANE * 4) + 10 * t * t * 4
    return pl.pallas_call(
        functools.partial(_diff_attn_kernel, t=t, hd=hd, hb=hb, out_scale=out_scale),
        grid_spec=grid_spec,
        out_shape=jax.ShapeDtypeStruct((s, heads * w), BF16),
        compiler_params=_cparams(("arbitrary", "arbitrary"), est),
        name="diff_attention",
    )(qi_tab, ki_tab, kind_tab, qkv, qkv, qkv, bias_tiles, pos.reshape(s, 1), pos.reshape(1, s), rel_bias, lam, sub_g)


def _rope_slot(x, c, s1, s2):
    return x * c + pltpu.roll(x, LANE - 32, 1) * s1 + pltpu.roll(x, 32, 1) * s2


def _rope_tables(pos, rope_dim):
    half = rope_dim // 2
    inv = ROPE_THETA ** (-jnp.arange(half, dtype=F32) / half)
    ang = pos.astype(F32)[:, None] * inv
    cos, sin = jnp.cos(ang), jnp.sin(ang)
    z = jnp.zeros_like(cos)
    pad = jnp.zeros((pos.shape[0], LANE - 2 * half), F32)
    c = jnp.concatenate([cos, cos, pad], axis=1)
    s1 = jnp.concatenate([-sin, z, pad], axis=1)
    s2 = jnp.concatenate([z, sin, pad], axis=1)
    return c, s1, s2


def _rope_rows_kernel(x_ref, c_ref, s1_ref, s2_ref, o_ref):
    o_ref[...] = _rope_slot(x_ref[...].astype(F32), c_ref[...], s1_ref[...], s2_ref[...]).astype(o_ref.dtype)


def rope_rows(x, col_block, tables, cfg):
    m = x.shape[0]
    tm = min(cfg.lin_tm, m)
    row = pl.BlockSpec((tm, LANE), lambda i: (i, 0))
    return pl.pallas_call(
        _rope_rows_kernel,
        grid=(m // tm,),
        in_specs=[pl.BlockSpec((tm, LANE), lambda i: (i, col_block)), row, row, row],
        out_specs=row,
        out_shape=jax.ShapeDtypeStruct((m, LANE), BF16),
        compiler_params=_cparams(("arbitrary",), 10 * tm * LANE * 4),
        name="mla_rope_key",
    )(x, *tables)


def _mla_attn_kernel(qi_ref, ki_ref, kind_ref, q_ref, kv_ref, kr_ref, c_ref, s1_ref, s2_ref, pq_ref, pk_ref, o_ref,
                     qs_ref, kc_ref, m_ref, l_ref, acc_ref, *, t, hb):
    pair = pl.program_id(1)
    qi, ki, kind = qi_ref[pair], ki_ref[pair], kind_ref[pair]
    w = 2 * LANE

    @pl.when(ki == 0)
    def _():
        m_ref[...] = jnp.full(m_ref.shape, -jnp.inf, F32)
        l_ref[...] = jnp.zeros(l_ref.shape, F32)
        acc_ref[...] = jnp.zeros(acc_ref.shape, F32)
        for h in range(hb):
            qs_ref[h, :, :LANE] = q_ref[:, h * w:h * w + LANE]
            qr = _rope_slot(q_ref[:, h * w + LANE:(h + 1) * w].astype(F32), c_ref[...], s1_ref[...], s2_ref[...])
            qs_ref[h, :, LANE:] = qr.astype(BF16)

    def step(adjust):
        for h in range(hb):
            kc_ref[h, :, :LANE] = kv_ref[:, h * w:h * w + LANE]
            kc_ref[h, :, LANE:] = kr_ref[...]
            s = lax.dot_general(qs_ref[h], kc_ref[h], (((1,), (1,)), ((), ())), preferred_element_type=F32)
            _online_softmax_step(adjust(s), kv_ref[:, h * w + LANE:(h + 1) * w], m_ref.at[h], l_ref.at[h], acc_ref.at[h])

    @pl.when(kind == KIND_PLAIN)
    def _():
        step(lambda s: s)

    @pl.when(kind == KIND_DIAG)
    def _():
        causal = lax.broadcasted_iota(I32, (t, t), 1) <= lax.broadcasted_iota(I32, (t, t), 0)
        step(lambda s: jnp.where(causal, s, NEG_INF))

    @pl.when(kind == KIND_GENERAL)
    def _():
        rel = pq_ref[...] - pk_ref[...]
        struct = _structural_ok(qi, ki, t)

        def adjust(s):
            s = jnp.where(rel >= 0, s, NEG_INF)
            return jnp.where(struct, s, -jnp.inf)
        step(adjust)

    @pl.when(ki == qi)
    def _():
        for h in range(hb):
            l = jnp.sum(l_ref[h], axis=-1, keepdims=True)
            o_ref[:, h * LANE:(h + 1) * LANE] = (acc_ref[h] / l).astype(o_ref.dtype)


def mla_attention(q_cat, kv, kr_rot, tables, kinds2d, pos, cfg):
    s = q_cat.shape[0]
    t = min(cfg.mla_t, s)
    nq = s // t
    heads, hb = cfg.mla_heads, cfg.mla_hb
    assert heads % hb == 0
    w = 2 * LANE
    qi_tab, ki_tab, kind_tab = _attn_pairs(kinds2d, nq)
    qrow = pl.BlockSpec((t, LANE), lambda g, p, qt, kt, kd: (qt[p], 0))
    grid_spec = pltpu.PrefetchScalarGridSpec(
        num_scalar_prefetch=3,
        grid=(heads // hb, qi_tab.shape[0]),
        in_specs=[
            pl.BlockSpec((t, hb * w), lambda g, p, qt, kt, kd: (qt[p], g)),
            pl.BlockSpec((t, hb * w), lambda g, p, qt, kt, kd: (kt[p], g)),
            pl.BlockSpec((t, LANE), lambda g, p, qt, kt, kd: (kt[p], 0)),
            qrow, qrow, qrow,
            pl.BlockSpec((t, 1), lambda g, p, qt, kt, kd: (qt[p], 0)),
            pl.BlockSpec((1, t), lambda g, p, qt, kt, kd: (0, kt[p])),
        ],
        out_specs=pl.BlockSpec((t, hb * LANE), lambda g, p, qt, kt, kd: (qt[p], g)),
        scratch_shapes=[pltpu.VMEM((hb, t, w), BF16), pltpu.VMEM((hb, t, w), BF16),
                        pltpu.VMEM((hb, t, LANE), F32), pltpu.VMEM((hb, t, LANE), F32), pltpu.VMEM((hb, t, LANE), F32)],
    )
    est = hb * (2 * (2 * t * w * 2 + t * LANE * 2) + 2 * t * w * 2 + 3 * t * LANE * 4) + 8 * t * LANE * 4 + 10 * t * t * 4
    return pl.pallas_call(
        functools.partial(_mla_attn_kernel, t=t, hb=hb),
        grid_spec=grid_spec,
        out_shape=jax.ShapeDtypeStruct((s, heads * cfg.mla_v), BF16),
        compiler_params=_cparams(("arbitrary", "arbitrary"), est),
        name="mla_attention",
    )(qi_tab, ki_tab, kind_tab, q_cat, kv, kr_rot, *tables, pos.reshape(s, 1), pos.reshape(1, s))


def _router_kernel(h_ref, g_ref, sc_ref, sh_ref, wr_ref, br_ref, u_ref, idx_ref, gate_ref, *, top_k):
    u = _rmsnorm_f32(h_ref[...], g_ref[...]) * (1.0 + sc_ref[...]) + sh_ref[...]
    half = u.shape[-1] // 2
    ub = u.astype(BF16).astype(F32)
    lo = lax.shift_right_logical(lax.bitcast_convert_type(ub[:, :half], I32), 16)
    hi = lax.bitcast_convert_type(ub[:, half:], I32) & (-65536)
    u_ref[...] = lo | hi
    w = wr_ref[...]
    u_hi, w_hi = u.astype(BF16), w.astype(BF16)
    u_lo = (u - u_hi.astype(F32)).astype(BF16)
    w_lo = (w - w_hi.astype(F32)).astype(BF16)
    n_e = w.shape[-1]
    both = jnp.dot(u_hi, jnp.concatenate([w_hi, w_lo], axis=1), preferred_element_type=F32)
    logits = both[:, :n_e] + both[:, n_e:] + jnp.dot(u_lo, w_hi, preferred_element_type=F32) + br_ref[...]
    lane = lax.broadcasted_iota(I32, logits.shape, 1)
    vals, idxs = [], []
    cur = logits
    for _ in range(top_k):
        mx = jnp.max(cur, axis=-1, keepdims=True)
        ix = jnp.min(jnp.where(cur == mx, lane, n_e), axis=-1, keepdims=True)
        vals.append(mx)
        idxs.append(ix)
        cur = jnp.where(lane == ix, -jnp.inf, cur)
    v = jnp.concatenate(vals, axis=1)
    e = jnp.exp(v - vals[0])
    gate_ref[...] = e / jnp.sum(e, axis=-1, keepdims=True)
    idx_ref[...] = jnp.concatenate(idxs, axis=1)


def router(h, g, sc, sh, w_router, b_router, cfg):
    t, d = h.shape
    e = w_router.shape[1]
    tm = min(cfg.router_tm, t)
    vec = pl.BlockSpec((1, d), lambda i: (0, 0))
    return pl.pallas_call(
        functools.partial(_router_kernel, top_k=cfg.top_k),
        grid=(t // tm,),
        in_specs=[pl.BlockSpec((tm, d), lambda i: (i, 0)), vec, vec, vec,
                  pl.BlockSpec((d, e), lambda i: (0, 0)), pl.BlockSpec((1, e), lambda i: (0, 0))],
        out_specs=[pl.BlockSpec((tm, d // 2), lambda i: (i, 0)),
                   pl.BlockSpec((tm, cfg.top_k), lambda i: (i, 0)),
                   pl.BlockSpec((tm, cfg.top_k), lambda i: (i, 0))],
        out_shape=[jax.ShapeDtypeStruct((t, d // 2), I32),
                   jax.ShapeDtypeStruct((t, cfg.top_k), I32),
                   jax.ShapeDtypeStruct((t, cfg.top_k), F32)],
        compiler_params=_cparams(("arbitrary",), 6 * tm * d * 4 + 2 * d * LANE * 4),
        name="moe_router",
    )(h, g, sc, sh, w_router, b_router)


def _routing_tables(idx, cfg):
    t, k = idx.shape
    a = t * k
    e = cfg.n_experts
    r = cfg.moe_rows
    p_rows = a + e * MOE_BLOCK
    g_max = e + a // r + 1
    e_flat = idx.reshape(a)
    onehot = (e_flat[:, None] == jnp.arange(e, dtype=I32)[None, :]).astype(I32)
    csum = jnp.cumsum(onehot, axis=0)
    counts = csum[-1]
    rank = jnp.sum(onehot * csum, axis=1) - 1
    padded = ((counts + MOE_BLOCK - 1) // MOE_BLOCK) * MOE_BLOCK
    pad_start = jnp.cumsum(padded) - padded
    pos = (pad_start[e_flat] + rank).astype(I32)
    order = jnp.argsort(e_flat, stable=True).astype(I32)
    rows_i = jnp.arange(p_rows + 2 * r, dtype=I32)[:, None]
    past = (rows_i >= (pad_start + padded)[None, :]).astype(I32)
    inside = jnp.concatenate([jnp.ones_like(past[:, :1]), past[:, :-1]], axis=1) - past
    r_row = rows_i[:, 0] - jnp.sum(past * padded[None, :], axis=1)
    src = jnp.clip(jnp.sum(past * counts[None, :], axis=1) + r_row, 0, a - 1)
    valid = r_row < jnp.sum(inside * counts[None, :], axis=1)
    tok_sorted = jnp.where(valid, order[src] // k, 0).astype(I32)
    n_grp = (padded + r - 1) // r
    cum = jnp.cumsum(n_grp)
    n_groups = cum[-1]
    gid = jnp.arange(g_max, dtype=I32)
    last = jnp.maximum(n_groups - 1, 0)
    gid_c = jnp.minimum(gid, last)
    g_exp = jnp.minimum(jnp.searchsorted(cum, gid_c, side="right"), e - 1).astype(I32)
    local = gid_c - (cum - n_grp)[g_exp]
    g_row = (pad_start[g_exp] + local * r).astype(I32)
    g_n = jnp.clip(padded[g_exp] - local * r, 0, r).astype(I32)
    g_n = jnp.where(gid < n_groups, g_n, 0)
    return tok_sorted, pos, g_exp, g_row, g_n, n_groups.astype(I32), p_rows


MOE_ARM_BLOCKS = (8, 4, 2, 1)
MOE_GATHER_SHARE = 4


def _moe_kernel(tok_ref, gexp_ref, grow_ref, gn_ref,
                u_hbm, wg_ref, wu_ref, bg_ref, bu_ref, wd_ref, bd_ref, ys_hbm,
                xg_ref, xb_ref, yst_ref, cnt_ref, gsem, osem):
    del gexp_ref
    g, c = pl.program_id(0), pl.program_id(1)
    ng, nc = pl.num_programs(0), pl.num_programs(1)
    n = gn_ref[g]
    row0 = grow_ref[g]
    has_next = g + 1 < ng
    nxt = jnp.minimum(g + 1, gn_ref.shape[0] - 1)
    n_next = jnp.where(has_next, gn_ref[nxt], 0)
    base_next = grow_ref[nxt]

    def gather_row(base, idx):
        tok = tok_ref[base + idx]
        pltpu.make_async_copy(u_hbm.at[pl.ds(tok, 1)], xg_ref.at[pl.ds(idx, 1)], gsem).start()

    def wait_rows(ref, rows, sem):
        rows = pl.multiple_of(rows, MOE_BLOCK)
        pltpu.make_async_copy(ref.at[pl.ds(0, rows)], ref.at[pl.ds(0, rows)], sem).wait()

    @pl.when(c == 0)
    def _():
        @pl.when(g == 0)
        def _():
            def first(i, carry):
                gather_row(row0, i)
                return carry
            lax.fori_loop(0, n, first, 0)
        wait_rows(xg_ref, n, gsem)

        def cast(i, carry):
            r0 = pl.multiple_of(i * MOE_BLOCK, MOE_BLOCK)
            words = xg_ref[pl.ds(r0, MOE_BLOCK), :]
            half = words.shape[-1]
            lo = lax.bitcast_convert_type(lax.shift_left(words, 16), F32)
            hi = lax.bitcast_convert_type(words & (-65536), F32)
            xb_ref[pl.ds(r0, MOE_BLOCK), :half] = lo.astype(BF16)
            xb_ref[pl.ds(r0, MOE_BLOCK), half:] = hi.astype(BF16)
            return carry
        lax.fori_loop(0, n // MOE_BLOCK, cast, 0)

        @pl.when(g > 0)
        def _():
            wait_rows(yst_ref, gn_ref[jnp.maximum(g - 1, 0)], osem)

        def init(i, carry):
            r0 = pl.multiple_of(i * MOE_BLOCK, MOE_BLOCK)
            yst_ref[pl.ds(r0, MOE_BLOCK), :] = jnp.broadcast_to(bd_ref[...], (MOE_BLOCK, bd_ref.shape[-1]))
            return carry
        lax.fori_loop(0, n // MOE_BLOCK, init, 0)
        cnt_ref[0] = 0

    def arm(r0, m, n_gather):
        x = xb_ref[pl.ds(r0, m), :]
        hg = jnp.dot(x, wg_ref[...].astype(BF16), preferred_element_type=F32)
        hu = jnp.dot(x, wu_ref[...].astype(BF16), preferred_element_type=F32)
        gate = jnp.minimum(hg + bg_ref[...], SWIGLU_LIMIT)
        up = jnp.clip(hu + bu_ref[...], -SWIGLU_LIMIT, SWIGLU_LIMIT)
        act = ((up + 1.0) * (gate * jax.nn.sigmoid(gate * SWIGLU_ALPHA))).astype(BF16)
        yst_ref[pl.ds(r0, m), :] += jnp.dot(act, wd_ref[...].astype(BF16), preferred_element_type=F32)
        cnt = cnt_ref[0]
        for i in range(n_gather):
            @pl.when(cnt + i < n_next)
            def _():
                gather_row(base_next, cnt + i)
        cnt_ref[0] = cnt + n_gather

        @pl.when(c == nc - 1)
        def _():
            dst0 = pl.multiple_of(row0 + r0, MOE_BLOCK)
            pltpu.make_async_copy(yst_ref.at[pl.ds(r0, m)], ys_hbm.at[pl.ds(dst0, m)], osem).start()

    nb = n // MOE_BLOCK
    top = MOE_ARM_BLOCKS[0]
    top_rows = top * MOE_BLOCK
    rows_cap = xg_ref.shape[0]

    @pl.when(nb >= top)
    def _():
        arm(0, top_rows, -(-rows_cap // nc))
        for blocks in range(1, rows_cap // MOE_BLOCK - top + 1):
            @pl.when(nb - top == blocks)
            def _():
                arm(top_rows, blocks * MOE_BLOCK, 0)

    @pl.when(nb < top)
    def _():
        for blocks in MOE_ARM_BLOCKS[1:]:
            @pl.when((nb & blocks) != 0)
            def _():
                done = (nb // (2 * blocks)) * (2 * blocks)
                arm(pl.multiple_of(done * MOE_BLOCK, blocks * MOE_BLOCK), blocks * MOE_BLOCK,
                    blocks * MOE_BLOCK // MOE_GATHER_SHARE)

    @pl.when(c == nc - 1)
    def _():
        def rest(i, carry):
            gather_row(base_next, i)
            return carry
        lax.fori_loop(jnp.minimum(cnt_ref[0], n_next), n_next, rest, 0)

        @pl.when(g == ng - 1)
        def _():
            wait_rows(yst_ref, n, osem)
            yst_ref[pl.ds(0, MOE_BLOCK), :] = jnp.zeros((MOE_BLOCK, yst_ref.shape[-1]), F32)
            first = (row0 + n) // MOE_BLOCK
            n_fill = ys_hbm.shape[0] // MOE_BLOCK - first

            def fill(i, carry):
                dst0 = pl.multiple_of((first + i) * MOE_BLOCK, MOE_BLOCK)
                pltpu.make_async_copy(yst_ref.at[pl.ds(0, MOE_BLOCK)], ys_hbm.at[pl.ds(dst0, MOE_BLOCK)], osem).start()
                return carry
            lax.fori_loop(0, n_fill, fill, 0)

            def drain(i, carry):
                pltpu.make_async_copy(yst_ref.at[pl.ds(0, MOE_BLOCK)], yst_ref.at[pl.ds(0, MOE_BLOCK)], osem).wait()
                return carry
            lax.fori_loop(0, n_fill, drain, 0)


def moe_experts(u, tok_sorted, g_exp, g_row, g_n, n_groups, layer, w_gu, b_gu, w_down, b_down, p_rows, cfg):
    n_layers, e, d, f2 = w_gu.shape
    assert u.shape[1] * 2 == d and u.dtype == I32
    f = f2 // 2
    tf = min(cfg.moe_tf, f)
    nc = f // tf
    rows = cfg.moe_rows
    assert rows % MOE_BLOCK == 0

    grid_spec = pltpu.PrefetchScalarGridSpec(
        num_scalar_prefetch=4,
        grid=(n_groups, nc),
        in_specs=[
            pl.BlockSpec(memory_space=pl.ANY),
            pl.BlockSpec((None, None, d, tf), lambda g, c, tk, ge, gr, gn: (layer, ge[g], 0, c)),
            pl.BlockSpec((None, None, d, tf), lambda g, c, tk, ge, gr, gn: (layer, ge[g], 0, nc + c)),
            pl.BlockSpec((None, None, 1, tf), lambda g, c, tk, ge, gr, gn: (layer, ge[g], 0, c)),
            pl.BlockSpec((None, None, 1, tf), lambda g, c, tk, ge, gr, gn: (layer, ge[g], 0, nc + c)),
            pl.BlockSpec((None, None, tf, d), lambda g, c, tk, ge, gr, gn: (layer, ge[g], c, 0)),
            pl.BlockSpec((None, None, 1, d), lambda g, c, tk, ge, gr, gn: (layer, ge[g], 0, 0)),
        ],
        out_specs=pl.BlockSpec(memory_space=pl.ANY),
        scratch_shapes=[
            pltpu.VMEM((rows, d // 2), I32), pltpu.VMEM((rows, d), BF16), pltpu.VMEM((rows, d), F32),
            pltpu.SMEM((1,), I32), pltpu.SemaphoreType.DMA(()), pltpu.SemaphoreType.DMA(()),
        ],
    )
    est = rows * d * 8 + 2 * 3 * d * tf * 4 + 3 * d * tf * 2
    return pl.pallas_call(
        _moe_kernel,
        grid_spec=grid_spec,
        out_shape=jax.ShapeDtypeStruct((p_rows, d), F32),
        compiler_params=_cparams(("arbitrary", "arbitrary"), est),
        name="moe_experts",
    )(tok_sorted, g_exp, g_row, g_n,
      u, w_gu, w_gu, b_gu.reshape(n_layers, e, 1, f2), b_gu.reshape(n_layers, e, 1, f2), w_down,
      b_down.reshape(n_layers, e, 1, d))


def _combine_kernel(pos_ref, ys_hbm, h_ref, gate_ref, g2_ref, fg_ref, o_ref, rows_ref, sem, *, tm, top_k, final_norm):
    i = pl.program_id(0)
    n_tiles = pl.num_programs(0)
    n_rows = tm * top_k

    def start_tile(tile, slot, cond):
        base = tile * n_rows
        for j in range(n_rows):
            def issue(j=j):
                src = pos_ref[base + j]
                pltpu.make_async_copy(ys_hbm.at[pl.ds(src, 1)], rows_ref.at[slot, pl.ds(j, 1)], sem.at[slot]).start()
            if cond is None:
                issue()
            else:
                pl.when(cond)(issue)

    def finish(slot):
        pltpu.make_async_copy(rows_ref.at[slot], rows_ref.at[slot], sem.at[slot]).wait()
        gates = gate_ref[...]
        acc = rows_ref[slot, pl.ds(0, tm), :] * gates[:, 0:1]
        for k in range(1, top_k):
            acc = acc + rows_ref[slot, pl.ds(k * tm, tm), :] * gates[:, k:k + 1]
        out = h_ref[...] + g2_ref[...] * acc
        if final_norm:
            out = _rmsnorm_f32(out, fg_ref[...])
        o_ref[...] = out

    @pl.when(i == 0)
    def _():
        start_tile(0, 0, None)

    for parity in range(2):
        @pl.when(i % 2 == parity)
        def _():
            start_tile(i + 1, 1 - parity, i + 1 < n_tiles)
            finish(parity)


def moe_combine(ys, pos, h, gates, g2, final_g, cfg, *, final_norm):
    t, d = h.shape
    top_k = cfg.top_k
    tm = min(cfg.comb_tm, t)
    assert (tm * top_k) % 64 == 0
    pos = pos.reshape(t // tm, tm, top_k).transpose(0, 2, 1).reshape(-1)
    vec = pl.BlockSpec((1, d), lambda i, p: (0, 0))
    grid_spec = pltpu.PrefetchScalarGridSpec(
        num_scalar_prefetch=1,
        grid=(t // tm,),
        in_specs=[pl.BlockSpec(memory_space=pl.ANY),
                  pl.BlockSpec((tm, d), lambda i, p: (i, 0)),
                  pl.BlockSpec((tm, top_k), lambda i, p: (i, 0)), vec, vec],
        out_specs=pl.BlockSpec((tm, d), lambda i, p: (i, 0)),
        scratch_shapes=[pltpu.VMEM((2, tm * top_k, d), F32), pltpu.SemaphoreType.DMA((2,))],
    )
    return pl.pallas_call(
        functools.partial(_combine_kernel, tm=tm, top_k=top_k, final_norm=final_norm),
        grid_spec=grid_spec,
        out_shape=jax.ShapeDtypeStruct((t, d), F32),
        compiler_params=_cparams(("arbitrary",), 2 * tm * top_k * d * 4 + 6 * tm * d * 4),
        name="moe_combine",
    )(pos, ys, h, gates, g2, final_g)


def moe_layer(h, g, sc, sh, g2, w_router, b_router, layer, w_gu, b_gu, w_down, b_down, final_g, cfg, *, final_norm):
    u, idx, gates = router(h, g, sc, sh, w_router, b_router.reshape(1, -1), cfg)
    tok_sorted, pos, g_exp, g_row, g_n, n_groups, p_rows = _routing_tables(idx, cfg)
    ys = moe_experts(u, tok_sorted, g_exp, g_row, g_n, n_groups, layer, w_gu, b_gu, w_down, b_down, p_rows, cfg)
    return moe_combine(ys, pos, h, gates, g2, final_g, cfg, final_norm=final_norm)


def _diff_mixer(h, pos, g, sc, sh, gate, w_qkv, lq1, lk1, lq2, lk2, sub_g, w_o, rel_bias, layer_idx, cfg):
    heads, hd = cfg.diff_heads, cfg.diff_head_dim
    qk_w = heads * 2 * hd
    lam_init = 0.8 - 0.6 * math.exp(-0.3 * (layer_idx - 1))
    lam = (jnp.exp(jnp.sum(lq1 * lk1)) - jnp.exp(jnp.sum(lq2 * lk2)) + lam_init).reshape(1).astype(F32)
    n = w_qkv.shape[1]
    col_scale = jnp.concatenate([jnp.full((qk_w,), hd ** -0.5 * LOG2E, F32), jnp.ones((n - qk_w,), F32)]).reshape(1, n)
    qkv = norm_linear(h, 0, h.shape[1], g, sc, sh, w_qkv.astype(BF16), col_scale, BF16, cfg,
                      modulate=True, name="diff_qkv_proj")
    t = min(cfg.diff_t, h.shape[0])
    tiles = diff_bias_tiles(rel_bias, heads, t)
    kinds = _block_kinds(pos, t, REL_FAR, toeplitz=True)
    o = diff_attention(qkv, tiles, kinds, pos, rel_bias, lam, sub_g.reshape(1, -1), 1.0 - lam_init, cfg)
    return linear_residual(o, w_o.astype(BF16), h, gate, cfg, name="diff_out_proj")


def _mla_mixer(h, pos, g, sc, sh, gate, w_in, q_norm_g, kv_norm_g, w_uq, w_ukv, w_o, cfg):
    d = h.shape[1]
    heads, nope, rope, qr, kvr = cfg.mla_heads, cfg.mla_nope, cfg.mla_rope, cfg.mla_q_rank, cfg.mla_kv_rank
    assert nope == LANE and cfg.mla_v == LANE and rope <= LANE and qr % LANE == 0 and kvr == qr
    w_in_p = jnp.concatenate([w_in, jnp.zeros((d, LANE - rope), w_in.dtype)], axis=1).astype(BF16)
    ones = lambda n: jnp.ones((1, n), F32)
    z = norm_linear(h, 0, d, g, sc, sh, w_in_p, ones(w_in_p.shape[1]), F32, cfg, modulate=True, name="mla_down_proj")
    w_q = w_uq.reshape(qr, heads, nope + rope)
    w_q = jnp.concatenate([w_q, jnp.zeros((qr, heads, 2 * LANE - nope - rope), w_uq.dtype)], axis=2)
    w_q = w_q.reshape(qr, heads * 2 * LANE).astype(BF16)
    zeros_k = jnp.zeros((1, qr), F32)
    q_scale = jnp.full((1, heads * 2 * LANE), (nope + rope) ** -0.5 * LOG2E, F32)
    q_cat = norm_linear(z, 0, qr, q_norm_g.reshape(1, -1), zeros_k, zeros_k, w_q, q_scale, BF16, cfg,
                        modulate=False, name="mla_q_up_proj")
    kv = norm_linear(z, 1, kvr, kv_norm_g.reshape(1, -1), zeros_k, zeros_k, w_ukv.astype(BF16),
                     ones(w_ukv.shape[1]), BF16, cfg, modulate=False, name="mla_kv_up_proj")
    tables = _rope_tables(pos, rope)
    kr_rot = rope_rows(z, (qr + kvr) // LANE, tables, cfg)
    t = min(cfg.mla_t, h.shape[0])
    kinds = _block_kinds(pos, t, 0, toeplitz=True)
    o = mla_attention(q_cat, kv, kr_rot, tables, kinds, pos, cfg)
    return linear_residual(o, w_o.astype(BF16), h, gate, cfg, name="mla_out_proj")


def _forward(cfg, x, c, positions, ada_w, ada_b, norm1_g, norm2_g, final_g, rel_bias,
             diff_w_qkv, diff_lq1, diff_lk1, diff_lq2, diff_lk2, diff_sub_g, diff_w_o,
             mla_w_in, mla_q_norm_g, mla_kv_norm_g, mla_w_uq, mla_w_ukv, mla_w_o,
             router_w, router_b, exp_w_gu, exp_b_gu, exp_w_down, exp_b_down):
    b, s, d = x.shape
    assert b == 1, "kernels are written for a single sequence"
    h = x.reshape(s, d)
    pos = positions.reshape(s).astype(I32)
    mod = ada_modulation(c.reshape(d, 1), ada_w, ada_b, cfg)
    fg = final_g.reshape(1, d)
    for i in range(cfg.depth):
        sh1, sc1, g1, sh2, sc2, g2 = [mod[i, :, j * d:(j + 1) * d] for j in range(6)]
        n1 = norm1_g[i].reshape(1, d)
        j = i // 2
        if i % 2 == 0:
            h = _diff_mixer(h, pos, n1, sc1, sh1, g1, diff_w_qkv[j], diff_lq1[j], diff_lk1[j], diff_lq2[j],
                            diff_lk2[j], diff_sub_g[j], diff_w_o[j], rel_bias, i + 1, cfg)
        else:
            h = _mla_mixer(h, pos, n1, sc1, sh1, g1, mla_w_in[j], mla_q_norm_g[j], mla_kv_norm_g[j],
                           mla_w_uq[j], mla_w_ukv[j], mla_w_o[j], cfg)
        h = moe_layer(h, norm2_g[i].reshape(1, d), sc2, sh2, g2, router_w[i], router_b[i], i, exp_w_gu, exp_b_gu,
                      exp_w_down, exp_b_down, fg, cfg, final_norm=(i == cfg.depth - 1))
    return h.reshape(b, s, d)


def kernel(x, c, positions, ada_w, ada_b, norm1_g, norm2_g, final_g, rel_bias, diff_w_qkv, diff_lq1, diff_lk1, diff_lq2, diff_lk2, diff_sub_g, diff_w_o, mla_w_in, mla_q_norm_g, mla_kv_norm_g, mla_w_uq, mla_w_ukv, mla_w_o, router_w, router_b, exp_w_gu, exp_b_gu, exp_w_down, exp_b_down):
    return _forward(Cfg(), x, c, positions, ada_w, ada_b, norm1_g, norm2_g, final_g, rel_bias,
                    diff_w_qkv, diff_lq1, diff_lk1, diff_lq2, diff_lk2, diff_sub_g, diff_w_o,
                    mla_w_in, mla_q_norm_g, mla_kv_norm_g, mla_w_uq, mla_w_ukv, mla_w_o,
                    router_w, router_b, exp_w_gu, exp_b_gu, exp_w_down, exp_b_down)
```

```python
import dataclasses
import functools
import math

import jax
import jax.numpy as jnp
import numpy as np
from jax import lax
from jax.experimental import pallas as pl
from jax.experimental.pallas import tpu as pltpu

F32 = jnp.float32
BF16 = jnp.bfloat16
I32 = jnp.int32

RMS_EPS = 1e-6
NEG_INF = -1e30
LOG2E = math.log2(math.e)
ROPE_THETA = 10000.0
REL_BUCKETS = 32
REL_MAX_DIST = 128
REL_FAR = 113
SWIGLU_LIMIT = 7.0
SWIGLU_ALPHA = 1.702
Q_BLOCK = 128
MOE_BLOCK = 128
LANE = 128
V7X_VMEM_BYTES = 64 * 1024 * 1024


@dataclasses.dataclass(frozen=True)
class Cfg:
    d_model: int = 2048
    seq: int = 8192
    depth: int = 2
    diff_heads: int = 8
    diff_head_dim: int = 128
    mla_heads: int = 16
    mla_q_rank: int = 512
    mla_kv_rank: int = 512
    mla_nope: int = 128
    mla_rope: int = 64
    mla_v: int = 128
    n_experts: int = 32
    top_k: int = 4
    d_ff: int = 2048
    ada_tn: int = 1024
    lin_tm: int = 1024
    lin_tn: int = 1024
    diff_t: int = 512
    diff_hb: int = 4
    mla_t: int = 1024
    mla_hb: int = 4
    router_tm: int = 512
    moe_rows: int = 1280
    moe_tf: int = 256
    comb_tm: int = 128


def _vmem_limit(nbytes):
    return int(min(nbytes * 1.25 + (6 << 20), V7X_VMEM_BYTES - (6 << 20)))


def _cparams(sem, nbytes):
    return pltpu.CompilerParams(dimension_semantics=sem, vmem_limit_bytes=_vmem_limit(nbytes))


def _ada_kernel(c_ref, w_ref, b_ref, o_ref):
    c = c_ref[...]
    cs = c * jax.nn.sigmoid(c)
    o_ref[...] = jnp.sum(w_ref[...] * cs, axis=0, keepdims=True) + b_ref[...]


def ada_modulation(c_col, ada_w, ada_b, cfg):
    depth, d, n = ada_w.shape
    tn = min(cfg.ada_tn, n)
    assert n % tn == 0
    return pl.pallas_call(
        _ada_kernel,
        grid=(depth, n // tn),
        in_specs=[
            pl.BlockSpec((d, 1), lambda l, j: (0, 0)),
            pl.BlockSpec((None, d, tn), lambda l, j: (l, 0, j)),
            pl.BlockSpec((None, 1, tn), lambda l, j: (l, 0, j)),
        ],
        out_specs=pl.BlockSpec((None, 1, tn), lambda l, j: (l, 0, j)),
        out_shape=jax.ShapeDtypeStruct((depth, 1, n), F32),
        compiler_params=_cparams(("arbitrary", "arbitrary"), 2 * d * tn * 4 + d * tn * 4),
        name="ada_modulation",
    )(c_col, ada_w, ada_b.reshape(depth, 1, n))


def _rmsnorm_f32(x, g):
    return x * lax.rsqrt(jnp.mean(x * x, axis=-1, keepdims=True) + RMS_EPS) * g


def _norm_linear_kernel(x_ref, g_ref, sc_ref, sh_ref, w_ref, cs_ref, o_ref, xn_ref, *, modulate):
    @pl.when(pl.program_id(1) == 0)
    def _():
        y = _rmsnorm_f32(x_ref[...].astype(F32), g_ref[...])
        if modulate:
            y = y * (1.0 + sc_ref[...]) + sh_ref[...]
        xn_ref[...] = y.astype(BF16)

    acc = jnp.dot(xn_ref[...], w_ref[...], preferred_element_type=F32)
    o_ref[...] = (acc * cs_ref[...]).astype(o_ref.dtype)


def norm_linear(x, x_col_block, k, g, sc, sh, w_bf, col_scale, out_dtype, cfg, *, modulate, name):
    m = x.shape[0]
    n = w_bf.shape[1]
    tm = min(cfg.lin_tm, m)
    tn = n if n <= cfg.lin_tn or n % cfg.lin_tn else cfg.lin_tn
    assert m % tm == 0 and n % tn == 0
    out_b = jnp.dtype(out_dtype).itemsize
    est = 2 * tm * k * 4 + tm * k * 2 + 2 * k * tn * 2 + 2 * tm * tn * out_b + tm * k * 4
    return pl.pallas_call(
        functools.partial(_norm_linear_kernel, modulate=modulate),
        grid=(m // tm, n // tn),
        in_specs=[
            pl.BlockSpec((tm, k), lambda i, j: (i, x_col_block)),
            pl.BlockSpec((1, k), lambda i, j: (0, 0)),
            pl.BlockSpec((1, k), lambda i, j: (0, 0)),
            pl.BlockSpec((1, k), lambda i, j: (0, 0)),
            pl.BlockSpec((k, tn), lambda i, j: (0, j)),
            pl.BlockSpec((1, tn), lambda i, j: (0, j)),
        ],
        out_specs=pl.BlockSpec((tm, tn), lambda i, j: (i, j)),
        out_shape=jax.ShapeDtypeStruct((m, n), out_dtype),
        scratch_shapes=[pltpu.VMEM((tm, k), BF16)],
        compiler_params=_cparams(("arbitrary", "arbitrary"), est),
        name=name,
    )(x, g, sc, sh, w_bf, col_scale)


def _linear_res_kernel(a_ref, w_ref, h_ref, g_ref, o_ref):
    acc = jnp.dot(a_ref[...], w_ref[...], preferred_element_type=F32)
    o_ref[...] = h_ref[...] + g_ref[...] * acc


def linear_residual(a_bf, w_bf, h, gate, cfg, *, name):
    m, k = a_bf.shape
    n = w_bf.shape[1]
    tm = min(cfg.lin_tm, m)
    tn = min(cfg.lin_tn, n)
    assert m % tm == 0 and n % tn == 0
    est = 2 * tm * k * 2 + 2 * k * tn * 2 + 4 * tm * tn * 4
    return pl.pallas_call(
        _linear_res_kernel,
        grid=(m // tm, n // tn),
        in_specs=[
            pl.BlockSpec((tm, k), lambda i, j: (i, 0)),
            pl.BlockSpec((k, tn), lambda i, j: (0, j)),
            pl.BlockSpec((tm, tn), lambda i, j: (i, j)),
            pl.BlockSpec((1, tn), lambda i, j: (0, j)),
        ],
        out_specs=pl.BlockSpec((tm, tn), lambda i, j: (i, j)),
        out_shape=jax.ShapeDtypeStruct((m, n), F32),
        compiler_params=_cparams(("arbitrary", "arbitrary"), est),
        name=name,
    )(a_bf, w_bf, h, gate)


KIND_SKIP, KIND_PLAIN, KIND_DIAG, KIND_OFFDIAG, KIND_GENERAL = 0, 1, 2, 3, 4


def _block_kinds(pos, t, far_dist, toeplitz):
    s = pos.shape[0]
    nb = s // t
    pb = pos.reshape(nb, t)
    pmin, pmax = pb.min(axis=1), pb.max(axis=1)
    consecutive = jnp.all(pb == pb[:, :1] + jnp.arange(t, dtype=pos.dtype)[None, :], axis=1)
    qi = jnp.arange(nb)[:, None]
    ki = jnp.arange(nb)[None, :]
    gap = pmin[:, None] - pmax[None, :]
    plain = (ki < qi) & (gap >= (far_dist if far_dist else 0))
    kinds = jnp.where(plain, KIND_PLAIN, KIND_GENERAL)
    if toeplitz:
        both = consecutive[:, None] & consecutive[None, :]
        d = pb[:, 0][:, None] - pb[:, 0][None, :]
        kinds = jnp.where(~plain & both & (ki == qi - 1) & (d == t), KIND_OFFDIAG, kinds)
        kinds = jnp.where(both & (ki == qi) & (d == 0), KIND_DIAG, kinds)
    return jnp.where(ki > qi, KIND_SKIP, kinds).astype(I32)


def _attn_pairs(kinds2d, nq):
    qi = np.concatenate([np.full(q + 1, q, np.int32) for q in range(nq)])
    ki = np.concatenate([np.arange(q + 1, dtype=np.int32) for q in range(nq)])
    return jnp.asarray(qi), jnp.asarray(ki), kinds2d[qi, ki].astype(I32)


def _t5_bias_minus_last(rel, rb_ref, h):
    n = jnp.maximum(rel, 0)
    max_exact = REL_BUCKETS // 2
    nf = jnp.maximum(n, 1).astype(F32)
    large = max_exact + (jnp.log(nf / max_exact) / math.log(REL_MAX_DIST / max_exact)
                         * (REL_BUCKETS - max_exact)).astype(I32)
    large = jnp.minimum(large, REL_BUCKETS - 1)
    bucket = jnp.where(n < max_exact, n, large)
    last = rb_ref[REL_BUCKETS - 1, h]
    out = jnp.zeros(rel.shape, F32)
    for b in range(REL_BUCKETS - 1):
        out = jnp.where(bucket == b, rb_ref[b, h] - last, out)
    return out


def _bias_tile_kernel(rb_ref, o_ref, *, t):
    kind = pl.program_id(0)
    h = pl.program_id(1)
    rel = kind * t + lax.broadcasted_iota(I32, (t, t), 0) - lax.broadcasted_iota(I32, (t, t), 1)
    bias = _t5_bias_minus_last(rel, rb_ref, h) * LOG2E
    o_ref[...] = jnp.where(rel >= 0, bias, NEG_INF)


def diff_bias_tiles(rel_bias, heads, t):
    return pl.pallas_call(
        functools.partial(_bias_tile_kernel, t=t),
        grid=(2, heads),
        in_specs=[pl.BlockSpec(memory_space=pltpu.SMEM)],
        out_specs=pl.BlockSpec((None, None, t, t), lambda k, h: (k, h, 0, 0)),
        out_shape=jax.ShapeDtypeStruct((2, heads, t, t), F32),
        compiler_params=_cparams(("arbitrary", "arbitrary"), 8 * t * t * 4),
        name="diff_bias_tiles",
    )(rel_bias)


def _structural_ok(qi, ki, t):
    q_idx = qi * t + lax.broadcasted_iota(I32, (t, t), 0)
    k_idx = ki * t + lax.broadcasted_iota(I32, (t, t), 1)
    return k_idx < ((q_idx // Q_BLOCK) + 1) * Q_BLOCK


def _online_softmax_step(s, v, m_ref, l_ref, acc_ref):
    nl = s.shape[1] // LANE
    m_prev = m_ref[...]
    m_new = jnp.maximum(m_prev, jnp.max(s, axis=-1, keepdims=True))
    alpha = jnp.exp2(m_prev - m_new)
    p = jnp.exp2(s - jnp.tile(m_new, (1, nl)))
    psum = p[:, :LANE]
    for j in range(1, nl):
        psum = psum + p[:, j * LANE:(j + 1) * LANE]
    l_ref[...] = alpha * l_ref[...] + psum
    pv = jnp.dot(p.astype(BF16), v, preferred_element_type=F32)
    acc_ref[...] = jnp.tile(alpha, (1, v.shape[1] // LANE)) * acc_ref[...] + pv
    m_ref[...] = m_new


def _diff_attn_kernel(qi_ref, ki_ref, kind_ref, q_ref, k_ref, v_ref, bt_ref, pq_ref, pk_ref, rb_ref, lam_ref, sg_ref,
                      o_ref, m_ref, l_ref, acc_ref, *, t, hd, hb, out_scale):
    hg, pair = pl.program_id(0), pl.program_id(1)
    qi, ki, kind = qi_ref[pair], ki_ref[pair], kind_ref[pair]
    w = 2 * hd

    @pl.when(ki == 0)
    def _():
        m_ref[...] = jnp.full(m_ref.shape, -jnp.inf, F32)
        l_ref[...] = jnp.zeros(l_ref.shape, F32)
        acc_ref[...] = jnp.zeros(acc_ref.shape, F32)

    def step(adjust):
        for h in range(hb):
            v = v_ref[:, h * w:(h + 1) * w]
            for mp in range(2):
                c0 = h * w + mp * hd
                s = lax.dot_general(q_ref[:, c0:c0 + hd], k_ref[:, c0:c0 + hd], (((1,), (1,)), ((), ())),
                                    preferred_element_type=F32)
                j = 2 * h + mp
                _online_softmax_step(adjust(s, h), v, m_ref.at[j], l_ref.at[j], acc_ref.at[j])

    @pl.when(kind == KIND_PLAIN)
    def _():
        step(lambda s, h: s)

    @pl.when((kind == KIND_DIAG) | (kind == KIND_OFFDIAG))
    def _():
        step(lambda s, h: s + bt_ref[h])

    @pl.when(kind == KIND_GENERAL)
    def _():
        rel = pq_ref[...] - pk_ref[...]
        struct = _structural_ok(qi, ki, t)

        def adjust(s, h):
            add = _t5_bias_minus_last(rel, rb_ref, hg * hb + h) * LOG2E
            s = jnp.where(rel >= 0, s + add, NEG_INF)
            return jnp.where(struct, s, -jnp.inf)
        step(adjust)

    @pl.when(ki == qi)
    def _():
        for h in range(hb):
            l0 = jnp.sum(l_ref[2 * h], axis=-1, keepdims=True)
            l1 = jnp.sum(l_ref[2 * h + 1], axis=-1, keepdims=True)
            o = acc_ref[2 * h] / l0 - lam_ref[0] * (acc_ref[2 * h + 1] / l1)
            o_ref[:, h * w:(h + 1) * w] = (_rmsnorm_f32(o, sg_ref[...]) * out_scale).astype(o_ref.dtype)


def diff_attention(qkv, bias_tiles, kinds2d, pos, rel_bias, lam, sub_g, out_scale, cfg):
    s = qkv.shape[0]
    t = min(cfg.diff_t, s)
    nq = s // t
    heads, hd, hb = cfg.diff_heads, cfg.diff_head_dim, cfg.diff_hb
    assert heads % hb == 0
    ng = heads // hb
    w = 2 * hd
    qi_tab, ki_tab, kind_tab = _attn_pairs(kinds2d, nq)

    grid_spec = pltpu.PrefetchScalarGridSpec(
        num_scalar_prefetch=3,
        grid=(ng, qi_tab.shape[0]),
        in_specs=[
            pl.BlockSpec((t, hb * w), lambda g, p, qt, kt, kd: (qt[p], g)),
            pl.BlockSpec((t, hb * w), lambda g, p, qt, kt, kd: (kt[p], ng + g)),
            pl.BlockSpec((t, hb * w), lambda g, p, qt, kt, kd: (kt[p], 2 * ng + g)),
            pl.BlockSpec((None, hb, t, t), lambda g, p, qt, kt, kd: (jnp.where(kd[p] == KIND_OFFDIAG, 1, 0), g, 0, 0)),
            pl.BlockSpec((t, 1), lambda g, p, qt, kt, kd: (qt[p], 0)),
            pl.BlockSpec((1, t), lambda g, p, qt, kt, kd: (0, kt[p])),
            pl.BlockSpec(memory_space=pltpu.SMEM),
            pl.BlockSpec(memory_space=pltpu.SMEM),
            pl.BlockSpec((1, w), lambda g, p, qt, kt, kd: (0, 0)),
        ],
        out_specs=pl.BlockSpec((t, hb * w), lambda g, p, qt, kt, kd: (qt[p], g)),
        scratch_shapes=[pltpu.VMEM((2 * hb, t, LANE), F32), pltpu.VMEM((2 * hb, t, LANE), F32),
                        pltpu.VMEM((2 * hb, t, w), F32)],
    )
    est = hb * (2 * (4 * t * w * 2 + t * t * 4) + 2 * t * w * 4 + 4 * t * LANE * 4) + 10 * t * t * 4
    return pl.pallas_call(
        functools.partial(_diff_attn_kernel, t=t, hd=hd, hb=hb, out_scale=out_scale),
        grid_spec=grid_spec,
        out_shape=jax.ShapeDtypeStruct((s, heads * w), BF16),
        compiler_params=_cparams(("arbitrary", "arbitrary"), est),
        name="diff_attention",
    )(qi_tab, ki_tab, kind_tab, qkv, qkv, qkv, bias_tiles, pos.reshape(s, 1), pos.reshape(1, s), rel_bias, lam, sub_g)


def _rope_slot(x, c, s1, s2):
    return x * c + pltpu.roll(x, LANE - 32, 1) * s1 + pltpu.roll(x, 32, 1) * s2


def _rope_tables(pos, rope_dim):
    half = rope_dim // 2
    inv = ROPE_THETA ** (-jnp.arange(half, dtype=F32) / half)
    ang = pos.astype(F32)[:, None] * inv
    cos, sin = jnp.cos(ang), jnp.sin(ang)
    z = jnp.zeros_like(cos)
    pad = jnp.zeros((pos.shape[0], LANE - 2 * half), F32)
    c = jnp.concatenate([cos, cos, pad], axis=1)
    s1 = jnp.concatenate([-sin, z, pad], axis=1)
    s2 = jnp.concatenate([z, sin, pad], axis=1)
    return c, s1, s2


def _rope_rows_kernel(x_ref, c_ref, s1_ref, s2_ref, o_ref):
    o_ref[...] = _rope_slot(x_ref[...].astype(F32), c_ref[...], s1_ref[...], s2_ref[...]).astype(o_ref.dtype)


def rope_rows(x, col_block, tables, cfg):
    m = x.shape[0]
    tm = min(cfg.lin_tm, m)
    row = pl.BlockSpec((tm, LANE), lambda i: (i, 0))
    return pl.pallas_call(
        _rope_rows_kernel,
        grid=(m // tm,),
        in_specs=[pl.BlockSpec((tm, LANE), lambda i: (i, col_block)), row, row, row],
        out_specs=row,
        out_shape=jax.ShapeDtypeStruct((m, LANE), BF16),
        compiler_params=_cparams(("arbitrary",), 10 * tm * LANE * 4),
        name="mla_rope_key",
    )(x, *tables)


def _mla_attn_kernel(qi_ref, ki_ref, kind_ref, q_ref, kv_ref, kr_ref, c_ref, s1_ref, s2_ref, pq_ref, pk_ref, o_ref,
                     qs_ref, kc_ref, m_ref, l_ref, acc_ref, *, t, hb):
    pair = pl.program_id(1)
    qi, ki, kind = qi_ref[pair], ki_ref[pair], kind_ref[pair]
    w = 2 * LANE

    @pl.when(ki == 0)
    def _():
        m_ref[...] = jnp.full(m_ref.shape, -jnp.inf, F32)
        l_ref[...] = jnp.zeros(l_ref.shape, F32)
        acc_ref[...] = jnp.zeros(acc_ref.shape, F32)
        for h in range(hb):
            qs_ref[h, :, :LANE] = q_ref[:, h * w:h * w + LANE]
            qr = _rope_slot(q_ref[:, h * w + LANE:(h + 1) * w].astype(F32), c_ref[...], s1_ref[...], s2_ref[...])
            qs_ref[h, :, LANE:] = qr.astype(BF16)

    def step(adjust):
        for h in range(hb):
            kc_ref[h, :, :LANE] = kv_ref[:, h * w:h * w + LANE]
            kc_ref[h, :, LANE:] = kr_ref[...]
            s = lax.dot_general(qs_ref[h], kc_ref[h], (((1,), (1,)), ((), ())), preferred_element_type=F32)
            _online_softmax_step(adjust(s), kv_ref[:, h * w + LANE:(h + 1) * w], m_ref.at[h], l_ref.at[h], acc_ref.at[h])

    @pl.when(kind == KIND_PLAIN)
    def _():
        step(lambda s: s)

    @pl.when(kind == KIND_DIAG)
    def _():
        causal = lax.broadcasted_iota(I32, (t, t), 1) <= lax.broadcasted_iota(I32, (t, t), 0)
        step(lambda s: jnp.where(causal, s, NEG_INF))

    @pl.when(kind == KIND_GENERAL)
    def _():
        rel = pq_ref[...] - pk_ref[...]
        struct = _structural_ok(qi, ki, t)

        def adjust(s):
            s = jnp.where(rel >= 0, s, NEG_INF)
            return jnp.where(struct, s, -jnp.inf)
        step(adjust)

    @pl.when(ki == qi)
    def _():
        for h in range(hb):
            l = jnp.sum(l_ref[h], axis=-1, keepdims=True)
            o_ref[:, h * LANE:(h + 1) * LANE] = (acc_ref[h] / l).astype(o_ref.dtype)


def mla_attention(q_cat, kv, kr_rot, tables, kinds2d, pos, cfg):
    s = q_cat.shape[0]
    t = min(cfg.mla_t, s)
    nq = s // t
    heads, hb = cfg.mla_heads, cfg.mla_hb
    assert heads % hb == 0
    w = 2 * LANE
    qi_tab, ki_tab, kind_tab = _attn_pairs(kinds2d, nq)
    qrow = pl.BlockSpec((t, LANE), lambda g, p, qt, kt, kd: (qt[p], 0))
    grid_spec = pltpu.PrefetchScalarGridSpec(
        num_scalar_prefetch=3,
        grid=(heads // hb, qi_tab.shape[0]),
        in_specs=[
            pl.BlockSpec((t, hb * w), lambda g, p, qt, kt, kd: (qt[p], g)),
            pl.BlockSpec((t, hb * w), lambda g, p, qt, kt, kd: (kt[p], g)),
            pl.BlockSpec((t, LANE), lambda g, p, qt, kt, kd: (kt[p], 0)),
            qrow, qrow, qrow,
            pl.BlockSpec((t, 1), lambda g, p, qt, kt, kd: (qt[p], 0)),
            pl.BlockSpec((1, t), lambda g, p, qt, kt, kd: (0, kt[p])),
        ],
        out_specs=pl.BlockSpec((t, hb * LANE), lambda g, p, qt, kt, kd: (qt[p], g)),
        scratch_shapes=[pltpu.VMEM((hb, t, w), BF16), pltpu.VMEM((hb, t, w), BF16),
                        pltpu.VMEM((hb, t, LANE), F32), pltpu.VMEM((hb, t, LANE), F32), pltpu.VMEM((hb, t, LANE), F32)],
    )
    est = hb * (2 * (2 * t * w * 2 + t * LANE * 2) + 2 * t * w * 2 + 3 * t * LANE * 4) + 8 * t * LANE * 4 + 10 * t * t * 4
    return pl.pallas_call(
        functools.partial(_mla_attn_kernel, t=t, hb=hb),
        grid_spec=grid_spec,
        out_shape=jax.ShapeDtypeStruct((s, heads * cfg.mla_v), BF16),
        compiler_params=_cparams(("arbitrary", "arbitrary"), est),
        name="mla_attention",
    )(qi_tab, ki_tab, kind_tab, q_cat, kv, kr_rot, *tables, pos.reshape(s, 1), pos.reshape(1, s))


def _router_kernel(h_ref, g_ref, sc_ref, sh_ref, wr_ref, br_ref, u_ref, idx_ref, gate_ref, *, top_k):
    u = _rmsnorm_f32(h_ref[...], g_ref[...]) * (1.0 + sc_ref[...]) + sh_ref[...]
    half = u.shape[-1] // 2
    ub = u.astype(BF16).astype(F32)
    lo = lax.shift_right_logical(lax.bitcast_convert_type(ub[:, :half], I32), 16)
    hi = lax.bitcast_convert_type(ub[:, half:], I32) & (-65536)
    u_ref[...] = lo | hi
    w = wr_ref[...]
    u_hi, w_hi = u.astype(BF16), w.astype(BF16)
    u_lo = (u - u_hi.astype(F32)).astype(BF16)
    w_lo = (w - w_hi.astype(F32)).astype(BF16)
    n_e = w.shape[-1]
    both = jnp.dot(u_hi, jnp.concatenate([w_hi, w_lo], axis=1), preferred_element_type=F32)
    logits = both[:, :n_e] + both[:, n_e:] + jnp.dot(u_lo, w_hi, preferred_element_type=F32) + br_ref[...]
    lane = lax.broadcasted_iota(I32, logits.shape, 1)
    vals, idxs = [], []
    cur = logits
    for _ in range(top_k):
        mx = jnp.max(cur, axis=-1, keepdims=True)
        ix = jnp.min(jnp.where(cur == mx, lane, n_e), axis=-1, keepdims=True)
        vals.append(mx)
        idxs.append(ix)
        cur = jnp.where(lane == ix, -jnp.inf, cur)
    v = jnp.concatenate(vals, axis=1)
    e = jnp.exp(v - vals[0])
    gate_ref[...] = e / jnp.sum(e, axis=-1, keepdims=True)
    idx_ref[...] = jnp.concatenate(idxs, axis=1)


def router(h, g, sc, sh, w_router, b_router, cfg):
    t, d = h.shape
    e = w_router.shape[1]
    tm = min(cfg.router_tm, t)
    vec = pl.BlockSpec((1, d), lambda i: (0, 0))
    return pl.pallas_call(
        functools.partial(_router_kernel, top_k=cfg.top_k),
        grid=(t // tm,),
        in_specs=[pl.BlockSpec((tm, d), lambda i: (i, 0)), vec, vec, vec,
                  pl.BlockSpec((d, e), lambda i: (0, 0)), pl.BlockSpec((1, e), lambda i: (0, 0))],
        out_specs=[pl.BlockSpec((tm, d // 2), lambda i: (i, 0)),
                   pl.BlockSpec((tm, cfg.top_k), lambda i: (i, 0)),
                   pl.BlockSpec((tm, cfg.top_k), lambda i: (i, 0))],
        out_shape=[jax.ShapeDtypeStruct((t, d // 2), I32),
                   jax.ShapeDtypeStruct((t, cfg.top_k), I32),
                   jax.ShapeDtypeStruct((t, cfg.top_k), F32)],
        compiler_params=_cparams(("arbitrary",), 6 * tm * d * 4 + 2 * d * LANE * 4),
        name="moe_router",
    )(h, g, sc, sh, w_router, b_router)


def _routing_tables(idx, cfg):
    t, k = idx.shape
    a = t * k
    e = cfg.n_experts
    r = cfg.moe_rows
    p_rows = a + e * MOE_BLOCK
    g_max = e + a // r + 1
    e_flat = idx.reshape(a)
    onehot = (e_flat[:, None] == jnp.arange(e, dtype=I32)[None, :]).astype(I32)
    csum = jnp.cumsum(onehot, axis=0)
    counts = csum[-1]
    rank = jnp.sum(onehot * csum, axis=1) - 1
    padded = ((counts + MOE_BLOCK - 1) // MOE_BLOCK) * MOE_BLOCK
    pad_start = jnp.cumsum(padded) - padded
    pos = (pad_start[e_flat] + rank).astype(I32)
    order = jnp.argsort(e_flat, stable=True).astype(I32)
    rows_i = jnp.arange(p_rows + 2 * r, dtype=I32)[:, None]
    past = (rows_i >= (pad_start + padded)[None, :]).astype(I32)
    inside = jnp.concatenate([jnp.ones_like(past[:, :1]), past[:, :-1]], axis=1) - past
    r_row = rows_i[:, 0] - jnp.sum(past * padded[None, :], axis=1)
    src = jnp.clip(jnp.sum(past * counts[None, :], axis=1) + r_row, 0, a - 1)
    valid = r_row < jnp.sum(inside * counts[None, :], axis=1)
    tok_sorted = jnp.where(valid, order[src] // k, 0).astype(I32)
    n_grp = (padded + r - 1) // r
    cum = jnp.cumsum(n_grp)
    n_groups = cum[-1]
    gid = jnp.arange(g_max, dtype=I32)
    last = jnp.maximum(n_groups - 1, 0)
    gid_c = jnp.minimum(gid, last)
    g_exp = jnp.minimum(jnp.searchsorted(cum, gid_c, side="right"), e - 1).astype(I32)
    local = gid_c - (cum - n_grp)[g_exp]
    g_row = (pad_start[g_exp] + local * r).astype(I32)
    g_n = jnp.clip(padded[g_exp] - local * r, 0, r).astype(I32)
    g_n = jnp.where(gid < n_groups, g_n, 0)
    return tok_sorted, pos, g_exp, g_row, g_n, n_groups.astype(I32), p_rows


MOE_ARM_BLOCKS = (8, 4, 2, 1)
MOE_GATHER_SHARE = 4


def _moe_kernel(tok_ref, gexp_ref, grow_ref, gn_ref,
                u_hbm, wg_ref, wu_ref, bg_ref, bu_ref, wd_ref, bd_ref, ys_hbm,
                xg_ref, xb_ref, yst_ref, cnt_ref, gsem, osem):
    del gexp_ref
    g, c = pl.program_id(0), pl.program_id(1)
    ng, nc = pl.num_programs(0), pl.num_programs(1)
    n = gn_ref[g]
    row0 = grow_ref[g]
    has_next = g + 1 < ng
    nxt = jnp.minimum(g + 1, gn_ref.shape[0] - 1)
    n_next = jnp.where(has_next, gn_ref[nxt], 0)
    base_next = grow_ref[nxt]

    def gather_row(base, idx):
        tok = tok_ref[base + idx]
        pltpu.make_async_copy(u_hbm.at[pl.ds(tok, 1)], xg_ref.at[pl.ds(idx, 1)], gsem).start()

    def wait_rows(ref, rows, sem):
        rows = pl.multiple_of(rows, MOE_BLOCK)
        pltpu.make_async_copy(ref.at[pl.ds(0, rows)], ref.at[pl.ds(0, rows)], sem).wait()

    @pl.when(c == 0)
    def _():
        @pl.when(g == 0)
        def _():
            def first(i, carry):
                gather_row(row0, i)
                return carry
            lax.fori_loop(0, n, first, 0)
        wait_rows(xg_ref, n, gsem)

        def cast(i, carry):
            r0 = pl.multiple_of(i * MOE_BLOCK, MOE_BLOCK)
            words = xg_ref[pl.ds(r0, MOE_BLOCK), :]
            half = words.shape[-1]
            lo = lax.bitcast_convert_type(lax.shift_left(words, 16), F32)
            hi = lax.bitcast_convert_type(words & (-65536), F32)
            xb_ref[pl.ds(r0, MOE_BLOCK), :half] = lo.astype(BF16)
            xb_ref[pl.ds(r0, MOE_BLOCK), half:] = hi.astype(BF16)
            return carry
        lax.fori_loop(0, n // MOE_BLOCK, cast, 0)

        @pl.when(g > 0)
        def _():
            wait_rows(yst_ref, gn_ref[jnp.maximum(g - 1, 0)], osem)

        def init(i, carry):
            r0 = pl.multiple_of(i * MOE_BLOCK, MOE_BLOCK)
            yst_ref[pl.ds(r0, MOE_BLOCK), :] = jnp.broadcast_to(bd_ref[...], (MOE_BLOCK, bd_ref.shape[-1]))
            return carry
        lax.fori_loop(0, n // MOE_BLOCK, init, 0)
        cnt_ref[0] = 0

    def arm(r0, m, n_gather):
        x = xb_ref[pl.ds(r0, m), :]
        hg = jnp.dot(x, wg_ref[...].astype(BF16), preferred_element_type=F32)
        hu = jnp.dot(x, wu_ref[...].astype(BF16), preferred_element_type=F32)
        gate = jnp.minimum(hg + bg_ref[...], SWIGLU_LIMIT)
        up = jnp.clip(hu + bu_ref[...], -SWIGLU_LIMIT, SWIGLU_LIMIT)
        act = ((up + 1.0) * (gate * jax.nn.sigmoid(gate * SWIGLU_ALPHA))).astype(BF16)
        yst_ref[pl.ds(r0, m), :] += jnp.dot(act, wd_ref[...].astype(BF16), preferred_element_type=F32)
        cnt = cnt_ref[0]
        for i in range(n_gather):
            @pl.when(cnt + i < n_next)
            def _():
                gather_row(base_next, cnt + i)
        cnt_ref[0] = cnt + n_gather

        @pl.when(c == nc - 1)
        def _():
            dst0 = pl.multiple_of(row0 + r0, MOE_BLOCK)
            pltpu.make_async_copy(yst_ref.at[pl.ds(r0, m)], ys_hbm.at[pl.ds(dst0, m)], osem).start()

    nb = n // MOE_BLOCK
    top = MOE_ARM_BLOCKS[0]
    top_rows = top * MOE_BLOCK
    rows_cap = xg_ref.shape[0]

    @pl.when(nb >= top)
    def _():
        arm(0, top_rows, -(-rows_cap // nc))
        for blocks in range(1, rows_cap // MOE_BLOCK - top + 1):
            @pl.when(nb - top == blocks)
            def _():
                arm(top_rows, blocks * MOE_BLOCK, 0)

    @pl.when(nb < top)
    def _():
        for blocks in MOE_ARM_BLOCKS[1:]:
            @pl.when((nb & blocks) != 0)
            def _():
                done = (nb // (2 * blocks)) * (2 * blocks)
                arm(pl.multiple_of(done * MOE_BLOCK, blocks * MOE_BLOCK), blocks * MOE_BLOCK,
                    blocks * MOE_BLOCK // MOE_GATHER_SHARE)

    @pl.when(c == nc - 1)
    def _():
        def rest(i, carry):
            gather_row(base_next, i)
            return carry
        lax.fori_loop(jnp.minimum(cnt_ref[0], n_next), n_next, rest, 0)

        @pl.when(g == ng - 1)
        def _():
            wait_rows(yst_ref, n, osem)
            yst_ref[pl.ds(0, MOE_BLOCK), :] = jnp.zeros((MOE_BLOCK, yst_ref.shape[-1]), F32)
            first = (row0 + n) // MOE_BLOCK
            n_fill = ys_hbm.shape[0] // MOE_BLOCK - first

            def fill(i, carry):
                dst0 = pl.multiple_of((first + i) * MOE_BLOCK, MOE_BLOCK)
                pltpu.make_async_copy(yst_ref.at[pl.ds(0, MOE_BLOCK)], ys_hbm.at[pl.ds(dst0, MOE_BLOCK)], osem).start()
                return carry
            lax.fori_loop(0, n_fill, fill, 0)

            def drain(i, carry):
                pltpu.make_async_copy(yst_ref.at[pl.ds(0, MOE_BLOCK)], yst_ref.at[pl.ds(0, MOE_BLOCK)], osem).wait()
                return carry
            lax.fori_loop(0, n_fill, drain, 0)


def moe_experts(u, tok_sorted, g_exp, g_row, g_n, n_groups, layer, w_gu, b_gu, w_down, b_down, p_rows, cfg):
    n_layers, e, d, f2 = w_gu.shape
    assert u.shape[1] * 2 == d and u.dtype == I32
    f = f2 // 2
    tf = min(cfg.moe_tf, f)
    nc = f // tf
    rows = cfg.moe_rows
    assert rows % MOE_BLOCK == 0

    grid_spec = pltpu.PrefetchScalarGridSpec(
        num_scalar_prefetch=4,
        grid=(n_groups, nc),
        in_specs=[
            pl.BlockSpec(memory_space=pl.ANY),
            pl.BlockSpec((None, None, d, tf), lambda g, c, tk, ge, gr, gn: (layer, ge[g], 0, c)),
            pl.BlockSpec((None, None, d, tf), lambda g, c, tk, ge, gr, gn: (layer, ge[g], 0, nc + c)),
            pl.BlockSpec((None, None, 1, tf), lambda g, c, tk, ge, gr, gn: (layer, ge[g], 0, c)),
            pl.BlockSpec((None, None, 1, tf), lambda g, c, tk, ge, gr, gn: (layer, ge[g], 0, nc + c)),
            pl.BlockSpec((None, None, tf, d), lambda g, c, tk, ge, gr, gn: (layer, ge[g], c, 0)),
            pl.BlockSpec((None, None, 1, d), lambda g, c, tk, ge, gr, gn: (layer, ge[g], 0, 0)),
        ],
        out_specs=pl.BlockSpec(memory_space=pl.ANY),
        scratch_shapes=[
            pltpu.VMEM((rows, d // 2), I32), pltpu.VMEM((rows, d), BF16), pltpu.VMEM((rows, d), F32),
            pltpu.SMEM((1,), I32), pltpu.SemaphoreType.DMA(()), pltpu.SemaphoreType.DMA(()),
        ],
    )
    est = rows * d * 8 + 2 * 3 * d * tf * 4 + 3 * d * tf * 2
    return pl.pallas_call(
        _moe_kernel,
        grid_spec=grid_spec,
        out_shape=jax.ShapeDtypeStruct((p_rows, d), F32),
        compiler_params=_cparams(("arbitrary", "arbitrary"), est),
        name="moe_experts",
    )(tok_sorted, g_exp, g_row, g_n,
      u, w_gu, w_gu, b_gu.reshape(n_layers, e, 1, f2), b_gu.reshape(n_layers, e, 1, f2), w_down,
      b_down.reshape(n_layers, e, 1, d))


def _combine_kernel(pos_ref, ys_hbm, h_ref, gate_ref, g2_ref, fg_ref, o_ref, rows_ref, sem, *, tm, top_k, final_norm):
    i = pl.program_id(0)
    n_tiles = pl.num_programs(0)
    n_rows = tm * top_k

    def start_tile(tile, slot, cond):
        base = tile * n_rows
        for j in range(n_rows):
            def issue(j=j):
                src = pos_ref[base + j]
                pltpu.make_async_copy(ys_hbm.at[pl.ds(src, 1)], rows_ref.at[slot, pl.ds(j, 1)],
                                      sem.at[slot]).start(priority=j % 2)
            if cond is None:
                issue()
            else:
                pl.when(cond)(issue)

    def finish(slot):
        pltpu.make_async_copy(rows_ref.at[slot], rows_ref.at[slot], sem.at[slot]).wait()
        gates = gate_ref[...]
        acc = rows_ref[slot, pl.ds(0, tm), :] * gates[:, 0:1]
        for k in range(1, top_k):
            acc = acc + rows_ref[slot, pl.ds(k * tm, tm), :] * gates[:, k:k + 1]
        out = h_ref[...] + g2_ref[...] * acc
        if final_norm:
            out = _rmsnorm_f32(out, fg_ref[...])
        o_ref[...] = out

    @pl.when(i == 0)
    def _():
        start_tile(0, 0, None)

    for parity in range(2):
        @pl.when(i % 2 == parity)
        def _():
            start_tile(i + 1, 1 - parity, i + 1 < n_tiles)
            finish(parity)


def moe_combine(ys, pos, h, gates, g2, final_g, cfg, *, final_norm):
    t, d = h.shape
    top_k = cfg.top_k
    tm = min(cfg.comb_tm, t)
    assert (tm * top_k) % 64 == 0
    pos = pos.reshape(t // tm, tm, top_k).transpose(0, 2, 1).reshape(-1)
    vec = pl.BlockSpec((1, d), lambda i, p: (0, 0))
    grid_spec = pltpu.PrefetchScalarGridSpec(
        num_scalar_prefetch=1,
        grid=(t // tm,),
        in_specs=[pl.BlockSpec(memory_space=pl.ANY),
                  pl.BlockSpec((tm, d), lambda i, p: (i, 0)),
                  pl.BlockSpec((tm, top_k), lambda i, p: (i, 0)), vec, vec],
        out_specs=pl.BlockSpec((tm, d), lambda i, p: (i, 0)),
        scratch_shapes=[pltpu.VMEM((2, tm * top_k, d), F32), pltpu.SemaphoreType.DMA((2,))],
    )
    return pl.pallas_call(
        functools.partial(_combine_kernel, tm=tm, top_k=top_k, final_norm=final_norm),
        grid_spec=grid_spec,
        out_shape=jax.ShapeDtypeStruct((t, d), F32),
        compiler_params=_cparams(("arbitrary",), 2 * tm * top_k * d * 4 + 6 * tm * d * 4),
        name="moe_combine",
    )(pos, ys, h, gates, g2, final_g)


def moe_layer(h, g, sc, sh, g2, w_router, b_router, layer, w_gu, b_gu, w_down, b_down, final_g, cfg, *, final_norm):
    u, idx, gates = router(h, g, sc, sh, w_router, b_router.reshape(1, -1), cfg)
    tok_sorted, pos, g_exp, g_row, g_n, n_groups, p_rows = _routing_tables(idx, cfg)
    ys = moe_experts(u, tok_sorted, g_exp, g_row, g_n, n_groups, layer, w_gu, b_gu, w_down, b_down, p_rows, cfg)
    return moe_combine(ys, pos, h, gates, g2, final_g, cfg, final_norm=final_norm)


def _diff_mixer(h, pos, g, sc, sh, gate, w_qkv, lq1, lk1, lq2, lk2, sub_g, w_o, rel_bias, layer_idx, cfg):
    heads, hd = cfg.diff_heads, cfg.diff_head_dim
    qk_w = heads * 2 * hd
    lam_init = 0.8 - 0.6 * math.exp(-0.3 * (layer_idx - 1))
    lam = (jnp.exp(jnp.sum(lq1 * lk1)) - jnp.exp(jnp.sum(lq2 * lk2)) + lam_init).reshape(1).astype(F32)
    n = w_qkv.shape[1]
    col_scale = jnp.concatenate([jnp.full((qk_w,), hd ** -0.5 * LOG2E, F32), jnp.ones((n - qk_w,), F32)]).reshape(1, n)
    qkv = norm_linear(h, 0, h.shape[1], g, sc, sh, w_qkv.astype(BF16), col_scale, BF16, cfg,
                      modulate=True, name="diff_qkv_proj")
    t = min(cfg.diff_t, h.shape[0])
    tiles = diff_bias_tiles(rel_bias, heads, t)
    kinds = _block_kinds(pos, t, REL_FAR, toeplitz=True)
    o = diff_attention(qkv, tiles, kinds, pos, rel_bias, lam, sub_g.reshape(1, -1), 1.0 - lam_init, cfg)
    return linear_residual(o, w_o.astype(BF16), h, gate, cfg, name="diff_out_proj")


def _mla_mixer(h, pos, g, sc, sh, gate, w_in, q_norm_g, kv_norm_g, w_uq, w_ukv, w_o, cfg):
    d = h.shape[1]
    heads, nope, rope, qr, kvr = cfg.mla_heads, cfg.mla_nope, cfg.mla_rope, cfg.mla_q_rank, cfg.mla_kv_rank
    assert nope == LANE and cfg.mla_v == LANE and rope <= LANE and qr % LANE == 0 and kvr == qr
    w_in_p = jnp.concatenate([w_in, jnp.zeros((d, LANE - rope), w_in.dtype)], axis=1).astype(BF16)
    ones = lambda n: jnp.ones((1, n), F32)
    z = norm_linear(h, 0, d, g, sc, sh, w_in_p, ones(w_in_p.shape[1]), F32, cfg, modulate=True, name="mla_down_proj")
    w_q = w_uq.reshape(qr, heads, nope + rope)
    w_q = jnp.concatenate([w_q, jnp.zeros((qr, heads, 2 * LANE - nope - rope), w_uq.dtype)], axis=2)
    w_q = w_q.reshape(qr, heads * 2 * LANE).astype(BF16)
    zeros_k = jnp.zeros((1, qr), F32)
    q_scale = jnp.full((1, heads * 2 * LANE), (nope + rope) ** -0.5 * LOG2E, F32)
    q_cat = norm_linear(z, 0, qr, q_norm_g.reshape(1, -1), zeros_k, zeros_k, w_q, q_scale, BF16, cfg,
                        modulate=False, name="mla_q_up_proj")
    kv = norm_linear(z, 1, kvr, kv_norm_g.reshape(1, -1), zeros_k, zeros_k, w_ukv.astype(BF16),
                     ones(w_ukv.shape[1]), BF16, cfg, modulate=False, name="mla_kv_up_proj")
    tables = _rope_tables(pos, rope)
    kr_rot = rope_rows(z, (qr + kvr) // LANE, tables, cfg)
    t = min(cfg.mla_t, h.shape[0])
    kinds = _block_kinds(pos, t, 0, toeplitz=True)
    o = mla_attention(q_cat, kv, kr_rot, tables, kinds, pos, cfg)
    return linear_residual(o, w_o.astype(BF16), h, gate, cfg, name="mla_out_proj")


def _forward(cfg, x, c, positions, ada_w, ada_b, norm1_g, norm2_g, final_g, rel_bias,
             diff_w_qkv, diff_lq1, diff_lk1, diff_lq2, diff_lk2, diff_sub_g, diff_w_o,
             mla_w_in, mla_q_norm_g, mla_kv_norm_g, mla_w_uq, mla_w_ukv, mla_w_o,
             router_w, router_b, exp_w_gu, exp_b_gu, exp_w_down, exp_b_down):
    b, s, d = x.shape
    assert b == 1, "kernels are written for a single sequence"
    h = x.reshape(s, d)
    pos = positions.reshape(s).astype(I32)
    mod = ada_modulation(c.reshape(d, 1), ada_w, ada_b, cfg)
    fg = final_g.reshape(1, d)
    for i in range(cfg.depth):
        sh1, sc1, g1, sh2, sc2, g2 = [mod[i, :, j * d:(j + 1) * d] for j in range(6)]
        n1 = norm1_g[i].reshape(1, d)
        j = i // 2
        if i % 2 == 0:
            h = _diff_mixer(h, pos, n1, sc1, sh1, g1, diff_w_qkv[j], diff_lq1[j], diff_lk1[j], diff_lq2[j],
                            diff_lk2[j], diff_sub_g[j], diff_w_o[j], rel_bias, i + 1, cfg)
        else:
            h = _mla_mixer(h, pos, n1, sc1, sh1, g1, mla_w_in[j], mla_q_norm_g[j], mla_kv_norm_g[j],
                           mla_w_uq[j], mla_w_ukv[j], mla_w_o[j], cfg)
        h = moe_layer(h, norm2_g[i].reshape(1, d), sc2, sh2, g2, router_w[i], router_b[i], i, exp_w_gu, exp_b_gu,
                      exp_w_down, exp_b_down, fg, cfg, final_norm=(i == cfg.depth - 1))
    return h.reshape(b, s, d)


def kernel(x, c, positions, ada_w, ada_b, norm1_g, norm2_g, final_g, rel_bias, diff_w_qkv, diff_lq1, diff_lk1, diff_lq2, diff_lk2, diff_sub_g, diff_w_o, mla_w_in, mla_q_norm_g, mla_kv_norm_g, mla_w_uq, mla_w_ukv, mla_w_o, router_w, router_b, exp_w_gu, exp_b_gu, exp_w_down, exp_b_down):
    return _forward(Cfg(), x, c, positions, ada_w, ada_b, norm1_g, norm2_g, final_g, rel_bias,
                    diff_w_qkv, diff_lq1, diff_lk1, diff_lq2, diff_lk2, diff_sub_g, diff_w_o,
                    mla_w_in, mla_q_norm_g, mla_kv_norm_g, mla_w_uq, mla_w_ukv, mla_w_o,
                    router_w, router_b, exp_w_gu, exp_b_gu, exp_w_down, exp_b_down)
```
